```python
import math
import jax, jax.numpy as jnp
from jax import lax
import numpy as np

D_MODEL = 1024
BATCH = 1
SEQ = 16384
DEPTH = 4

HEAD_DIM = 64
BLOCK_Q = 128
NEG_INF = -1e30
RMS_EPS = 1e-6
MLA_HEADS = 8
MLA_NOPE_DIM = 64
MLA_ROPE_DIM = 32
MLA_V_DIM = 64
MLA_Q_RANK = 256
MLA_KV_RANK = 128
ROPE_THETA = 10000.0
FOX_HEADS = 8
DIL_GROUPS = ((128, 1), (512, 4), (2048, 16))
DIL_HEADS = 4
DIL_WIDTH = 3 * DIL_HEADS * HEAD_DIM
NSA_HEADS = 4
NSA_CMP_LEN = 32
NSA_CMP_STRIDE = 16
NSA_CMP_HIDDEN = 256
NSA_SEL_BLOCK = 64
NSA_TOPK = 16
NSA_WINDOW = 512
NSA_FORCE_SCORE = 1e9
T5_BUCKETS = 32
T5_MAX_DIST = 2048
T5_HEADS = 3 * DIL_HEADS + NSA_HEADS
D_FF = 4 * D_MODEL
PLE_DIM = 256

EVEN_IN_SIZES = (MLA_Q_RANK, MLA_KV_RANK, MLA_ROPE_DIM,
                 FOX_HEADS * HEAD_DIM, FOX_HEADS * HEAD_DIM, FOX_HEADS * HEAD_DIM, FOX_HEADS)
EVEN_IN_WIDTH = MLA_Q_RANK + MLA_KV_RANK + MLA_ROPE_DIM + 3 * FOX_HEADS * HEAD_DIM + FOX_HEADS
EVEN_OUT_WIDTH = MLA_HEADS * MLA_V_DIM + FOX_HEADS * HEAD_DIM
ODD_IN_SIZES = (DIL_WIDTH, DIL_WIDTH, DIL_WIDTH, NSA_HEADS * HEAD_DIM,
                HEAD_DIM, HEAD_DIM, HEAD_DIM, HEAD_DIM, HEAD_DIM, HEAD_DIM, NSA_HEADS * 3)
ODD_IN_WIDTH = 3 * DIL_WIDTH + NSA_HEADS * HEAD_DIM + 6 * HEAD_DIM + NSA_HEADS * 3
ODD_OUT_WIDTH = DIL_HEADS * HEAD_DIM + NSA_HEADS * HEAD_DIM
N_EVEN = (DEPTH + 1) // 2
N_ODD = DEPTH // 2

kernel_name = "hybrid_mla_fox_dilated_nsa_trunk"


def rms_norm(x, g):
    x32 = x.astype(jnp.float32)
    y = x32 * lax.rsqrt(jnp.mean(x32 * x32, axis=-1, keepdims=True) + RMS_EPS)
    return (y * g.astype(jnp.float32)).astype(x.dtype)


def split_cols(y, sizes):
    idx, acc = [], 0
    for s in sizes[:-1]:
        acc += s
        idx.append(acc)
    return jnp.split(y, idx, axis=-1)


def t5_bucket(dist):
    n = jnp.maximum(dist, 0)
    max_exact = T5_BUCKETS // 2
    ratio = jnp.log(jnp.maximum(n, 1).astype(jnp.float32) / max_exact) / math.log(T5_MAX_DIST / max_exact)
    large = jnp.minimum(max_exact + (ratio * (T5_BUCKETS - max_exact)).astype(jnp.int32), T5_BUCKETS - 1)
    return jnp.where(n < max_exact, n, large)


def apply_rope(x, cos, sin):
    x1, x2 = jnp.split(x, 2, axis=-1)
    return jnp.concatenate([x1 * cos - x2 * sin, x2 * cos + x1 * sin], axis=-1).astype(x.dtype)


def dense_causal_attention(q, k, v, scale, cum_log_forget=None):
    B, S, H, _ = q.shape
    Dv = v.shape[-1]
    s_pos = jnp.arange(S)
    cum_k = None if cum_log_forget is None else jnp.transpose(cum_log_forget, (0, 2, 1))

    def one_block(i):
        q0 = i * BLOCK_Q
        t = q0 + jnp.arange(BLOCK_Q)
        qb = lax.dynamic_slice_in_dim(q, q0, BLOCK_Q, axis=1)
        logits = jnp.einsum('bqhd,bkhd->bhqk', qb, k, preferred_element_type=jnp.float32) * scale
        if cum_k is not None:
            cq = lax.dynamic_slice_in_dim(cum_k, q0, BLOCK_Q, axis=2)
            logits = logits + cq[..., None] - cum_k[:, :, None, :]
        logits = jnp.where(s_pos[None, :] <= t[:, None], logits, NEG_INF)
        p = jax.nn.softmax(logits, axis=-1)
        return jnp.einsum('bhqk,bkhd->bqhd', p.astype(v.dtype), v)

    out = lax.map(one_block, jnp.arange(S // BLOCK_Q))
    return jnp.moveaxis(out, 0, 1).reshape(B, S, H, Dv)


def dilated_attention(q, k, v, bias_table):
    B, S, G, H, Dh = q.shape
    scale = Dh ** -0.5
    k_groups = [k[:, :, g] for g in range(G)]
    v_groups = [v[:, :, g] for g in range(G)]
    offsets, group_bias = [], []
    for g, (w, d) in enumerate(DIL_GROUPS):
        offs = d * jnp.arange(w // d + 1)
        offsets.append(offs)
        group_bias.append(bias_table[t5_bucket(offs), g * H:(g + 1) * H].T)

    def one_block(i):
        q0 = i * BLOCK_Q
        t = q0 + jnp.arange(BLOCK_Q)
        qb = lax.dynamic_slice_in_dim(q, q0, BLOCK_Q, axis=1)
        outs, lses = [], []
        for g in range(G):
            idx = t[:, None] - offsets[g][None, :]
            valid = idx >= 0
            idx = jnp.maximum(idx, 0)
            kg = k_groups[g][:, idx]
            vg = v_groups[g][:, idx]
            logits = jnp.einsum('bqhd,bqkhd->bhqk', qb[:, :, g], kg, preferred_element_type=jnp.float32) * scale
            logits = logits + group_bias[g][None, :, None, :].astype(jnp.float32)
            logits = jnp.where(valid[None, None], logits, NEG_INF)
            lse = jax.nn.logsumexp(logits, axis=-1)
            p = jnp.exp(logits - lse[..., None])
            outs.append(jnp.einsum('bhqk,bqkhd->bqhd', p.astype(v.dtype), vg))
            lses.append(lse)
        wts = jax.nn.softmax(jnp.stack(lses, axis=0), axis=0)
        return jnp.einsum('gbhq,gbqhd->bqhd', wts.astype(v.dtype), jnp.stack(outs, axis=0))

    out = lax.map(one_block, jnp.arange(S // BLOCK_Q))
    return jnp.moveaxis(out, 0, 1).reshape(B, S, H, Dh)


def nsa_attention(q, k_cmp_src, v_cmp_src, k_sel, v_sel, k_win, v_win, gates, bias_table,
                  cmp_pos_k, cmp_pos_v, w1_k, w2_k, w1_v, w2_v):
    B, S, H, Dh = q.shape
    scale = Dh ** -0.5
    n_cmp = (S - NSA_CMP_LEN) // NSA_CMP_STRIDE + 1
    n_sel = S // NSA_SEL_BLOCK
    topk = min(NSA_TOPK, n_sel)
    cmp_start = jnp.arange(n_cmp) * NSA_CMP_STRIDE
    blk_idx = cmp_start[:, None] + jnp.arange(NSA_CMP_LEN)[None, :]

    def compress(src, pos, w1, w2):
        blocks = src[:, blk_idx] + pos
        return jax.nn.gelu(blocks.reshape(B, n_cmp, NSA_CMP_LEN * Dh) @ w1) @ w2

    kc = compress(k_cmp_src, cmp_pos_k, w1_k, w2_k)
    vc = compress(v_cmp_src, cmp_pos_v, w1_v, w2_v)
    cmp_end = cmp_start + NSA_CMP_LEN - 1
    sel_start = jnp.arange(n_sel) * NSA_SEL_BLOCK
    overlap = ((cmp_start[:, None] < sel_start[None, :] + NSA_SEL_BLOCK) &
               (cmp_start[:, None] + NSA_CMP_LEN > sel_start[None, :])).astype(jnp.float32)
    k_win_pad = jnp.pad(k_win, ((0, 0), (NSA_WINDOW, 0), (0, 0)))
    v_win_pad = jnp.pad(v_win, ((0, 0), (NSA_WINDOW, 0), (0, 0)))
    win_len = BLOCK_Q + NSA_WINDOW
    gather_rows = jax.vmap(lambda a, i: a[i])

    def one_block(i):
        q0 = i * BLOCK_Q
        t = q0 + jnp.arange(BLOCK_Q)
        qb = lax.dynamic_slice_in_dim(q, q0, BLOCK_Q, axis=1)
        gb = lax.dynamic_slice_in_dim(gates, q0, BLOCK_Q, axis=1)
        c_mask = cmp_end[None, :] <= t[:, None]
        logits = jnp.einsum('bqhd,bnd->bhqn', qb, kc, preferred_element_type=jnp.float32) * scale
        p_cmp = jax.nn.softmax(jnp.where(c_mask, logits, NEG_INF), axis=-1) * c_mask
        o_cmp = jnp.einsum('bhqn,bnd->bqhd', p_cmp.astype(vc.dtype), vc)
        importance = jnp.einsum('bhqn,nj->bqj', p_cmp, overlap)
        tb = t // NSA_SEL_BLOCK
        j = jnp.arange(n_sel)
        forced = (j[None, :] == 0) | (j[None, :] == tb[:, None]) | (j[None, :] == tb[:, None] - 1)
        causal_blk = sel_start[None, :] <= t[:, None]
        score = jnp.where(forced, NSA_FORCE_SCORE, jnp.where(causal_blk, importance, -NSA_FORCE_SCORE))
        _, sel = lax.top_k(score, topk)
        tok = (sel[..., None] * NSA_SEL_BLOCK + jnp.arange(NSA_SEL_BLOCK)).reshape(B, BLOCK_Q, topk * NSA_SEL_BLOCK)
        ks = gather_rows(k_sel, tok)
        vs = gather_rows(v_sel, tok)
        dist = t[None, :, None] - tok
        logits = jnp.einsum('bqhd,bqkd->bhqk', qb, ks, preferred_element_type=jnp.float32) * scale
        logits = logits + jnp.moveaxis(bias_table[t5_bucket(dist)], -1, 1).astype(jnp.float32)
        logits = jnp.where((dist >= 0)[:, None], logits, NEG_INF)
        o_sel = jnp.einsum('bhqk,bqkd->bqhd', jax.nn.softmax(logits, axis=-1).astype(vs.dtype), vs)
        kw = lax.dynamic_slice_in_dim(k_win_pad, q0, win_len, axis=1)
        vw = lax.dynamic_slice_in_dim(v_win_pad, q0, win_len, axis=1)
        s = q0 - NSA_WINDOW + jnp.arange(win_len)
        dist_w = t[:, None] - s[None, :]
        w_mask = (dist_w >= 0) & (dist_w < NSA_WINDOW) & (s[None, :] >= 0)
        logits = jnp.einsum('bqhd,bkd->bhqk', qb, kw, preferred_element_type=jnp.float32) * scale
        logits = logits + jnp.transpose(bias_table[t5_bucket(dist_w)], (2, 0, 1))[None].astype(jnp.float32)
        logits = jnp.where(w_mask[None, None], logits, NEG_INF)
        o_win = jnp.einsum('bhqk,bkd->bqhd', jax.nn.softmax(logits, axis=-1).astype(vw.dtype), vw)
        return gb[..., 0:1] * o_cmp + gb[..., 1:2] * o_sel + gb[..., 2:3] * o_win

    out = lax.map(one_block, jnp.arange(S // BLOCK_Q))
    return jnp.moveaxis(out, 0, 1).reshape(B, S, H, Dh)


def even_mixer(hn, cos, sin, w_in, q_norm, w_uq, kv_norm, w_ukv, forget_bias, w_out):
    B, S, _ = hn.shape
    c_q, c_kv, k_r, fq, fk, fv, f_logit = split_cols(hn @ w_in, EVEN_IN_SIZES)
    q = (rms_norm(c_q, q_norm) @ w_uq).reshape(B, S, MLA_HEADS, MLA_NOPE_DIM + MLA_ROPE_DIM)
    q_nope, q_rope = jnp.split(q, [MLA_NOPE_DIM], axis=-1)
    q_rope = apply_rope(q_rope, cos[:, :, None], sin[:, :, None])
    kv = (rms_norm(c_kv, kv_norm) @ w_ukv).reshape(B, S, MLA_HEADS, MLA_NOPE_DIM + MLA_V_DIM)
    k_nope, v_mla = jnp.split(kv, [MLA_NOPE_DIM], axis=-1)
    k_r = apply_rope(k_r, cos, sin)
    q_mla = jnp.concatenate([q_nope, q_rope], axis=-1)
    k_mla = jnp.concatenate([k_nope, jnp.broadcast_to(k_r[:, :, None], (B, S, MLA_HEADS, MLA_ROPE_DIM))], axis=-1)
    o_mla = dense_causal_attention(q_mla, k_mla, v_mla, (MLA_NOPE_DIM + MLA_ROPE_DIM) ** -0.5)
    log_f = jax.nn.log_sigmoid((f_logit + forget_bias).astype(jnp.float32))
    cum = jnp.cumsum(log_f, axis=1)
    shp = (B, S, FOX_HEADS, HEAD_DIM)
    o_fox = dense_causal_attention(fq.reshape(shp), fk.reshape(shp), fv.reshape(shp), HEAD_DIM ** -0.5, cum)
    o = jnp.concatenate([o_mla.reshape(B, S, -1), o_fox.reshape(B, S, -1)], axis=-1)
    return o @ w_out


def odd_mixer(hn, t5_bias, w_in, cmp_pos_k, cmp_pos_v, cmp_w1_k, cmp_w2_k, cmp_w1_v, cmp_w2_v, w_out):
    B, S, _ = hn.shape
    (dq, dk, dv, nq, k_cmp_src, v_cmp_src, k_sel, v_sel, k_win, v_win,
     gate_logit) = split_cols(hn @ w_in, ODD_IN_SIZES)
    G = len(DIL_GROUPS)
    shp = (B, S, G, DIL_HEADS, HEAD_DIM)
    o_dil = dilated_attention(dq.reshape(shp), dk.reshape(shp), dv.reshape(shp), t5_bias[:, :G * DIL_HEADS])
    gates = jax.nn.sigmoid(gate_logit.reshape(B, S, NSA_HEADS, 3))
    o_nsa = nsa_attention(nq.reshape(B, S, NSA_HEADS, HEAD_DIM), k_cmp_src, v_cmp_src, k_sel, v_sel,
                          k_win, v_win, gates, t5_bias[:, G * DIL_HEADS:],
                          cmp_pos_k, cmp_pos_v, cmp_w1_k, cmp_w2_k, cmp_w1_v, cmp_w2_v)
    o = jnp.concatenate([o_dil.reshape(B, S, -1), o_nsa.reshape(B, S, -1)], axis=-1)
    return o @ w_out


def setup_inputs(seed: int = 0) -> dict:
    key = jax.random.key(seed)
    ks = iter(jax.random.split(key, 40))

    def dense(shape, fan_in):
        return jax.random.normal(next(ks), shape, jnp.float32) * fan_in ** -0.5

    def gain(shape):
        return 1.0 + 0.05 * jax.random.normal(next(ks), shape, jnp.float32)

    x = jax.random.normal(next(ks), (BATCH, SEQ, D_MODEL), jnp.float32)
    p = jax.random.normal(next(ks), (DEPTH, BATCH, SEQ, PLE_DIM), jnp.float32)
    offset = jax.random.randint(next(ks), (BATCH, 1), 0, 1024, jnp.int32)
    positions = offset + jnp.arange(SEQ, dtype=jnp.int32)[None, :]
    return {
        "x": x,
        "p": p,
        "positions": positions,
        "t5_bias": 0.5 * jax.random.normal(next(ks), (T5_BUCKETS, T5_HEADS), jnp.float32),
        "ev_w_in": dense((N_EVEN, D_MODEL, EVEN_IN_WIDTH), D_MODEL),
        "ev_q_norm": gain((N_EVEN, MLA_Q_RANK)),
        "ev_w_uq": dense((N_EVEN, MLA_Q_RANK, MLA_HEADS * (MLA_NOPE_DIM + MLA_ROPE_DIM)), MLA_Q_RANK),
        "ev_kv_norm": gain((N_EVEN, MLA_KV_RANK)),
        "ev_w_ukv": dense((N_EVEN, MLA_KV_RANK, MLA_HEADS * (MLA_NOPE_DIM + MLA_V_DIM)), MLA_KV_RANK),
        "ev_forget_bias": jax.random.uniform(next(ks), (N_EVEN, FOX_HEADS), jnp.float32, 2.0, 5.0),
        "ev_w_out": dense((N_EVEN, EVEN_OUT_WIDTH, D_MODEL), EVEN_OUT_WIDTH),
        "od_w_in": dense((N_ODD, D_MODEL, ODD_IN_WIDTH), D_MODEL),
        "od_cmp_pos_k": 0.1 * jax.random.normal(next(ks), (N_ODD, NSA_CMP_LEN, HEAD_DIM), jnp.float32),
        "od_cmp_pos_v": 0.1 * jax.random.normal(next(ks), (N_ODD, NSA_CMP_LEN, HEAD_DIM), jnp.float32),
        "od_cmp_w1_k": dense((N_ODD, NSA_CMP_LEN * HEAD_DIM, NSA_CMP_HIDDEN), NSA_CMP_LEN * HEAD_DIM),
        "od_cmp_w2_k": dense((N_ODD, NSA_CMP_HIDDEN, HEAD_DIM), NSA_CMP_HIDDEN),
        "od_cmp_w1_v": dense((N_ODD, NSA_CMP_LEN * HEAD_DIM, NSA_CMP_HIDDEN), NSA_CMP_LEN * HEAD_DIM),
        "od_cmp_w2_v": dense((N_ODD, NSA_CMP_HIDDEN, HEAD_DIM), NSA_CMP_HIDDEN),
        "od_w_out": dense((N_ODD, ODD_OUT_WIDTH, D_MODEL), ODD_OUT_WIDTH),
        "norm_mix_pre": gain((DEPTH, D_MODEL)),
        "norm_mix_post": gain((DEPTH, D_MODEL)),
        "norm_mlp_pre": gain((DEPTH, D_MODEL)),
        "norm_mlp_post": gain((DEPTH, D_MODEL)),
        "w_mlp_up": dense((DEPTH, D_MODEL, D_FF), D_MODEL),
        "w_mlp_down": dense((DEPTH, D_FF, D_MODEL), D_FF),
        "ple_norm": gain((DEPTH, D_MODEL)),
        "w_ple_gate": dense((DEPTH, D_MODEL, D_MODEL), D_MODEL),
        "w_ple_proj": dense((DEPTH, PLE_DIM, D_MODEL), PLE_DIM),
    }


def reference(x, p, positions, t5_bias, ev_w_in, ev_q_norm, ev_w_uq, ev_kv_norm, ev_w_ukv, ev_forget_bias,
              ev_w_out, od_w_in, od_cmp_pos_k, od_cmp_pos_v, od_cmp_w1_k, od_cmp_w2_k, od_cmp_w1_v, od_cmp_w2_v,
              od_w_out, norm_mix_pre, norm_mix_post, norm_mlp_pre, norm_mlp_post, w_mlp_up, w_mlp_down,
              ple_norm, w_ple_gate, w_ple_proj):
    inv_freq = ROPE_THETA ** (-jnp.arange(0, MLA_ROPE_DIM, 2, dtype=jnp.float32) / MLA_ROPE_DIM)
    angles = positions.astype(jnp.float32)[..., None] * inv_freq
    cos, sin = jnp.cos(angles), jnp.sin(angles)
    h = x
    for i in range(DEPTH):
        j = i // 2
        hn = rms_norm(h, norm_mix_pre[i])
        if i % 2 == 0:
            mix = even_mixer(hn, cos, sin, ev_w_in[j], ev_q_norm[j], ev_w_uq[j], ev_kv_norm[j], ev_w_ukv[j],
                             ev_forget_bias[j], ev_w_out[j])
        else:
            mix = odd_mixer(hn, t5_bias, od_w_in[j], od_cmp_pos_k[j], od_cmp_pos_v[j], od_cmp_w1_k[j],
                            od_cmp_w2_k[j], od_cmp_w1_v[j], od_cmp_w2_v[j], od_w_out[j])
        h = h + rms_norm(mix, norm_mix_post[i])
        hn = rms_norm(h, norm_mlp_pre[i])
        m = jnp.square(jax.nn.relu(hn @ w_mlp_up[i])) @ w_mlp_down[i]
        h = h + rms_norm(m, norm_mlp_post[i])
        gate = jax.nn.sigmoid(rms_norm(h, ple_norm[i]) @ w_ple_gate[i])
        h = h + gate * (p[i] @ w_ple_proj[i])
    return h
```

```python
import functools
import math

import numpy as np
import jax
import jax.numpy as jnp
from jax import lax
from jax.experimental import pallas as pl
from jax.experimental.pallas import tpu as pltpu

F32 = jnp.float32
BF16 = jnp.bfloat16

HEAD_DIM = 64
RMS_EPS = 1e-6
NEG_INF = -1e30
SEL_NEG = -(2.0 ** 99)

MLA_HEADS = 8
MLA_NOPE_DIM = 64
MLA_ROPE_DIM = 32
MLA_V_DIM = 64
MLA_Q_RANK = 256
MLA_KV_RANK = 128
MLA_QK_DIM = MLA_NOPE_DIM + MLA_ROPE_DIM
ROPE_THETA = 10000.0
FOX_HEADS = 8
DIL_GROUPS = ((128, 1), (512, 4), (2048, 16))
DIL_HEADS = 4
DIL_GW = DIL_HEADS * HEAD_DIM
NSA_HEADS = 4
NSA_W = NSA_HEADS * HEAD_DIM
NSA_CMP_LEN = 32
NSA_CMP_STRIDE = 16
NSA_SEL_BLOCK = 64
NSA_TOPK = 16
NSA_WINDOW = 512
NSA_FORCE_SCORE = 1e9
NSA_SEL_GROUP_BLOCKS = 128
T5_BUCKETS = 32
T5_MAX_DIST = 2048

EVEN_IN_PAD = 2048
ODD_IN_PAD = 3840

VMEM_LIMIT_BYTES = 56 * 1024 * 1024


def _cparams(sem):
    return pltpu.CompilerParams(dimension_semantics=sem, vmem_limit_bytes=VMEM_LIMIT_BYTES)


def _rms(x, g):
    return x * lax.rsqrt(jnp.mean(x * x, axis=-1, keepdims=True) + RMS_EPS) * g


def _dot(a, b):
    return jnp.dot(a, b, preferred_element_type=F32)


def _dot_nt(a, b):
    return lax.dot_general(a, b, (((1,), (1,)), ((), ())), preferred_element_type=F32)


def _norm_matmul_kernel(h_ref, g_ref, w_ref, o_ref, hn_ref):
    @pl.when(pl.program_id(1) == 0)
    def _():
        hn_ref[...] = _rms(h_ref[...], g_ref[...]).astype(BF16)

    o_ref[...] = _dot(hn_ref[...], w_ref[...])


def norm_matmul(h, g, w, tm, tn):
    S, D = h.shape
    N = w.shape[1]
    return pl.pallas_call(
        _norm_matmul_kernel,
        grid=(S // tm, N // tn),
        in_specs=[pl.BlockSpec((tm, D), lambda i, j: (i, 0)),
                  pl.BlockSpec((1, D), lambda i, j: (0, 0)),
                  pl.BlockSpec((D, tn), lambda i, j: (0, j))],
        out_specs=pl.BlockSpec((tm, tn), lambda i, j: (i, j)),
        out_shape=jax.ShapeDtypeStruct((S, N), F32),
        scratch_shapes=[pltpu.VMEM((tm, D), BF16)],
        compiler_params=_cparams(("parallel", "arbitrary")),
    )(h, g, w)


def _even_prep_kernel(y_ref, wq_ref, wqs_ref, wk_ref, e_ref, wv_ref, qn_ref, kvn_ref, fb_ref,
                      cq_ref, sq_ref, ck_ref, sk_ref,
                      qm_ref, km_ref, vm_ref, fq_ref, fk_ref, fv_ref, lf_ref):
    y = y_ref[...]
    c_q = y[:, 0:MLA_Q_RANK]
    c_kv = y[:, MLA_Q_RANK:MLA_Q_RANK + MLA_KV_RANK]
    o = MLA_Q_RANK + MLA_KV_RANK
    k_r = y[:, o:o + MLA_ROPE_DIM]
    k_rs = y[:, o + MLA_ROPE_DIM:o + 2 * MLA_ROPE_DIM]
    f_logit = y[:, o + 2 * MLA_ROPE_DIM:o + 2 * MLA_ROPE_DIM + FOX_HEADS]
    nq = _rms(c_q, qn_ref[...]).astype(BF16)
    nkv = _rms(c_kv, kvn_ref[...]).astype(BF16)
    k_rot = (k_r * ck_ref[...] + k_rs * sk_ref[...]).astype(BF16)
    k_rot96 = _dot(k_rot, e_ref[...])
    cq = cq_ref[...]
    sq = sq_ref[...]
    for h in range(MLA_HEADS):
        q = _dot(nq, wq_ref[h]) * cq + _dot(nq, wqs_ref[h]) * sq
        qm_ref[h] = q.astype(BF16)
        km_ref[h] = (_dot(nkv, wk_ref[h]) + k_rot96).astype(BF16)
        vm_ref[h] = _dot(nkv, wv_ref[h]).astype(BF16)
    base = 512
    fw = FOX_HEADS * HEAD_DIM
    for h in range(FOX_HEADS):
        lo = base + h * HEAD_DIM
        fq_ref[h] = (y[:, lo:lo + HEAD_DIM] * (HEAD_DIM ** -0.5)).astype(BF16)
        fk_ref[h] = y[:, lo + fw:lo + fw + HEAD_DIM].astype(BF16)
        fv_ref[h] = y[:, lo + 2 * fw:lo + 2 * fw + HEAD_DIM].astype(BF16)
    z = f_logit + fb_ref[...]
    lf_ref[...] = jnp.minimum(z, 0.0) - jnp.log1p(jnp.exp(-jnp.abs(z)))


def even_prep(y, wq, wqs, wk, e96, wv, qn, kvn, fb, cq, sq, ck, sk, tm):
    S = y.shape[0]
    full = lambda a: pl.BlockSpec(a.shape, lambda i: (0,) * a.ndim)
    rows = lambda w: pl.BlockSpec((tm, w), lambda i: (i, 0))
    heads = lambda n, w: pl.BlockSpec((n, tm, w), lambda i: (0, i, 0))
    out_shape = [jax.ShapeDtypeStruct((MLA_HEADS, S, MLA_QK_DIM), BF16),
                 jax.ShapeDtypeStruct((MLA_HEADS, S, MLA_QK_DIM), BF16),
                 jax.ShapeDtypeStruct((MLA_HEADS, S, MLA_V_DIM), BF16),
                 jax.ShapeDtypeStruct((FOX_HEADS, S, HEAD_DIM), BF16),
                 jax.ShapeDtypeStruct((FOX_HEADS, S, HEAD_DIM), BF16),
                 jax.ShapeDtypeStruct((FOX_HEADS, S, HEAD_DIM), BF16),
                 jax.ShapeDtypeStruct((S, FOX_HEADS), F32)]
    out_specs = [heads(MLA_HEADS, MLA_QK_DIM), heads(MLA_HEADS, MLA_QK_DIM), heads(MLA_HEADS, MLA_V_DIM),
                 heads(FOX_HEADS, HEAD_DIM), heads(FOX_HEADS, HEAD_DIM), heads(FOX_HEADS, HEAD_DIM),
                 rows(FOX_HEADS)]
    return pl.pallas_call(
        _even_prep_kernel,
        grid=(S // tm,),
        in_specs=[rows(EVEN_IN_PAD), full(wq), full(wqs), full(wk), full(e96), full(wv), full(qn), full(kvn),
                  full(fb), rows(MLA_QK_DIM), rows(MLA_QK_DIM), rows(MLA_ROPE_DIM), rows(MLA_ROPE_DIM)],
        out_specs=out_specs,
        out_shape=out_shape,
        compiler_params=_cparams(("parallel",)),
    )(y, wq, wqs, wk, e96, wv, qn, kvn, fb, cq, sq, ck, sk)


def _cumsum_kernel(x_ref, o_ref):
    x = x_ref[...]
    n = x.shape[1]
    lane = lax.broadcasted_iota(jnp.int32, x.shape, 1)
    shift = 1
    while shift < n:
        x = x + jnp.where(lane >= shift, pltpu.roll(x, shift, axis=1), 0.0)
        shift *= 2
    o_ref[...] = x


def cumsum_lanes(x):
    return pl.pallas_call(
        _cumsum_kernel,
        out_shape=jax.ShapeDtypeStruct(x.shape, F32),
        compiler_params=_cparams(None),
    )(x)


def _flash_kernel(qt_ref, kt_ref, *refs, hps, shared_kv, has_kbias, n_delta):
    if has_kbias:
        q_ref, k_ref, v_ref, kb_ref, bt_ref, o_ref, m_scr, l_scr, acc_scr = refs
    else:
        q_ref, k_ref, v_ref, bt_ref, o_ref, m_scr, l_scr, acc_scr = refs
        kb_ref = None
    step = pl.program_id(1)
    qi = qt_ref[step]
    ki = kt_ref[step]

    @pl.when(ki == 0)
    def _():
        m_scr[...] = jnp.full(m_scr.shape, -3e38, F32)
        l_scr[...] = jnp.zeros(l_scr.shape, F32)
        acc_scr[...] = jnp.zeros(acc_scr.shape, F32)

    hb = bt_ref.shape[1]
    for h in range(hps):
        hk = 0 if shared_kv else h
        s = _dot_nt(q_ref[0, h], k_ref[hk])
        if has_kbias:
            s = s - kb_ref[hk]
        s = s + bt_ref[0, h if hb > 1 else 0]
        m_prev = m_scr[h]
        m_new = jnp.maximum(m_prev, jnp.max(s, axis=-1, keepdims=True))
        alpha = jnp.exp(m_prev - m_new)
        p = jnp.exp(s - m_new)
        l_scr[h] = alpha * l_scr[h] + jnp.sum(p, axis=-1, keepdims=True)
        acc_scr[h] = alpha * acc_scr[h] + _dot(p.astype(BF16), v_ref[hk])
        m_scr[h] = m_new

    @pl.when(ki == qi)
    def _():
        outs = [acc_scr[h] / l_scr[h] for h in range(hps)]
        o_ref[...] = jnp.concatenate(outs, axis=1).astype(o_ref.dtype)


def flash_causal(q, k, v, kbias, btab, T, hps, key_group_tokens=None, out_dtype=BF16):
    G, H, S, Dk = q.shape
    Hk, _, Dv = v.shape
    shared_kv = Hk == 1
    n_delta, Hb = btab.shape[0], btab.shape[1]
    nt = S // T
    qt = np.array([i for i in range(nt) for _ in range(i + 1)], np.int32)
    kt = np.array([j for i in range(nt) for j in range(i + 1)], np.int32)
    ngroups = H // hps
    kv_h = 1 if shared_kv else hps
    tiles_per_group = (key_group_tokens // T) if G > 1 else 1

    def q_map(g, s, qt, kt):
        return (kt[s] // tiles_per_group if G > 1 else 0, g, qt[s], 0)

    def kv_map(g, s, qt, kt):
        return (0 if shared_kv else g, kt[s], 0)

    def kb_map(g, s, qt, kt):
        return (0 if shared_kv else g, 0, kt[s])

    def bt_map(g, s, qt, kt):
        return (jnp.minimum(qt[s] - kt[s], n_delta - 1), g if Hb > hps else 0, 0, 0)

    in_specs = [pl.BlockSpec((1, hps, T, Dk), q_map),
                pl.BlockSpec((kv_h, T, Dk), kv_map),
                pl.BlockSpec((kv_h, T, Dv), kv_map)]
    args = [q, k, v]
    if kbias is not None:
        in_specs.append(pl.BlockSpec((kv_h, 1, T), kb_map))
        args.append(kbias)
    in_specs.append(pl.BlockSpec((1, min(Hb, hps), T, T), bt_map))
    args.append(btab)
    grid_spec = pltpu.PrefetchScalarGridSpec(
        num_scalar_prefetch=2,
        grid=(ngroups, len(qt)),
        in_specs=in_specs,
        out_specs=pl.BlockSpec((T, hps * Dv), lambda g, s, qt, kt: (qt[s], g)),
        scratch_shapes=[pltpu.VMEM((hps, T, 1), F32), pltpu.VMEM((hps, T, 1), F32),
                        pltpu.VMEM((hps, T, Dv), F32)],
    )
    return pl.pallas_call(
        functools.partial(_flash_kernel, hps=hps, shared_kv=shared_kv, has_kbias=kbias is not None,
                          n_delta=n_delta),
        grid_spec=grid_spec,
        out_shape=jax.ShapeDtypeStruct((S, H * Dv), out_dtype),
        compiler_params=_cparams(("parallel", "arbitrary")),
    )(jnp.asarray(qt), jnp.asarray(kt), *args)


def _band_kernel(*refs, n_prev, Ta, H, shared_kv, want_lse):
    q_ref = refs[0]
    k_refs = refs[1:2 + n_prev]
    v_refs = refs[2 + n_prev:3 + 2 * n_prev]
    bt_ref = refs[3 + 2 * n_prev]
    o_ref = refs[4 + 2 * n_prev]
    lse_ref = refs[5 + 2 * n_prev] if want_lse else None
    ai = pl.program_id(1)
    k = jnp.concatenate([r[...] for r in k_refs], axis=0).astype(BF16)
    v = jnp.concatenate([r[...] for r in v_refs], axis=0).astype(BF16)
    W = (n_prev + 1) * Ta
    col = lax.broadcasted_iota(jnp.int32, (1, W), 1)
    col_valid = (ai - n_prev) * Ta + col >= 0
    q = q_ref[...]
    outs, lses = [], []
    for h in range(H):
        hk = 0 if shared_kv else h
        qh = (q[:, h * HEAD_DIM:(h + 1) * HEAD_DIM] * (HEAD_DIM ** -0.5)).astype(BF16)
        s = _dot_nt(qh, k[:, hk * HEAD_DIM:(hk + 1) * HEAD_DIM]) + bt_ref[h]
        s = jnp.where(col_valid, s, NEG_INF)
        m = jnp.max(s, axis=-1, keepdims=True)
        p = jnp.exp(s - m)
        l = jnp.sum(p, axis=-1, keepdims=True)
        outs.append(_dot(p.astype(BF16), v[:, hk * HEAD_DIM:(hk + 1) * HEAD_DIM]) / l)
        if want_lse:
            lses.append(jnp.broadcast_to(m + jnp.log(l), (Ta, HEAD_DIM)))
    o_ref[...] = jnp.concatenate(outs, axis=1)
    if want_lse:
        lse_ref[...] = jnp.concatenate(lses, axis=1)


def band_attention(q, k, v, btab, Ta, n_prev, want_lse):
    R, A, QW = q.shape
    KW = k.shape[2]
    H = QW // HEAD_DIM
    shared_kv = KW == HEAD_DIM
    nA = A // Ta

    def prev_map(p):
        return lambda r, a: (r, jnp.maximum(a - n_prev + p, 0), 0)

    cur = lambda r, a: (r, a, 0)
    kv_specs = [pl.BlockSpec((None, Ta, KW), prev_map(p)) for p in range(n_prev)] + [pl.BlockSpec((None, Ta, KW), cur)]
    out_shape = [jax.ShapeDtypeStruct((R, A, QW), F32)]
    out_specs = [pl.BlockSpec((None, Ta, QW), cur)]
    if want_lse:
        out_shape.append(jax.ShapeDtypeStruct((R, A, QW), F32))
        out_specs.append(pl.BlockSpec((None, Ta, QW), cur))
    res = pl.pallas_call(
        functools.partial(_band_kernel, n_prev=n_prev, Ta=Ta, H=H, shared_kv=shared_kv, want_lse=want_lse),
        grid=(R, nA),
        in_specs=[pl.BlockSpec((None, Ta, QW), cur)] + kv_specs + kv_specs
                 + [pl.BlockSpec(btab.shape, lambda r, a: (0, 0, 0))],
        out_specs=out_specs,
        out_shape=out_shape,
        compiler_params=_cparams(("parallel", "arbitrary")),
    )(q, *([k] * (n_prev + 1)), *([v] * (n_prev + 1)), btab)
    return res if want_lse else res[0]


def _gelu_tanh(x):
    return 0.5 * x * (1.0 + jnp.tanh(math.sqrt(2.0 / math.pi) * (x + 0.044715 * (x * x * x))))


def _compress_kernel(ks_ref, vs_ref, pk_ref, pv_ref, w1k_ref, w2k_ref, w1v_ref, w2v_ref, kc_ref, vc_ref):
    for src, pos, w1, w2, out in ((ks_ref, pk_ref, w1k_ref, w2k_ref, kc_ref),
                                  (vs_ref, pv_ref, w1v_ref, w2v_ref, vc_ref)):
        x = src[...]
        n = x.shape[0]
        first = _dot((x + pos[0:1, :]).astype(BF16), w1[0])
        second = _dot((x + pos[1:2, :]).astype(BF16), w1[1])
        pre = first + pltpu.roll(second, n - 1, axis=0)
        out[...] = _dot(_gelu_tanh(pre).astype(BF16), w2[...]).astype(out.dtype)


def nsa_compress(k_chunks, v_chunks, pos_k, pos_v, w1k, w2k, w1v, w2v):
    n = k_chunks.shape[0]
    return pl.pallas_call(
        _compress_kernel,
        out_shape=[jax.ShapeDtypeStruct((n, HEAD_DIM), BF16), jax.ShapeDtypeStruct((n, HEAD_DIM), BF16)],
        compiler_params=_cparams(None),
    )(k_chunks, v_chunks, pos_k, pos_v, w1k, w2k, w1v, w2v)


def _cmp_topk_kernel(nq_ref, kc_ref, vc_ref, ov_ref, ocmp_ref, qaug_ref, *, Tq, n_cmp, n_sel, nbg):
    q0 = pl.program_id(0) * Tq
    t = q0 + lax.broadcasted_iota(jnp.int32, (Tq, 1), 0)
    cmp_end = NSA_CMP_STRIDE * lax.broadcasted_iota(jnp.int32, (1, n_cmp), 1) + (NSA_CMP_LEN - 1)
    c_mask = cmp_end <= t
    kc = kc_ref[...]
    vc = vc_ref[...]
    q = nq_ref[...]
    psum = jnp.zeros((Tq, n_cmp), F32)
    outs, qhs = [], []
    for h in range(NSA_HEADS):
        qh = (q[:, h * HEAD_DIM:(h + 1) * HEAD_DIM] * (HEAD_DIM ** -0.5)).astype(BF16)
        qhs.append(qh)
        s = jnp.where(c_mask, _dot_nt(qh, kc), NEG_INF)
        m = jnp.max(s, axis=-1, keepdims=True)
        p = jnp.exp(s - m)
        p = jnp.where(c_mask, p / jnp.sum(p, axis=-1, keepdims=True), 0.0)
        psum = psum + p
        outs.append(_dot(p.astype(BF16), vc))
    ocmp_ref[...] = jnp.concatenate(outs, axis=1)
    hi = psum.astype(BF16)
    lo = (psum - hi.astype(F32)).astype(BF16)
    imp = _dot(hi, ov_ref[...]) + _dot(lo, ov_ref[...])
    j = lax.broadcasted_iota(jnp.int32, (Tq, n_sel), 1)
    start = j * NSA_SEL_BLOCK
    cur_blk = (start <= t) & (t < start + NSA_SEL_BLOCK)
    prev_blk = (start + NSA_SEL_BLOCK <= t) & (t < start + 2 * NSA_SEL_BLOCK)
    forced = (j == 0) | cur_blk | prev_blk
    score = jnp.where(forced, NSA_FORCE_SCORE, jnp.where(start <= t, imp, -NSA_FORCE_SCORE))
    jf = j.astype(F32)
    sel = jnp.zeros((Tq, n_sel), jnp.bool_)
    for _ in range(min(NSA_TOPK, n_sel)):
        mx = jnp.max(score, axis=-1, keepdims=True)
        first = jnp.min(jnp.where(score == mx, jf, float(n_sel)), axis=-1, keepdims=True)
        hit = jf == first
        sel = sel | hit
        score = jnp.where(hit, -3e38, score)
    sel_bias = jnp.where(sel, 0.0, SEL_NEG).astype(BF16)
    for g in range(n_sel // nbg):
        sb = sel_bias[:, g * nbg:(g + 1) * nbg]
        for h in range(NSA_HEADS):
            qaug_ref[g, h] = jnp.concatenate([qhs[h], sb], axis=1)


def nsa_cmp_topk(nq, kc, vc, overlap, Tq, nbg):
    S = nq.shape[0]
    n_cmp = kc.shape[0]
    n_sel = S // NSA_SEL_BLOCK
    G = n_sel // nbg
    return pl.pallas_call(
        functools.partial(_cmp_topk_kernel, Tq=Tq, n_cmp=n_cmp, n_sel=n_sel, nbg=nbg),
        grid=(S // Tq,),
        in_specs=[pl.BlockSpec((Tq, NSA_W), lambda i: (i, 0)),
                  pl.BlockSpec((n_cmp, HEAD_DIM), lambda i: (0, 0)),
                  pl.BlockSpec((n_cmp, HEAD_DIM), lambda i: (0, 0)),
                  pl.BlockSpec((n_cmp, n_sel), lambda i: (0, 0))],
        out_specs=[pl.BlockSpec((Tq, NSA_W), lambda i: (i, 0)),
                   pl.BlockSpec((G, NSA_HEADS, Tq, HEAD_DIM + nbg), lambda i: (0, 0, i, 0))],
        out_shape=[jax.ShapeDtypeStruct((S, NSA_W), F32),
                   jax.ShapeDtypeStruct((G, NSA_HEADS, S, HEAD_DIM + nbg), BF16)],
        compiler_params=_cparams(("parallel",)),
    )(nq, kc, vc, overlap)


def _outproj_even_kernel(oa_ref, ob_ref, wa_ref, wb_ref, g_ref, h_ref, out_ref):
    m = _dot(oa_ref[...], wa_ref[...]) + _dot(ob_ref[...], wb_ref[...])
    out_ref[...] = h_ref[...] + _rms(m, g_ref[...])


def outproj_even(oa, ob, wa, wb, g, h, tm):
    S, D = h.shape
    rows = lambda w: pl.BlockSpec((tm, w), lambda i: (i, 0))
    full = lambda a: pl.BlockSpec(a.shape, lambda i: (0,) * a.ndim)
    return pl.pallas_call(
        _outproj_even_kernel,
        grid=(S // tm,),
        in_specs=[rows(oa.shape[1]), rows(ob.shape[1]), full(wa), full(wb), full(g), rows(D)],
        out_specs=rows(D),
        out_shape=jax.ShapeDtypeStruct((S, D), F32),
        compiler_params=_cparams(("parallel",)),
    )(oa, ob, wa, wb, g, h)


def _outproj_odd_kernel(o0_ref, o1_ref, o2_ref, l0_ref, l1_ref, l2_ref, oc_ref, os_ref, ow_ref, gt_ref,
                        wd_ref, wn_ref, g_ref, h_ref, out_ref):
    l0, l1, l2 = l0_ref[...], l1_ref[...], l2_ref[...]
    mx = jnp.maximum(jnp.maximum(l0, l1), l2)
    e0, e1, e2 = jnp.exp(l0 - mx), jnp.exp(l1 - mx), jnp.exp(l2 - mx)
    o_dil = (e0 * o0_ref[...] + e1 * o1_ref[...] + e2 * o2_ref[...]) / (e0 + e1 + e2)
    gt = jax.nn.sigmoid(gt_ref[...])
    o_nsa = (gt[:, 0:NSA_W] * oc_ref[...] + gt[:, NSA_W:2 * NSA_W] * os_ref[...].astype(F32)
             + gt[:, 2 * NSA_W:3 * NSA_W] * ow_ref[...])
    m = _dot(o_dil.astype(BF16), wd_ref[...]) + _dot(o_nsa.astype(BF16), wn_ref[...])
    out_ref[...] = h_ref[...] + _rms(m, g_ref[...])


def outproj_odd(o_dil, lse_dil, o_cmp, o_sel, o_win, gates, wd, wn, g, h, tm):
    S, D = h.shape
    rows = lambda w: pl.BlockSpec((tm, w), lambda i: (i, 0))
    full = lambda a: pl.BlockSpec(a.shape, lambda i: (0,) * a.ndim)
    return pl.pallas_call(
        _outproj_odd_kernel,
        grid=(S // tm,),
        in_specs=[rows(DIL_GW)] * 6 + [rows(NSA_W)] * 3 + [rows(3 * NSA_W), full(wd), full(wn), full(g), rows(D)],
        out_specs=rows(D),
        out_shape=jax.ShapeDtypeStruct((S, D), F32),
        compiler_params=_cparams(("parallel",)),
    )(*o_dil, *lse_dil, o_cmp, o_sel, o_win, gates, wd, wn, g, h)


def _mlp_ple_kernel(h_ref, g1_ref, wu_ref, wd_ref, g2_ref, g3_ref, wg_ref, p_ref, wp_ref, out_ref,
                    hn_ref, acc_ref):
    j = pl.program_id(1)

    @pl.when(j == 0)
    def _():
        hn_ref[...] = _rms(h_ref[...], g1_ref[...]).astype(BF16)
        acc_ref[...] = jnp.zeros(acc_ref.shape, F32)

    u = jnp.maximum(_dot(hn_ref[...], wu_ref[...]), 0.0)
    acc_ref[...] += _dot((u * u).astype(BF16), wd_ref[...])

    @pl.when(j == pl.num_programs(1) - 1)
    def _():
        h2 = h_ref[...] + _rms(acc_ref[...], g2_ref[...])
        gate = jax.nn.sigmoid(_dot(_rms(h2, g3_ref[...]).astype(BF16), wg_ref[...]))
        out_ref[...] = h2 + gate * _dot(p_ref[...].astype(BF16), wp_ref[...])


def mlp_ple(h, g1, wu, wd, g2, g3, wg, p, wp, tm, tf):
    S, D = h.shape
    FF = wu.shape[1]
    PD = p.shape[1]
    const = lambda a: pl.BlockSpec(a.shape, lambda i, j: (0,) * a.ndim)
    return pl.pallas_call(
        _mlp_ple_kernel,
        grid=(S // tm, FF // tf),
        in_specs=[pl.BlockSpec((tm, D), lambda i, j: (i, 0)), const(g1),
                  pl.BlockSpec((D, tf), lambda i, j: (0, j)),
                  pl.BlockSpec((tf, D), lambda i, j: (j, 0)),
                  const(g2), const(g3), const(wg),
                  pl.BlockSpec((tm, PD), lambda i, j: (i, 0)), const(wp)],
        out_specs=pl.BlockSpec((tm, D), lambda i, j: (i, 0)),
        out_shape=jax.ShapeDtypeStruct((S, D), F32),
        scratch_shapes=[pltpu.VMEM((tm, D), BF16), pltpu.VMEM((tm, D), F32)],
        compiler_params=_cparams(("parallel", "arbitrary")),
    )(h, g1, wu, wd, g2, g3, wg, p, wp)


def _t5_bucket_of(dist):
    n = jnp.maximum(dist, 0)
    max_exact = T5_BUCKETS // 2
    ratio = jnp.log(jnp.maximum(n, 1).astype(F32) / max_exact) / math.log(T5_MAX_DIST / max_exact)
    large = jnp.minimum(max_exact + (ratio * (T5_BUCKETS - max_exact)).astype(jnp.int32), T5_BUCKETS - 1)
    return jnp.where(n < max_exact, n, large)


def _causal_tables(T):
    i = jnp.arange(T)[:, None]
    j = jnp.arange(T)[None, :]
    diag = jnp.where(i >= j, 0.0, NEG_INF).astype(F32)
    return jnp.stack([diag, jnp.zeros((T, T), F32)])[:, None]


def _t5_delta_tables(bias, T):
    n_delta = -(-(T5_MAX_DIST - 1) // T) + 2
    i = jnp.arange(T)[:, None]
    j = jnp.arange(T)[None, :]
    tabs = []
    for delta in range(n_delta):
        dist = delta * T + i - j
        val = bias[_t5_bucket_of(dist)] - bias[T5_BUCKETS - 1]
        tabs.append(jnp.where((dist >= 0)[..., None], val, NEG_INF))
    return jnp.transpose(jnp.stack(tabs), (0, 3, 1, 2)).astype(F32)


def _band_table(bias, Ta, n_prev, max_rel, stride, inclusive):
    P = n_prev * Ta
    i = jnp.arange(Ta)[:, None]
    c = jnp.arange(P + Ta)[None, :]
    rel = i + P - c
    ok = (rel >= 0) & ((rel <= max_rel) if inclusive else (rel < max_rel))
    val = bias[_t5_bucket_of(rel * stride)]
    return jnp.transpose(jnp.where(ok[..., None], val, NEG_INF), (2, 0, 1)).astype(F32)


def _tile(S, pref):
    t = min(pref, S)
    assert S % t == 0
    return t


def even_mixer_core(y, cos, sin, w_uq, q_norm, w_ukv, kv_norm, forget_bias, T):
    S = y.shape[0]
    scale = MLA_QK_DIM ** -0.5
    half = MLA_ROPE_DIM // 2
    ones = jnp.ones((S, MLA_NOPE_DIM), F32)
    cq = jnp.concatenate([ones, cos, cos], axis=1) * scale
    sq = jnp.concatenate([0.0 * ones, -sin, sin], axis=1) * scale
    ck = jnp.concatenate([cos, cos], axis=1)
    sk = jnp.concatenate([-sin, sin], axis=1)
    wq = jnp.transpose(w_uq.reshape(MLA_Q_RANK, MLA_HEADS, MLA_QK_DIM), (1, 0, 2))
    swap = np.concatenate([np.arange(MLA_NOPE_DIM), MLA_NOPE_DIM + half + np.arange(half),
                           MLA_NOPE_DIM + np.arange(half)])
    wqs = wq[:, :, swap]
    wkv = jnp.transpose(w_ukv.reshape(MLA_KV_RANK, MLA_HEADS, MLA_NOPE_DIM + MLA_V_DIM), (1, 0, 2))
    wk = jnp.pad(wkv[:, :, :MLA_NOPE_DIM], ((0, 0), (0, 0), (0, MLA_ROPE_DIM)))
    wv = wkv[:, :, MLA_NOPE_DIM:]
    e96 = jnp.pad(jnp.eye(MLA_ROPE_DIM, dtype=F32), ((0, 0), (MLA_NOPE_DIM, 0)))
    qm, km, vm, fq, fk, fv, logf = even_prep(
        y, wq.astype(BF16), wqs.astype(BF16), wk.astype(BF16), e96.astype(BF16), wv.astype(BF16),
        q_norm[None, :], kv_norm[None, :], forget_bias[None, :], cq, sq, ck, sk, _tile(S, 512))
    cum = cumsum_lanes(jnp.transpose(logf))
    causal = _causal_tables(T)
    o_mla = flash_causal(qm[None], km, vm, None, causal, T, hps=2)
    o_fox = flash_causal(fq[None], fk, fv, cum[:, None, :], causal, T, hps=2)
    return o_mla, o_fox


def odd_mixer_core(y, t5_bias, pos_k, pos_v, w1k, w2k, w1v, w2v, T):
    S = y.shape[0]
    G = len(DIL_GROUPS)
    Ta = 128
    o_dil, lse_dil = [], []
    for g, (w, d) in enumerate(DIL_GROUPS):
        def deint(a):
            return jnp.transpose(a.reshape(S // d, d, DIL_GW), (1, 0, 2))
        q, k, v = (deint(y[:, (b * G + g) * DIL_GW:(b * G + g + 1) * DIL_GW]) for b in range(3))
        btab = _band_table(t5_bias[:, g * DIL_HEADS:(g + 1) * DIL_HEADS], Ta, 1, w // d, d, True)
        o, lse = band_attention(q, k, v, btab, _tile(S // d, Ta), 1, True)
        o_dil.append(jnp.transpose(o, (1, 0, 2)).reshape(S, DIL_GW))
        lse_dil.append(jnp.transpose(lse, (1, 0, 2)).reshape(S, DIL_GW))
    base = 3 * G * DIL_GW
    nq = y[:, base:base + NSA_W]
    k_cmp, v_cmp, k_sel, v_sel, k_win, v_win = (
        y[:, base + NSA_W + i * HEAD_DIM:base + NSA_W + (i + 1) * HEAD_DIM] for i in range(6))
    gates = y[:, base + NSA_W + 6 * HEAD_DIM:base + NSA_W + 6 * HEAD_DIM + 3 * NSA_W]
    bias_nsa = t5_bias[:, G * DIL_HEADS:]
    n_chunk = S // NSA_CMP_STRIDE
    cw = NSA_CMP_STRIDE * HEAD_DIM
    kc, vc = nsa_compress(k_cmp.reshape(n_chunk, cw), v_cmp.reshape(n_chunk, cw),
                          pos_k.reshape(2, cw), pos_v.reshape(2, cw),
                          w1k.reshape(2, cw, -1).astype(BF16), w2k.astype(BF16),
                          w1v.reshape(2, cw, -1).astype(BF16), w2v.astype(BF16))
    n_sel = S // NSA_SEL_BLOCK
    ci = np.arange(n_chunk)[:, None] * NSA_CMP_STRIDE
    sj = np.arange(n_sel)[None, :] * NSA_SEL_BLOCK
    overlap = jnp.asarray(((ci < sj + NSA_SEL_BLOCK) & (ci + NSA_CMP_LEN > sj)).astype(np.float32), BF16)
    nbg = min(n_sel, NSA_SEL_GROUP_BLOCKS)
    o_cmp, qaug = nsa_cmp_topk(nq, kc, vc, overlap, _tile(S, 128), nbg)
    onehot = jnp.asarray(((np.arange(S)[:, None] // NSA_SEL_BLOCK) % nbg == np.arange(nbg)[None, :])
                         .astype(np.float32), BF16)
    kaug = jnp.concatenate([k_sel.astype(BF16), onehot], axis=1)
    o_sel = flash_causal(qaug, kaug[None], v_sel.astype(BF16)[None], None, _t5_delta_tables(bias_nsa, T), T,
                         hps=NSA_HEADS, key_group_tokens=nbg * NSA_SEL_BLOCK)
    Tw = _tile(S, 256)
    n_prev = NSA_WINDOW // Tw
    wtab = _band_table(bias_nsa, Tw, n_prev, NSA_WINDOW, 1, False)
    o_win = band_attention(nq[None], k_win[None], v_win[None], wtab, Tw, n_prev, False)[0]
    return o_dil, lse_dil, o_cmp, o_sel, o_win, gates


def _pad_cols(w, n):
    return jnp.pad(w, ((0, 0), (0, n - w.shape[1])))


def _even_w_in(w):
    cq, ckv, kr, fq, fk, fv, fl = jnp.split(w, np.cumsum([256, 128, 32, 512, 512, 512])[:], axis=1)
    half = MLA_ROPE_DIM // 2
    kr_sw = jnp.concatenate([kr[:, half:], kr[:, :half]], axis=1)
    head = _pad_cols(jnp.concatenate([cq, ckv, kr, kr_sw, fl], axis=1), 512)
    return jnp.concatenate([head, fq, fk, fv], axis=1).astype(BF16)


def _odd_w_in(w):
    main = w[:, :3 * 3 * DIL_GW + NSA_W + 6 * HEAD_DIM]
    gl = w[:, 3 * 3 * DIL_GW + NSA_W + 6 * HEAD_DIM:]
    gl = jnp.transpose(gl.reshape(-1, NSA_HEADS, 3), (0, 2, 1))
    gl = jnp.repeat(gl[..., None], HEAD_DIM, axis=-1).reshape(w.shape[0], 3 * NSA_W)
    return _pad_cols(jnp.concatenate([main, gl], axis=1), ODD_IN_PAD).astype(BF16)


def _trunk(x, p, positions, t5_bias, ev_w_in, ev_q_norm, ev_w_uq, ev_kv_norm, ev_w_ukv, ev_forget_bias,
           ev_w_out, od_w_in, od_cmp_pos_k, od_cmp_pos_v, od_cmp_w1_k, od_cmp_w2_k, od_cmp_w1_v, od_cmp_w2_v,
           od_w_out, norm_mix_pre, norm_mix_post, norm_mlp_pre, norm_mlp_post, w_mlp_up, w_mlp_down,
           ple_norm, w_ple_gate, w_ple_proj):
    S, D = x.shape
    depth = p.shape[0]
    T = _tile(S, 512)
    tm = _tile(S, 512)
    inv_freq = ROPE_THETA ** (-jnp.arange(0, MLA_ROPE_DIM, 2, dtype=F32) / MLA_ROPE_DIM)
    angles = positions.astype(F32)[:, None] * inv_freq
    cos, sin = jnp.cos(angles), jnp.sin(angles)
    h = x
    for i in range(depth):
        j = i // 2
        if i % 2 == 0:
            y = norm_matmul(h, norm_mix_pre[i][None], _even_w_in(ev_w_in[j]), tm, 512)
            o_mla, o_fox = even_mixer_core(y, cos, sin, ev_w_uq[j], ev_q_norm[j], ev_w_ukv[j], ev_kv_norm[j],
                                           ev_forget_bias[j], T)
            wo = ev_w_out[j].astype(BF16)
            na = MLA_HEADS * MLA_V_DIM
            h = outproj_even(o_mla, o_fox, wo[:na], wo[na:], norm_mix_post[i][None], h, tm)
        else:
            y = norm_matmul(h, norm_mix_pre[i][None], _odd_w_in(od_w_in[j]), tm, 768)
            o_dil, lse_dil, o_cmp, o_sel, o_win, gates = odd_mixer_core(
                y, t5_bias, od_cmp_pos_k[j], od_cmp_pos_v[j], od_cmp_w1_k[j], od_cmp_w2_k[j],
                od_cmp_w1_v[j], od_cmp_w2_v[j], T)
            wo = od_w_out[j].astype(BF16)
            h = outproj_odd(o_dil, lse_dil, o_cmp, o_sel, o_win, gates, wo[:DIL_GW], wo[DIL_GW:],
                            norm_mix_post[i][None], h, tm)
        h = mlp_ple(h, norm_mlp_pre[i][None], w_mlp_up[i].astype(BF16), w_mlp_down[i].astype(BF16),
                    norm_mlp_post[i][None], ple_norm[i][None], w_ple_gate[i].astype(BF16), p[i],
                    w_ple_proj[i].astype(BF16), tm, 1024)
    return h


def kernel(x, p, positions, t5_bias, ev_w_in, ev_q_norm, ev_w_uq, ev_kv_norm, ev_w_ukv, ev_forget_bias, ev_w_out, od_w_in, od_cmp_pos_k, od_cmp_pos_v, od_cmp_w1_k, od_cmp_w2_k, od_cmp_w1_v, od_cmp_w2_v, od_w_out, norm_mix_pre, norm_mix_post, norm_mlp_pre, norm_mlp_post, w_mlp_up, w_mlp_down, ple_norm, w_ple_gate, w_ple_proj):
    params = (t5_bias, ev_w_in, ev_q_norm, ev_w_uq, ev_kv_norm, ev_w_ukv, ev_forget_bias, ev_w_out, od_w_in,
              od_cmp_pos_k, od_cmp_pos_v, od_cmp_w1_k, od_cmp_w2_k, od_cmp_w1_v, od_cmp_w2_v, od_w_out,
              norm_mix_pre, norm_mix_post, norm_mlp_pre, norm_mlp_post, w_mlp_up, w_mlp_down, ple_norm,
              w_ple_gate, w_ple_proj)
    outs = [_trunk(x[b], p[:, b], positions[b], *params) for b in range(x.shape[0])]
    return jnp.stack(outs).astype(x.dtype)
```

```python
import functools
import math

import numpy as np
import jax
import jax.numpy as jnp
from jax import lax
from jax.experimental import pallas as pl
from jax.experimental.pallas import tpu as pltpu

F32 = jnp.float32
BF16 = jnp.bfloat16

HEAD_DIM = 64
RMS_EPS = 1e-6
NEG_INF = -1e30
SEL_NEG = -(2.0 ** 99)

MLA_HEADS = 8
MLA_NOPE_DIM = 64
MLA_ROPE_DIM = 32
MLA_V_DIM = 64
MLA_Q_RANK = 256
MLA_KV_RANK = 128
MLA_QK_DIM = MLA_NOPE_DIM + MLA_ROPE_DIM
ROPE_THETA = 10000.0
FOX_HEADS = 8
FOX_SPLIT = 3
FOX_PAD = 16
LOG2E = math.log2(math.e)
DIL_GROUPS = ((128, 1), (512, 4), (2048, 16))
DIL_HEADS = 4
DIL_GW = DIL_HEADS * HEAD_DIM
NSA_HEADS = 4
NSA_W = NSA_HEADS * HEAD_DIM
NSA_CMP_LEN = 32
NSA_CMP_STRIDE = 16
NSA_SEL_BLOCK = 64
NSA_TOPK = 16
NSA_WINDOW = 512
NSA_FORCE_SCORE = 1e9
NSA_SEL_GROUP_BLOCKS = 128
T5_BUCKETS = 32
T5_MAX_DIST = 2048

EVEN_IN_PAD = 2048
ODD_IN_PAD = 3840

VMEM_LIMIT_BYTES = 56 * 1024 * 1024


def _cparams(sem):
    return pltpu.CompilerParams(dimension_semantics=sem, vmem_limit_bytes=VMEM_LIMIT_BYTES)


def _rms(x, g):
    return x * lax.rsqrt(jnp.mean(x * x, axis=-1, keepdims=True) + RMS_EPS) * g


def _dot(a, b):
    return jnp.dot(a, b, preferred_element_type=F32)


def _dot_nt(a, b):
    return lax.dot_general(a, b, (((1,), (1,)), ((), ())), preferred_element_type=F32)


def _norm_matmul_kernel(h_ref, g_ref, w_ref, o_ref, hn_ref):
    @pl.when(pl.program_id(1) == 0)
    def _():
        hn_ref[...] = _rms(h_ref[...], g_ref[...]).astype(BF16)

    o_ref[...] = _dot(hn_ref[...], w_ref[...])


def norm_matmul(h, g, w, tm, tn):
    S, D = h.shape
    N = w.shape[1]
    return pl.pallas_call(
        _norm_matmul_kernel,
        grid=(S // tm, N // tn),
        in_specs=[pl.BlockSpec((tm, D), lambda i, j: (i, 0)),
                  pl.BlockSpec((1, D), lambda i, j: (0, 0)),
                  pl.BlockSpec((D, tn), lambda i, j: (0, j))],
        out_specs=pl.BlockSpec((tm, tn), lambda i, j: (i, j)),
        out_shape=jax.ShapeDtypeStruct((S, N), F32),
        scratch_shapes=[pltpu.VMEM((tm, D), BF16)],
        compiler_params=_cparams(("parallel", "arbitrary")),
    )(h, g, w)


def _even_prep_kernel(y_ref, wq_ref, wqs_ref, wk_ref, e_ref, wv_ref, eye_ref, qn_ref, kvn_ref, fb_ref,
                      cq_ref, sq_ref, ck_ref, sk_ref,
                      qm_ref, km_ref, vm_ref, fq_ref, fk_ref, fv_ref, lf_ref):
    y = y_ref[...]
    tm = y.shape[0]
    c_q = y[:, 0:MLA_Q_RANK]
    c_kv = y[:, MLA_Q_RANK:MLA_Q_RANK + MLA_KV_RANK]
    o = MLA_Q_RANK + MLA_KV_RANK
    k_r = y[:, o:o + MLA_ROPE_DIM]
    k_rs = y[:, o + MLA_ROPE_DIM:o + 2 * MLA_ROPE_DIM]
    f_logit = y[:, o + 2 * MLA_ROPE_DIM:o + 2 * MLA_ROPE_DIM + FOX_HEADS]
    nq = _rms(c_q, qn_ref[...]).astype(BF16)
    nkv = _rms(c_kv, kvn_ref[...]).astype(BF16)
    k_rot = (k_r * ck_ref[...] + k_rs * sk_ref[...]).astype(BF16)
    k_rot96 = _dot(k_rot, e_ref[...])
    cq = cq_ref[...]
    sq = sq_ref[...]
    for h in range(MLA_HEADS):
        q = _dot_nt(wq_ref[h], nq) * cq + _dot_nt(wqs_ref[h], nq) * sq
        qm_ref[h] = q.astype(BF16)
        km_ref[h] = (_dot(nkv, wk_ref[h]) + k_rot96).astype(BF16)
        vm_ref[h] = _dot_nt(wv_ref[h], nkv).astype(BF16)
    base = 512
    fw = FOX_HEADS * HEAD_DIM
    eye = eye_ref[...]
    ones_rows = jnp.where(lax.broadcasted_iota(jnp.int32, (FOX_PAD, tm), 0) < FOX_SPLIT, 1.0, 0.0).astype(BF16)
    for h in range(FOX_HEADS):
        lo = base + h * HEAD_DIM
        fq = (y[:, lo:lo + HEAD_DIM] * (HEAD_DIM ** -0.5 * LOG2E)).astype(BF16)
        fq_ref[h] = jnp.concatenate([_dot_nt(eye, fq).astype(BF16), ones_rows], axis=0)
        fk_ref[h] = y[:, lo + fw:lo + fw + HEAD_DIM].astype(BF16)
        fv = y[:, lo + 2 * fw:lo + 2 * fw + HEAD_DIM].astype(BF16)
        fv_ref[h] = _dot_nt(eye, fv).astype(BF16)
    z = f_logit + fb_ref[...]
    lf_ref[...] = jnp.minimum(z, 0.0) - jnp.log1p(jnp.exp(-jnp.abs(z)))


def even_prep(y, wq, wqs, wk, e96, wv, eye, qn, kvn, fb, cq, sq, ck, sk, tm):
    S = y.shape[0]
    full = lambda a: pl.BlockSpec(a.shape, lambda i: (0,) * a.ndim)
    rows = lambda w: pl.BlockSpec((tm, w), lambda i: (i, 0))
    cols = lambda w: pl.BlockSpec((w, tm), lambda i: (0, i))
    heads = lambda n, w: pl.BlockSpec((n, tm, w), lambda i: (0, i, 0))
    heads_t = lambda n, w: pl.BlockSpec((n, w, tm), lambda i: (0, 0, i))
    out_shape = [jax.ShapeDtypeStruct((MLA_HEADS, MLA_QK_DIM, S), BF16),
                 jax.ShapeDtypeStruct((MLA_HEADS, S, MLA_QK_DIM), BF16),
                 jax.ShapeDtypeStruct((MLA_HEADS, MLA_V_DIM, S), BF16),
                 jax.ShapeDtypeStruct((FOX_HEADS, HEAD_DIM + FOX_PAD, S), BF16),
                 jax.ShapeDtypeStruct((FOX_HEADS, S, HEAD_DIM), BF16),
                 jax.ShapeDtypeStruct((FOX_HEADS, HEAD_DIM, S), BF16),
                 jax.ShapeDtypeStruct((S, FOX_HEADS), F32)]
    out_specs = [heads_t(MLA_HEADS, MLA_QK_DIM), heads(MLA_HEADS, MLA_QK_DIM), heads_t(MLA_HEADS, MLA_V_DIM),
                 heads_t(FOX_HEADS, HEAD_DIM + FOX_PAD), heads(FOX_HEADS, HEAD_DIM), heads_t(FOX_HEADS, HEAD_DIM),
                 rows(FOX_HEADS)]
    return pl.pallas_call(
        _even_prep_kernel,
        grid=(S // tm,),
        in_specs=[rows(EVEN_IN_PAD), full(wq), full(wqs), full(wk), full(e96), full(wv), full(eye), full(qn),
                  full(kvn), full(fb), cols(MLA_QK_DIM), cols(MLA_QK_DIM), rows(MLA_ROPE_DIM), rows(MLA_ROPE_DIM)],
        out_specs=out_specs,
        out_shape=out_shape,
        compiler_params=_cparams(("parallel",)),
    )(y, wq, wqs, wk, e96, wv, eye, qn, kvn, fb, cq, sq, ck, sk)


def _cumsum_split_kernel(x_ref, o_ref):
    x = x_ref[...]
    n = x.shape[1]
    lane = lax.broadcasted_iota(jnp.int32, x.shape, 1)
    shift = 1
    while shift < n:
        x = x + jnp.where(lane >= shift, pltpu.roll(x, shift, axis=1), 0.0)
        shift *= 2
    r = -x * LOG2E
    for i in range(FOX_SPLIT):
        part = r.astype(BF16)
        o_ref[i] = part
        r = r - part.astype(F32)


def cumsum_split(x):
    return pl.pallas_call(
        _cumsum_split_kernel,
        out_shape=jax.ShapeDtypeStruct((FOX_SPLIT,) + x.shape, BF16),
        compiler_params=_cparams(None),
    )(x)


def _flash_kernel(qt_ref, kt_ref, q_ref, k_ref, v_ref, bt_ref, o_ref, m_scr, l_scr, acc_scr, *,
                  hps, shared_kv, n_delta):
    step = pl.program_id(1)
    qi = qt_ref[step]
    ki = kt_ref[step]

    @pl.when(ki == 0)
    def _():
        m_scr[...] = jnp.full(m_scr.shape, -3e38, F32)
        l_scr[...] = jnp.zeros(l_scr.shape, F32)
        acc_scr[...] = jnp.zeros(acc_scr.shape, F32)

    hb = bt_ref.shape[1]

    def update(near):
        m_prev = [m_scr[h] for h in range(hps)]
        l_prev = [l_scr[h] for h in range(hps)]
        acc_prev = [acc_scr[h] for h in range(hps)]
        ss = []
        for h in range(hps):
            s = _dot(k_ref[0 if shared_kv else h], q_ref[0, h])
            if near:
                s = s + bt_ref[0, h if hb > 1 else 0]
            ss.append(s)
        ms = [jnp.maximum(m_prev[h], jnp.max(ss[h], axis=0, keepdims=True)) for h in range(hps)]
        ps = [jnp.exp2(ss[h] - ms[h]) for h in range(hps)]
        for h in range(hps):
            alpha = jnp.exp2(m_prev[h] - ms[h])
            l_scr[h] = alpha * l_prev[h] + jnp.sum(ps[h], axis=0, keepdims=True)
            acc_scr[h] = alpha * acc_prev[h] + _dot(v_ref[0 if shared_kv else h], ps[h].astype(BF16))
            m_scr[h] = ms[h]

    @pl.when(qi - ki < n_delta - 1)
    def _():
        update(True)

    @pl.when(qi - ki >= n_delta - 1)
    def _():
        update(False)

    @pl.when(ki == qi)
    def _():
        outs = [(acc_scr[h] / l_scr[h]).T for h in range(hps)]
        o_ref[...] = jnp.concatenate(outs, axis=1).astype(o_ref.dtype)


def flash_causal(qT, k, vT, btab, T, hps, key_group_tokens=None, out_dtype=BF16):
    G, H, Dk, S = qT.shape
    Hk, Dv, _ = vT.shape
    shared_kv = Hk == 1
    n_delta, Hb = btab.shape[0], btab.shape[1]
    nt = S // T
    qt = np.array([i for i in range(nt) for _ in range(i + 1)], np.int32)
    kt = np.array([j for i in range(nt) for j in range(i + 1)], np.int32)
    kv_h = 1 if shared_kv else hps
    tiles_per_group = (key_group_tokens // T) if G > 1 else 1

    def q_map(g, s, qt, kt):
        return (kt[s] // tiles_per_group if G > 1 else 0, g, 0, qt[s])

    def bt_map(g, s, qt, kt):
        return (jnp.minimum(qt[s] - kt[s], n_delta - 1), g if Hb > hps else 0, 0, 0)

    in_specs = [pl.BlockSpec((1, hps, Dk, T), q_map),
                pl.BlockSpec((kv_h, T, Dk), lambda g, s, qt, kt: (0 if shared_kv else g, kt[s], 0)),
                pl.BlockSpec((kv_h, Dv, T), lambda g, s, qt, kt: (0 if shared_kv else g, 0, kt[s])),
                pl.BlockSpec((1, min(Hb, hps), T, T), bt_map)]
    grid_spec = pltpu.PrefetchScalarGridSpec(
        num_scalar_prefetch=2,
        grid=(H // hps, len(qt)),
        in_specs=in_specs,
        out_specs=pl.BlockSpec((T, hps * Dv), lambda g, s, qt, kt: (qt[s], g)),
        scratch_shapes=[pltpu.VMEM((hps, 1, T), F32), pltpu.VMEM((hps, 1, T), F32),
                        pltpu.VMEM((hps, Dv, T), F32)],
    )
    return pl.pallas_call(
        functools.partial(_flash_kernel, hps=hps, shared_kv=shared_kv, n_delta=n_delta),
        grid_spec=grid_spec,
        out_shape=jax.ShapeDtypeStruct((S, H * Dv), out_dtype),
        compiler_params=_cparams(("parallel", "arbitrary")),
    )(jnp.asarray(qt), jnp.asarray(kt), qT, k, vT, btab)


def _band_kernel(*refs, n_prev, Ta, H, shared_kv, want_lse):
    q_ref = refs[0]
    k_refs = refs[1:2 + n_prev]
    v_refs = refs[2 + n_prev:3 + 2 * n_prev]
    bt_ref = refs[3 + 2 * n_prev]
    o_ref = refs[4 + 2 * n_prev]
    lse_ref = refs[5 + 2 * n_prev] if want_lse else None
    ai = pl.program_id(1)
    k = jnp.concatenate([r[...] for r in k_refs], axis=0).astype(BF16)
    v = jnp.concatenate([r[...] for r in v_refs], axis=0).astype(BF16)
    W = (n_prev + 1) * Ta
    col = lax.broadcasted_iota(jnp.int32, (1, W), 1)
    col_valid = (ai - n_prev) * Ta + col >= 0
    q = q_ref[...]
    outs, lses = [], []
    for h in range(H):
        hk = 0 if shared_kv else h
        qh = (q[:, h * HEAD_DIM:(h + 1) * HEAD_DIM] * (HEAD_DIM ** -0.5)).astype(BF16)
        s = _dot_nt(qh, k[:, hk * HEAD_DIM:(hk + 1) * HEAD_DIM]) + bt_ref[h]
        s = jnp.where(col_valid, s, NEG_INF)
        m = jnp.max(s, axis=-1, keepdims=True)
        p = jnp.exp(s - m)
        l = jnp.sum(p, axis=-1, keepdims=True)
        outs.append(_dot(p.astype(BF16), v[:, hk * HEAD_DIM:(hk + 1) * HEAD_DIM]) / l)
        if want_lse:
            lses.append(jnp.broadcast_to(m + jnp.log(l), (Ta, HEAD_DIM)))
    o_ref[...] = jnp.concatenate(outs, axis=1)
    if want_lse:
        lse_ref[...] = jnp.concatenate(lses, axis=1)


def band_attention(q, k, v, btab, Ta, n_prev, want_lse):
    R, A, QW = q.shape
    KW = k.shape[2]
    H = QW // HEAD_DIM
    shared_kv = KW == HEAD_DIM
    nA = A // Ta

    def prev_map(p):
        return lambda r, a: (r, jnp.maximum(a - n_prev + p, 0), 0)

    cur = lambda r, a: (r, a, 0)
    kv_specs = [pl.BlockSpec((None, Ta, KW), prev_map(p)) for p in range(n_prev)] + [pl.BlockSpec((None, Ta, KW), cur)]
    out_shape = [jax.ShapeDtypeStruct((R, A, QW), F32)]
    out_specs = [pl.BlockSpec((None, Ta, QW), cur)]
    if want_lse:
        out_shape.append(jax.ShapeDtypeStruct((R, A, QW), F32))
        out_specs.append(pl.BlockSpec((None, Ta, QW), cur))
    res = pl.pallas_call(
        functools.partial(_band_kernel, n_prev=n_prev, Ta=Ta, H=H, shared_kv=shared_kv, want_lse=want_lse),
        grid=(R, nA),
        in_specs=[pl.BlockSpec((None, Ta, QW), cur)] + kv_specs + kv_specs
                 + [pl.BlockSpec(btab.shape, lambda r, a: (0, 0, 0))],
        out_specs=out_specs,
        out_shape=out_shape,
        compiler_params=_cparams(("parallel", "arbitrary")),
    )(q, *([k] * (n_prev + 1)), *([v] * (n_prev + 1)), btab)
    return res if want_lse else res[0]


def _gelu_tanh(x):
    return 0.5 * x * (1.0 + jnp.tanh(math.sqrt(2.0 / math.pi) * (x + 0.044715 * (x * x * x))))


def _compress_kernel(ks_ref, vs_ref, pk_ref, pv_ref, w1k_ref, w2k_ref, w1v_ref, w2v_ref, kc_ref, vc_ref):
    for src, pos, w1, w2, out in ((ks_ref, pk_ref, w1k_ref, w2k_ref, kc_ref),
                                  (vs_ref, pv_ref, w1v_ref, w2v_ref, vc_ref)):
        x = src[...]
        n = x.shape[0]
        first = _dot((x + pos[0:1, :]).astype(BF16), w1[0])
        second = _dot((x + pos[1:2, :]).astype(BF16), w1[1])
        pre = first + pltpu.roll(second, n - 1, axis=0)
        out[...] = _dot(_gelu_tanh(pre).astype(BF16), w2[...]).astype(out.dtype)


def nsa_compress(k_chunks, v_chunks, pos_k, pos_v, w1k, w2k, w1v, w2v):
    n = k_chunks.shape[0]
    return pl.pallas_call(
        _compress_kernel,
        out_shape=[jax.ShapeDtypeStruct((n, HEAD_DIM), BF16), jax.ShapeDtypeStruct((n, HEAD_DIM), BF16)],
        compiler_params=_cparams(None),
    )(k_chunks, v_chunks, pos_k, pos_v, w1k, w2k, w1v, w2v)


def _cmp_topk_kernel(nq_ref, kc_ref, vc_ref, ov_ref, eye_ref, ocmp_ref, qaug_ref, *, Tq, n_cmp, n_sel, nbg):
    q0 = pl.program_id(0) * Tq
    t = q0 + lax.broadcasted_iota(jnp.int32, (Tq, 1), 0)
    cmp_end = NSA_CMP_STRIDE * lax.broadcasted_iota(jnp.int32, (1, n_cmp), 1) + (NSA_CMP_LEN - 1)
    c_mask = cmp_end <= t
    kc = kc_ref[...]
    vc = vc_ref[...]
    q = nq_ref[...]
    psum = jnp.zeros((Tq, n_cmp), F32)
    outs = []
    for h in range(NSA_HEADS):
        qh = (q[:, h * HEAD_DIM:(h + 1) * HEAD_DIM] * (HEAD_DIM ** -0.5)).astype(BF16)
        s = jnp.where(c_mask, _dot_nt(qh, kc), NEG_INF)
        m = jnp.max(s, axis=-1, keepdims=True)
        p = jnp.exp(s - m)
        p = jnp.where(c_mask, p / jnp.sum(p, axis=-1, keepdims=True), 0.0)
        psum = psum + p
        outs.append(_dot(p.astype(BF16), vc))
    ocmp_ref[...] = jnp.concatenate(outs, axis=1)
    hi = psum.astype(BF16)
    lo = (psum - hi.astype(F32)).astype(BF16)
    imp = _dot(hi, ov_ref[...]) + _dot(lo, ov_ref[...])
    j = lax.broadcasted_iota(jnp.int32, (Tq, n_sel), 1)
    start = j * NSA_SEL_BLOCK
    cur_blk = (start <= t) & (t < start + NSA_SEL_BLOCK)
    prev_blk = (start + NSA_SEL_BLOCK <= t) & (t < start + 2 * NSA_SEL_BLOCK)
    forced = (j == 0) | cur_blk | prev_blk
    score = jnp.where(forced, NSA_FORCE_SCORE, jnp.where(start <= t, imp, -NSA_FORCE_SCORE))
    jf = j.astype(F32)
    sel = jnp.zeros((Tq, n_sel), jnp.bool_)
    for _ in range(min(NSA_TOPK, n_sel)):
        mx = jnp.max(score, axis=-1, keepdims=True)
        first = jnp.min(jnp.where(score == mx, jf, float(n_sel)), axis=-1, keepdims=True)
        hit = jf == first
        sel = sel | hit
        score = jnp.where(hit, -3e38, score)
    eye = eye_ref[...]
    sel_bias_t = _dot_nt(eye[:n_sel, :n_sel], jnp.where(sel, 0.0, SEL_NEG).astype(BF16)).astype(BF16)
    for h in range(NSA_HEADS):
        qh = (q[:, h * HEAD_DIM:(h + 1) * HEAD_DIM] * (HEAD_DIM ** -0.5 * LOG2E)).astype(BF16)
        qh_t = _dot_nt(eye[:HEAD_DIM, :HEAD_DIM], qh).astype(BF16)
        for g in range(n_sel // nbg):
            qaug_ref[g, h] = jnp.concatenate([qh_t, sel_bias_t[g * nbg:(g + 1) * nbg]], axis=0)


def nsa_cmp_topk(nq, kc, vc, overlap, eye, Tq, nbg):
    S = nq.shape[0]
    n_cmp = kc.shape[0]
    n_sel = S // NSA_SEL_BLOCK
    G = n_sel // nbg
    return pl.pallas_call(
        functools.partial(_cmp_topk_kernel, Tq=Tq, n_cmp=n_cmp, n_sel=n_sel, nbg=nbg),
        grid=(S // Tq,),
        in_specs=[pl.BlockSpec((Tq, NSA_W), lambda i: (i, 0)),
                  pl.BlockSpec((n_cmp, HEAD_DIM), lambda i: (0, 0)),
                  pl.BlockSpec((n_cmp, HEAD_DIM), lambda i: (0, 0)),
                  pl.BlockSpec((n_cmp, n_sel), lambda i: (0, 0)),
                  pl.BlockSpec(eye.shape, lambda i: (0, 0))],
        out_specs=[pl.BlockSpec((Tq, NSA_W), lambda i: (i, 0)),
                   pl.BlockSpec((G, NSA_HEADS, HEAD_DIM + nbg, Tq), lambda i: (0, 0, 0, i))],
        out_shape=[jax.ShapeDtypeStruct((S, NSA_W), F32),
                   jax.ShapeDtypeStruct((G, NSA_HEADS, HEAD_DIM + nbg, S), BF16)],
        compiler_params=_cparams(("parallel",)),
    )(nq, kc, vc, overlap, eye)


def _outproj_even_kernel(oa_ref, ob_ref, wa_ref, wb_ref, g_ref, h_ref, out_ref):
    m = _dot(oa_ref[...], wa_ref[...]) + _dot(ob_ref[...], wb_ref[...])
    out_ref[...] = h_ref[...] + _rms(m, g_ref[...])


def outproj_even(oa, ob, wa, wb, g, h, tm):
    S, D = h.shape
    rows = lambda w: pl.BlockSpec((tm, w), lambda i: (i, 0))
    full = lambda a: pl.BlockSpec(a.shape, lambda i: (0,) * a.ndim)
    return pl.pallas_call(
        _outproj_even_kernel,
        grid=(S // tm,),
        in_specs=[rows(oa.shape[1]), rows(ob.shape[1]), full(wa), full(wb), full(g), rows(D)],
        out_specs=rows(D),
        out_shape=jax.ShapeDtypeStruct((S, D), F32),
        compiler_params=_cparams(("parallel",)),
    )(oa, ob, wa, wb, g, h)


def _outproj_odd_kernel(o0_ref, o1_ref, o2_ref, l0_ref, l1_ref, l2_ref, oc_ref, os_ref, ow_ref, gt_ref,
                        wd_ref, wn_ref, g_ref, h_ref, out_ref):
    l0, l1, l2 = l0_ref[...], l1_ref[...], l2_ref[...]
    mx = jnp.maximum(jnp.maximum(l0, l1), l2)
    e0, e1, e2 = jnp.exp(l0 - mx), jnp.exp(l1 - mx), jnp.exp(l2 - mx)
    o_dil = (e0 * o0_ref[...] + e1 * o1_ref[...] + e2 * o2_ref[...]) / (e0 + e1 + e2)
    gt = jax.nn.sigmoid(gt_ref[...])
    o_nsa = (gt[:, 0:NSA_W] * oc_ref[...] + gt[:, NSA_W:2 * NSA_W] * os_ref[...].astype(F32)
             + gt[:, 2 * NSA_W:3 * NSA_W] * ow_ref[...])
    m = _dot(o_dil.astype(BF16), wd_ref[...]) + _dot(o_nsa.astype(BF16), wn_ref[...])
    out_ref[...] = h_ref[...] + _rms(m, g_ref[...])


def outproj_odd(o_dil, lse_dil, o_cmp, o_sel, o_win, gates, wd, wn, g, h, tm):
    S, D = h.shape
    rows = lambda w: pl.BlockSpec((tm, w), lambda i: (i, 0))
    full = lambda a: pl.BlockSpec(a.shape, lambda i: (0,) * a.ndim)
    return pl.pallas_call(
        _outproj_odd_kernel,
        grid=(S // tm,),
        in_specs=[rows(DIL_GW)] * 6 + [rows(NSA_W)] * 3 + [rows(3 * NSA_W), full(wd), full(wn), full(g), rows(D)],
        out_specs=rows(D),
        out_shape=jax.ShapeDtypeStruct((S, D), F32),
        compiler_params=_cparams(("parallel",)),
    )(*o_dil, *lse_dil, o_cmp, o_sel, o_win, gates, wd, wn, g, h)


def _mlp_ple_kernel(h_ref, g1_ref, wu_ref, wd_ref, g2_ref, g3_ref, wg_ref, p_ref, wp_ref, out_ref,
                    hn_ref, acc_ref):
    j = pl.program_id(1)

    @pl.when(j == 0)
    def _():
        hn_ref[...] = _rms(h_ref[...], g1_ref[...]).astype(BF16)
        acc_ref[...] = jnp.zeros(acc_ref.shape, F32)

    u = jnp.maximum(_dot(hn_ref[...], wu_ref[...]), 0.0)
    acc_ref[...] += _dot((u * u).astype(BF16), wd_ref[...])

    @pl.when(j == pl.num_programs(1) - 1)
    def _():
        h2 = h_ref[...] + _rms(acc_ref[...], g2_ref[...])
        gate = jax.nn.sigmoid(_dot(_rms(h2, g3_ref[...]).astype(BF16), wg_ref[...]))
        out_ref[...] = h2 + gate * _dot(p_ref[...].astype(BF16), wp_ref[...])


def mlp_ple(h, g1, wu, wd, g2, g3, wg, p, wp, tm, tf):
    S, D = h.shape
    FF = wu.shape[1]
    PD = p.shape[1]
    const = lambda a: pl.BlockSpec(a.shape, lambda i, j: (0,) * a.ndim)
    return pl.pallas_call(
        _mlp_ple_kernel,
        grid=(S // tm, FF // tf),
        in_specs=[pl.BlockSpec((tm, D), lambda i, j: (i, 0)), const(g1),
                  pl.BlockSpec((D, tf), lambda i, j: (0, j)),
                  pl.BlockSpec((tf, D), lambda i, j: (j, 0)),
                  const(g2), const(g3), const(wg),
                  pl.BlockSpec((tm, PD), lambda i, j: (i, 0)), const(wp)],
        out_specs=pl.BlockSpec((tm, D), lambda i, j: (i, 0)),
        out_shape=jax.ShapeDtypeStruct((S, D), F32),
        scratch_shapes=[pltpu.VMEM((tm, D), BF16), pltpu.VMEM((tm, D), F32)],
        compiler_params=_cparams(("parallel", "arbitrary")),
    )(h, g1, wu, wd, g2, g3, wg, p, wp)


def _t5_bucket_of(dist):
    n = jnp.maximum(dist, 0)
    max_exact = T5_BUCKETS // 2
    ratio = jnp.log(jnp.maximum(n, 1).astype(F32) / max_exact) / math.log(T5_MAX_DIST / max_exact)
    large = jnp.minimum(max_exact + (ratio * (T5_BUCKETS - max_exact)).astype(jnp.int32), T5_BUCKETS - 1)
    return jnp.where(n < max_exact, n, large)


def _bias_of_dist(bias, dist):
    bucket = _t5_bucket_of(dist)[None]
    out = jnp.zeros((bias.shape[1],) + dist.shape, F32)
    for b in range(T5_BUCKETS):
        out = jnp.where(bucket == b, bias[b].reshape((-1,) + (1,) * dist.ndim), out)
    return out


def _causal_tables(T):
    k = jnp.arange(T)[:, None]
    q = jnp.arange(T)[None, :]
    diag = jnp.where(q >= k, 0.0, NEG_INF).astype(F32)
    return jnp.stack([diag, jnp.zeros((T, T), F32)])[:, None]


def _t5_delta_tables(bias, T):
    n_delta = -(-(T5_MAX_DIST - 1) // T) + 2
    k = jnp.arange(T)[None, :, None]
    q = jnp.arange(T)[None, None, :]
    dist = jnp.arange(n_delta)[:, None, None] * T + q - k
    val = (_bias_of_dist(bias, dist) - bias[T5_BUCKETS - 1].reshape(-1, 1, 1, 1)) * LOG2E
    return jnp.transpose(jnp.where(dist[None] >= 0, val, NEG_INF), (1, 0, 2, 3))


def _band_table(bias, Ta, n_prev, max_rel, stride, inclusive):
    P = n_prev * Ta
    i = jnp.arange(Ta)[:, None]
    c = jnp.arange(P + Ta)[None, :]
    rel = i + P - c
    ok = (rel >= 0) & ((rel <= max_rel) if inclusive else (rel < max_rel))
    return jnp.where(ok[None], _bias_of_dist(bias, rel * stride), NEG_INF)


def _tile(S, pref):
    t = min(pref, S)
    assert S % t == 0
    return t


def even_mixer_core(y, cos, sin, w_uq, q_norm, w_ukv, kv_norm, forget_bias, T):
    S = y.shape[0]
    scale = MLA_QK_DIM ** -0.5 * LOG2E
    half = MLA_ROPE_DIM // 2
    cos_t, sin_t = jnp.transpose(cos), jnp.transpose(sin)
    ones = jnp.ones((MLA_NOPE_DIM, S), F32)
    cq = jnp.concatenate([ones, cos_t, cos_t], axis=0) * scale
    sq = jnp.concatenate([0.0 * ones, -sin_t, sin_t], axis=0) * scale
    ck = jnp.concatenate([cos, cos], axis=1)
    sk = jnp.concatenate([-sin, sin], axis=1)
    wq = jnp.transpose(w_uq.reshape(MLA_Q_RANK, MLA_HEADS, MLA_QK_DIM), (1, 2, 0))
    swap = np.concatenate([np.arange(MLA_NOPE_DIM), MLA_NOPE_DIM + half + np.arange(half),
                           MLA_NOPE_DIM + np.arange(half)])
    wqs = wq[:, swap, :]
    wkv = jnp.transpose(w_ukv.reshape(MLA_KV_RANK, MLA_HEADS, MLA_NOPE_DIM + MLA_V_DIM), (1, 0, 2))
    wk = jnp.pad(wkv[:, :, :MLA_NOPE_DIM], ((0, 0), (0, 0), (0, MLA_ROPE_DIM)))
    wv = jnp.transpose(wkv[:, :, MLA_NOPE_DIM:], (0, 2, 1))
    e96 = jnp.pad(jnp.eye(MLA_ROPE_DIM, dtype=F32), ((0, 0), (MLA_NOPE_DIM, 0)))
    qm, km, vm, fq, fk, fv, logf = even_prep(
        y, wq.astype(BF16), wqs.astype(BF16), wk.astype(BF16), e96.astype(BF16), wv.astype(BF16),
        jnp.eye(HEAD_DIM, dtype=BF16), q_norm[None, :], kv_norm[None, :], forget_bias[None, :],
        cq, sq, ck, sk, _tile(S, 512))
    neg_f = cumsum_split(jnp.transpose(logf))
    fk_aug = jnp.concatenate([fk, jnp.transpose(neg_f, (1, 2, 0)),
                              jnp.zeros((FOX_HEADS, S, FOX_PAD - FOX_SPLIT), BF16)], axis=2)
    causal = _causal_tables(T)
    o_mla = flash_causal(qm[None], km, vm, causal, T, hps=4)
    o_fox = flash_causal(fq[None], fk_aug, fv, causal, T, hps=4)
    return o_mla, o_fox


def odd_mixer_core(y, t5_bias, pos_k, pos_v, w1k, w2k, w1v, w2v, T):
    S = y.shape[0]
    G = len(DIL_GROUPS)
    Ta = 128
    o_dil, lse_dil = [], []
    for g, (w, d) in enumerate(DIL_GROUPS):
        def deint(a):
            return jnp.transpose(a.reshape(S // d, d, DIL_GW), (1, 0, 2))
        q, k, v = (deint(y[:, (b * G + g) * DIL_GW:(b * G + g + 1) * DIL_GW]) for b in range(3))
        btab = _band_table(t5_bias[:, g * DIL_HEADS:(g + 1) * DIL_HEADS], Ta, 1, w // d, d, True)
        o, lse = band_attention(q, k, v, btab, _tile(S // d, Ta), 1, True)
        o_dil.append(jnp.transpose(o, (1, 0, 2)).reshape(S, DIL_GW))
        lse_dil.append(jnp.transpose(lse, (1, 0, 2)).reshape(S, DIL_GW))
    base = 3 * G * DIL_GW
    nq = y[:, base:base + NSA_W]
    k_cmp, v_cmp, k_sel, v_sel, k_win, v_win = (
        y[:, base + NSA_W + i * HEAD_DIM:base + NSA_W + (i + 1) * HEAD_DIM] for i in range(6))
    gates = y[:, base + NSA_W + 6 * HEAD_DIM:base + NSA_W + 6 * HEAD_DIM + 3 * NSA_W]
    bias_nsa = t5_bias[:, G * DIL_HEADS:]
    n_chunk = S // NSA_CMP_STRIDE
    cw = NSA_CMP_STRIDE * HEAD_DIM
    kc, vc = nsa_compress(k_cmp.reshape(n_chunk, cw), v_cmp.reshape(n_chunk, cw),
                          pos_k.reshape(2, cw), pos_v.reshape(2, cw),
                          w1k.reshape(2, cw, -1).astype(BF16), w2k.astype(BF16),
                          w1v.reshape(2, cw, -1).astype(BF16), w2v.astype(BF16))
    n_sel = S // NSA_SEL_BLOCK
    ci = np.arange(n_chunk)[:, None] * NSA_CMP_STRIDE
    sj = np.arange(n_sel)[None, :] * NSA_SEL_BLOCK
    overlap = jnp.asarray(((ci < sj + NSA_SEL_BLOCK) & (ci + NSA_CMP_LEN > sj)).astype(np.float32), BF16)
    nbg = min(n_sel, NSA_SEL_GROUP_BLOCKS)
    o_cmp, qaug = nsa_cmp_topk(nq, kc, vc, overlap, jnp.eye(max(n_sel, HEAD_DIM), dtype=BF16),
                               _tile(S, 128), nbg)
    onehot = jnp.asarray(((np.arange(S)[:, None] // NSA_SEL_BLOCK) % nbg == np.arange(nbg)[None, :])
                         .astype(np.float32), BF16)
    kaug = jnp.concatenate([k_sel.astype(BF16), onehot], axis=1)
    o_sel = flash_causal(qaug, kaug[None], jnp.transpose(v_sel).astype(BF16)[None],
                         _t5_delta_tables(bias_nsa, T), T, hps=NSA_HEADS, key_group_tokens=nbg * NSA_SEL_BLOCK)
    Tw = _tile(S, 256)
    n_prev = NSA_WINDOW // Tw
    wtab = _band_table(bias_nsa, Tw, n_prev, NSA_WINDOW, 1, False)
    o_win = band_attention(nq[None], k_win[None], v_win[None], wtab, Tw, n_prev, False)[0]
    return o_dil, lse_dil, o_cmp, o_sel, o_win, gates


def _pad_cols(w, n):
    return jnp.pad(w, ((0, 0), (0, n - w.shape[1])))


def _even_w_in(w):
    cq, ckv, kr, fq, fk, fv, fl = jnp.split(w, np.cumsum([256, 128, 32, 512, 512, 512])[:], axis=1)
    half = MLA_ROPE_DIM // 2
    kr_sw = jnp.concatenate([kr[:, half:], kr[:, :half]], axis=1)
    head = _pad_cols(jnp.concatenate([cq, ckv, kr, kr_sw, fl], axis=1), 512)
    return jnp.concatenate([head, fq, fk, fv], axis=1).astype(BF16)


def _odd_w_in(w):
    main = w[:, :3 * 3 * DIL_GW + NSA_W + 6 * HEAD_DIM]
    gl = w[:, 3 * 3 * DIL_GW + NSA_W + 6 * HEAD_DIM:]
    gl = jnp.transpose(gl.reshape(-1, NSA_HEADS, 3), (0, 2, 1))
    gl = jnp.repeat(gl[..., None], HEAD_DIM, axis=-1).reshape(w.shape[0], 3 * NSA_W)
    return _pad_cols(jnp.concatenate([main, gl], axis=1), ODD_IN_PAD).astype(BF16)


def _trunk(x, p, positions, t5_bias, ev_w_in, ev_q_norm, ev_w_uq, ev_kv_norm, ev_w_ukv, ev_forget_bias,
           ev_w_out, od_w_in, od_cmp_pos_k, od_cmp_pos_v, od_cmp_w1_k, od_cmp_w2_k, od_cmp_w1_v, od_cmp_w2_v,
           od_w_out, norm_mix_pre, norm_mix_post, norm_mlp_pre, norm_mlp_post, w_mlp_up, w_mlp_down,
           ple_norm, w_ple_gate, w_ple_proj):
    S, D = x.shape
    depth = p.shape[0]
    T = _tile(S, 512)
    tm = _tile(S, 512)
    inv_freq = ROPE_THETA ** (-jnp.arange(0, MLA_ROPE_DIM, 2, dtype=F32) / MLA_ROPE_DIM)
    angles = positions.astype(F32)[:, None] * inv_freq
    cos, sin = jnp.cos(angles), jnp.sin(angles)
    h = x
    for i in range(depth):
        j = i // 2
        if i % 2 == 0:
            y = norm_matmul(h, norm_mix_pre[i][None], _even_w_in(ev_w_in[j]), tm, 512)
            o_mla, o_fox = even_mixer_core(y, cos, sin, ev_w_uq[j], ev_q_norm[j], ev_w_ukv[j], ev_kv_norm[j],
                                           ev_forget_bias[j], T)
            wo = ev_w_out[j].astype(BF16)
            na = MLA_HEADS * MLA_V_DIM
            h = outproj_even(o_mla, o_fox, wo[:na], wo[na:], norm_mix_post[i][None], h, tm)
        else:
            y = norm_matmul(h, norm_mix_pre[i][None], _odd_w_in(od_w_in[j]), tm, 768)
            o_dil, lse_dil, o_cmp, o_sel, o_win, gates = odd_mixer_core(
                y, t5_bias, od_cmp_pos_k[j], od_cmp_pos_v[j], od_cmp_w1_k[j], od_cmp_w2_k[j],
                od_cmp_w1_v[j], od_cmp_w2_v[j], T)
            wo = od_w_out[j].astype(BF16)
            h = outproj_odd(o_dil, lse_dil, o_cmp, o_sel, o_win, gates, wo[:DIL_GW], wo[DIL_GW:],
                            norm_mix_post[i][None], h, tm)
        h = mlp_ple(h, norm_mlp_pre[i][None], w_mlp_up[i].astype(BF16), w_mlp_down[i].astype(BF16),
                    norm_mlp_post[i][None], ple_norm[i][None], w_ple_gate[i].astype(BF16), p[i],
                    w_ple_proj[i].astype(BF16), tm, 1024)
    return h


def kernel(x, p, positions, t5_bias, ev_w_in, ev_q_norm, ev_w_uq, ev_kv_norm, ev_w_ukv, ev_forget_bias, ev_w_out, od_w_in, od_cmp_pos_k, od_cmp_pos_v, od_cmp_w1_k, od_cmp_w2_k, od_cmp_w1_v, od_cmp_w2_v, od_w_out, norm_mix_pre, norm_mix_post, norm_mlp_pre, norm_mlp_post, w_mlp_up, w_mlp_down, ple_norm, w_ple_gate, w_ple_proj):
    params = (t5_bias, ev_w_in, ev_q_norm, ev_w_uq, ev_kv_norm, ev_w_ukv, ev_forget_bias, ev_w_out, od_w_in,
              od_cmp_pos_k, od_cmp_pos_v, od_cmp_w1_k, od_cmp_w2_k, od_cmp_w1_v, od_cmp_w2_v, od_w_out,
              norm_mix_pre, norm_mix_post, norm_mlp_pre, norm_mlp_post, w_mlp_up, w_mlp_down, ple_norm,
              w_ple_gate, w_ple_proj)
    outs = [_trunk(x[b], p[:, b], positions[b], *params) for b in range(x.shape[0])]
    return jnp.stack(outs).astype(x.dtype)
```

```python
import functools
import math

import numpy as np
import jax
import jax.numpy as jnp
from jax import lax
from jax.experimental import pallas as pl
from jax.experimental.pallas import tpu as pltpu

F32 = jnp.float32
BF16 = jnp.bfloat16

HEAD_DIM = 64
RMS_EPS = 1e-6
NEG_INF = -1e30
SEL_NEG = -(2.0 ** 99)

MLA_HEADS = 8
MLA_NOPE_DIM = 64
MLA_ROPE_DIM = 32
MLA_V_DIM = 64
MLA_Q_RANK = 256
MLA_KV_RANK = 128
MLA_QK_DIM = MLA_NOPE_DIM + MLA_ROPE_DIM
ROPE_THETA = 10000.0
FOX_HEADS = 8
FOX_SPLIT = 3
FOX_PAD = 16
LOG2E = math.log2(math.e)
STAT_ROWS = 16
FAST_OVER = 64.0
FAST_MARGIN = 150.0
P_CHUNK = 32
DIL_GROUPS = ((128, 1), (512, 4), (2048, 16))
DIL_HEADS = 4
DIL_GW = DIL_HEADS * HEAD_DIM
NSA_HEADS = 4
NSA_W = NSA_HEADS * HEAD_DIM
NSA_CMP_LEN = 32
NSA_CMP_STRIDE = 16
NSA_SEL_BLOCK = 64
NSA_TOPK = 16
NSA_WINDOW = 512
NSA_FORCE_SCORE = 1e9
NSA_SEL_GROUP_BLOCKS = 128
T5_BUCKETS = 32
T5_MAX_DIST = 2048

EVEN_IN_PAD = 2048
ODD_IN_PAD = 3840

VMEM_LIMIT_BYTES = 56 * 1024 * 1024


def _cparams(sem):
    return pltpu.CompilerParams(dimension_semantics=sem, vmem_limit_bytes=VMEM_LIMIT_BYTES)


def _rms(x, g):
    return x * lax.rsqrt(jnp.mean(x * x, axis=-1, keepdims=True) + RMS_EPS) * g


def _dot(a, b):
    return jnp.dot(a, b, preferred_element_type=F32)


def _dot_nt(a, b):
    return lax.dot_general(a, b, (((1,), (1,)), ((), ())), preferred_element_type=F32)


def _norm_matmul_kernel(h_ref, g_ref, w_ref, o_ref, hn_ref):
    @pl.when(pl.program_id(1) == 0)
    def _():
        hn_ref[...] = _rms(h_ref[...], g_ref[...]).astype(BF16)

    o_ref[...] = _dot(hn_ref[...], w_ref[...])


def norm_matmul(h, g, w, tm, tn):
    S, D = h.shape
    N = w.shape[1]
    return pl.pallas_call(
        _norm_matmul_kernel,
        grid=(S // tm, N // tn),
        in_specs=[pl.BlockSpec((tm, D), lambda i, j: (i, 0)),
                  pl.BlockSpec((1, D), lambda i, j: (0, 0)),
                  pl.BlockSpec((D, tn), lambda i, j: (0, j))],
        out_specs=pl.BlockSpec((tm, tn), lambda i, j: (i, j)),
        out_shape=jax.ShapeDtypeStruct((S, N), F32),
        scratch_shapes=[pltpu.VMEM((tm, D), BF16)],
        compiler_params=_cparams(("parallel", "arbitrary")),
    )(h, g, w)


def _even_prep_kernel(y_ref, wq_ref, wqs_ref, wk_ref, e_ref, wv_ref, eye_ref, qn_ref, kvn_ref, fb_ref,
                      cq_ref, sq_ref, ck_ref, sk_ref,
                      qm_ref, km_ref, vm_ref, fq_ref, fk_ref, fv_ref, lf_ref, st_ref):
    y = y_ref[...]
    tm = y.shape[0]
    c_q = y[:, 0:MLA_Q_RANK]
    c_kv = y[:, MLA_Q_RANK:MLA_Q_RANK + MLA_KV_RANK]
    o = MLA_Q_RANK + MLA_KV_RANK
    k_r = y[:, o:o + MLA_ROPE_DIM]
    k_rs = y[:, o + MLA_ROPE_DIM:o + 2 * MLA_ROPE_DIM]
    f_logit = y[:, o + 2 * MLA_ROPE_DIM:o + 2 * MLA_ROPE_DIM + FOX_HEADS]
    nq = _rms(c_q, qn_ref[...]).astype(BF16)
    nkv = _rms(c_kv, kvn_ref[...]).astype(BF16)
    k_rot = (k_r * ck_ref[...] + k_rs * sk_ref[...]).astype(BF16)
    k_rot96 = _dot(k_rot, e_ref[...])
    cq = cq_ref[...]
    sq = sq_ref[...]
    eye = eye_ref[...]
    row = lax.broadcasted_iota(jnp.int32, (STAT_ROWS, tm), 0)
    ones_row = jnp.where(row < 1, 1.0, 0.0).astype(BF16)
    stats = [[] for _ in range(6)]

    def record(slot, q_t, k):
        q_t = q_t.astype(F32)
        k_t = _dot_nt(eye[:k.shape[1], :k.shape[1]], k)
        stats[slot].append(jnp.sum(q_t * q_t, axis=0, keepdims=True))
        stats[slot + 1].append(jnp.sum(k_t * k_t, axis=0, keepdims=True))
        stats[slot + 2].append(jnp.sum(q_t * k_t, axis=0, keepdims=True))

    for h in range(MLA_HEADS):
        q = (_dot_nt(wq_ref[h], nq) * cq + _dot_nt(wqs_ref[h], nq) * sq).astype(BF16)
        k = (_dot(nkv, wk_ref[h]) + k_rot96).astype(BF16)
        qm_ref[h] = q
        km_ref[h] = k
        vm_ref[h] = jnp.concatenate([_dot_nt(wv_ref[h], nkv).astype(BF16), ones_row], axis=0)
        record(0, q, k)
    base = 512
    fw = FOX_HEADS * HEAD_DIM
    eye64 = eye[:HEAD_DIM, :HEAD_DIM]
    ones_rows = jnp.where(row < FOX_SPLIT, 1.0, 0.0).astype(BF16)
    for h in range(FOX_HEADS):
        lo = base + h * HEAD_DIM
        fq = _dot_nt(eye64, (y[:, lo:lo + HEAD_DIM] * (HEAD_DIM ** -0.5 * LOG2E)).astype(BF16)).astype(BF16)
        fk = y[:, lo + fw:lo + fw + HEAD_DIM].astype(BF16)
        fq_ref[h] = jnp.concatenate([fq, ones_rows], axis=0)
        fk_ref[h] = fk
        fv = y[:, lo + 2 * fw:lo + 2 * fw + HEAD_DIM].astype(BF16)
        fv_ref[h] = jnp.concatenate([_dot_nt(eye64, fv).astype(BF16), ones_row], axis=0)
        record(3, fq, fk)
    for i in range(6):
        st_ref[i] = jnp.concatenate(stats[i], axis=0)
    z = f_logit + fb_ref[...]
    lf_ref[...] = jnp.minimum(z, 0.0) - jnp.log1p(jnp.exp(-jnp.abs(z)))


def even_prep(y, wq, wqs, wk, e96, wv, eye, qn, kvn, fb, cq, sq, ck, sk, tm):
    S = y.shape[0]
    full = lambda a: pl.BlockSpec(a.shape, lambda i: (0,) * a.ndim)
    rows = lambda w: pl.BlockSpec((tm, w), lambda i: (i, 0))
    cols = lambda w: pl.BlockSpec((w, tm), lambda i: (0, i))
    heads = lambda n, w: pl.BlockSpec((n, tm, w), lambda i: (0, i, 0))
    heads_t = lambda n, w: pl.BlockSpec((n, w, tm), lambda i: (0, 0, i))
    out_shape = [jax.ShapeDtypeStruct((MLA_HEADS, MLA_QK_DIM, S), BF16),
                 jax.ShapeDtypeStruct((MLA_HEADS, S, MLA_QK_DIM), BF16),
                 jax.ShapeDtypeStruct((MLA_HEADS, MLA_V_DIM + STAT_ROWS, S), BF16),
                 jax.ShapeDtypeStruct((FOX_HEADS, HEAD_DIM + FOX_PAD, S), BF16),
                 jax.ShapeDtypeStruct((FOX_HEADS, S, HEAD_DIM), BF16),
                 jax.ShapeDtypeStruct((FOX_HEADS, HEAD_DIM + STAT_ROWS, S), BF16),
                 jax.ShapeDtypeStruct((S, FOX_HEADS), F32),
                 jax.ShapeDtypeStruct((6, MLA_HEADS, S), F32)]
    out_specs = [heads_t(MLA_HEADS, MLA_QK_DIM), heads(MLA_HEADS, MLA_QK_DIM),
                 heads_t(MLA_HEADS, MLA_V_DIM + STAT_ROWS),
                 heads_t(FOX_HEADS, HEAD_DIM + FOX_PAD), heads(FOX_HEADS, HEAD_DIM),
                 heads_t(FOX_HEADS, HEAD_DIM + STAT_ROWS),
                 rows(FOX_HEADS), heads_t(6, MLA_HEADS)]
    return pl.pallas_call(
        _even_prep_kernel,
        grid=(S // tm,),
        in_specs=[rows(EVEN_IN_PAD), full(wq), full(wqs), full(wk), full(e96), full(wv), full(eye), full(qn),
                  full(kvn), full(fb), cols(MLA_QK_DIM), cols(MLA_QK_DIM), rows(MLA_ROPE_DIM), rows(MLA_ROPE_DIM)],
        out_specs=out_specs,
        out_shape=out_shape,
        compiler_params=_cparams(("parallel",)),
    )(y, wq, wqs, wk, e96, wv, eye, qn, kvn, fb, cq, sq, ck, sk)


def _cumsum_split_kernel(x_ref, o_ref):
    x = x_ref[...]
    n = x.shape[1]
    lane = lax.broadcasted_iota(jnp.int32, x.shape, 1)
    shift = 1
    while shift < n:
        x = x + jnp.where(lane >= shift, pltpu.roll(x, shift, axis=1), 0.0)
        shift *= 2
    r = -x * LOG2E
    for i in range(FOX_SPLIT):
        part = r.astype(BF16)
        o_ref[i] = part
        r = r - part.astype(F32)


def cumsum_split(x):
    return pl.pallas_call(
        _cumsum_split_kernel,
        out_shape=jax.ShapeDtypeStruct((FOX_SPLIT,) + x.shape, BF16),
        compiler_params=_cparams(None),
    )(x)


def _flash_kernel(qt_ref, kt_ref, fast_ref, q_ref, k_ref, v_ref, bt_ref, o_ref, m_scr, acc_scr, p_scr, *,
                  hps, shared_kv, n_delta, nt):
    step = pl.program_id(1)
    qi = qt_ref[step]
    ki = kt_ref[step]
    fast = fast_ref[pl.program_id(0) * nt + qi] != 0
    near = qi - ki < n_delta - 1

    @pl.when(ki == 0)
    def _():
        m_scr[...] = jnp.full(m_scr.shape, -3e38, F32)
        acc_scr[...] = jnp.zeros(acc_scr.shape, F32)

    hb = bt_ref.shape[1]
    Tk = k_ref.shape[1]
    Dv = v_ref.shape[1] - STAT_ROWS

    def logits(h, with_table):
        s = _dot(k_ref[0 if shared_kv else h], q_ref[0, h])
        return s + bt_ref[0, h if hb > 1 else 0] if with_table else s

    def store_p(h, s, m):
        for c in range(Tk // P_CHUNK):
            rows = slice(c * P_CHUNK, (c + 1) * P_CHUNK)
            x = s[rows] if m is None else s[rows] - m
            p_scr[h, rows, :] = jnp.exp2(x).astype(BF16)

    def fast_update(with_table):
        acc_prev = [acc_scr[h] for h in range(hps)]
        for h in range(hps):
            store_p(h, logits(h, with_table), None)
        for h in range(hps):
            acc_scr[h] = acc_prev[h] + _dot(v_ref[0 if shared_kv else h], p_scr[h])

    def safe_update(with_table):
        m_prev = [m_scr[h] for h in range(hps)]
        acc_prev = [acc_scr[h] for h in range(hps)]
        ss = [logits(h, with_table) for h in range(hps)]
        ms = [jnp.maximum(m_prev[h], jnp.max(ss[h], axis=0, keepdims=True)) for h in range(hps)]
        for h in range(hps):
            store_p(h, ss[h], ms[h])
        for h in range(hps):
            alpha = jnp.exp2(m_prev[h] - ms[h])
            acc_scr[h] = alpha * acc_prev[h] + _dot(v_ref[0 if shared_kv else h], p_scr[h])
            m_scr[h] = ms[h]

    for take_fast, update in ((True, fast_update), (False, safe_update)):
        for with_table in (True, False):
            cond = jnp.logical_and(fast == take_fast, near == with_table)
            pl.when(cond)(functools.partial(update, with_table))

    @pl.when(ki == qi)
    def _():
        outs = [(acc_scr[h, :Dv, :] / acc_scr[h, Dv:Dv + 1, :]).T for h in range(hps)]
        o_ref[...] = jnp.concatenate(outs, axis=1).astype(o_ref.dtype)


def flash_causal(qT, k, vT, btab, fast, T, hps, key_group_tokens=None, out_dtype=BF16):
    G, H, Dk, S = qT.shape
    Hk, Dv, _ = vT.shape
    Dv -= STAT_ROWS
    shared_kv = Hk == 1
    n_delta, Hb = btab.shape[0], btab.shape[1]
    nt = S // T
    qt = np.array([i for i in range(nt) for _ in range(i + 1)], np.int32)
    kt = np.array([j for i in range(nt) for j in range(i + 1)], np.int32)
    kv_h = 1 if shared_kv else hps
    tiles_per_group = (key_group_tokens // T) if G > 1 else 1

    def q_map(g, s, qt, kt, fast):
        return (kt[s] // tiles_per_group if G > 1 else 0, g, 0, qt[s])

    def bt_map(g, s, qt, kt, fast):
        return (jnp.minimum(qt[s] - kt[s], n_delta - 1), g if Hb > hps else 0, 0, 0)

    in_specs = [pl.BlockSpec((1, hps, Dk, T), q_map),
                pl.BlockSpec((kv_h, T, Dk), lambda g, s, qt, kt, fast: (0 if shared_kv else g, kt[s], 0)),
                pl.BlockSpec((kv_h, Dv + STAT_ROWS, T), lambda g, s, qt, kt, fast: (0 if shared_kv else g, 0, kt[s])),
                pl.BlockSpec((1, min(Hb, hps), T, T), bt_map)]
    grid_spec = pltpu.PrefetchScalarGridSpec(
        num_scalar_prefetch=3,
        grid=(H // hps, len(qt)),
        in_specs=in_specs,
        out_specs=pl.BlockSpec((T, hps * Dv), lambda g, s, qt, kt, fast: (qt[s], g)),
        scratch_shapes=[pltpu.VMEM((hps, 1, T), F32), pltpu.VMEM((hps, Dv + STAT_ROWS, T), F32),
                        pltpu.VMEM((hps, T, T), BF16)],
    )
    return pl.pallas_call(
        functools.partial(_flash_kernel, hps=hps, shared_kv=shared_kv, n_delta=n_delta, nt=nt),
        grid_spec=grid_spec,
        out_shape=jax.ShapeDtypeStruct((S, H * Dv), out_dtype),
        compiler_params=_cparams(("parallel", "arbitrary")),
    )(jnp.asarray(qt), jnp.asarray(kt), fast, qT, k, vT, btab)


def _reference_rows(qn2, kn2, l_self, extra, bias_max, bias_self, T, hps):
    H, S = qn2.shape
    kmax = jnp.sqrt(jnp.max(kn2, axis=1, keepdims=True))
    bound = jnp.sqrt(qn2) * kmax + bias_max + extra
    ok = bound - (l_self + bias_self + extra) <= FAST_MARGIN
    fast = jnp.all(ok.reshape(H // hps, hps, S // T, T), axis=(1, 3)).astype(jnp.int32).reshape(-1)
    r = FAST_OVER - bound
    parts = []
    for _ in range(3):
        part = r.astype(BF16)
        parts.append(part)
        r = r - part.astype(F32)
    rows = jnp.stack(parts + [jnp.zeros_like(parts[0])] * (STAT_ROWS - 3), axis=1)
    return rows, fast


def _stat_cols(Hk, S):
    return jnp.broadcast_to((jnp.arange(STAT_ROWS) < 3).astype(BF16), (Hk, S, STAT_ROWS))


def _band_kernel(*refs, n_prev, Ta, H, shared_kv, want_lse):
    q_ref = refs[0]
    k_refs = refs[1:2 + n_prev]
    v_refs = refs[2 + n_prev:3 + 2 * n_prev]
    bt_ref = refs[3 + 2 * n_prev]
    o_ref = refs[4 + 2 * n_prev]
    lse_ref = refs[5 + 2 * n_prev] if want_lse else None
    ai = pl.program_id(1)
    k = jnp.concatenate([r[...] for r in k_refs], axis=0).astype(BF16)
    v = jnp.concatenate([r[...] for r in v_refs], axis=0).astype(BF16)
    W = (n_prev + 1) * Ta
    col = lax.broadcasted_iota(jnp.int32, (1, W), 1)
    col_valid = (ai - n_prev) * Ta + col >= 0
    q = q_ref[...]
    outs, lses = [], []
    for h in range(H):
        hk = 0 if shared_kv else h
        qh = (q[:, h * HEAD_DIM:(h + 1) * HEAD_DIM] * (HEAD_DIM ** -0.5)).astype(BF16)
        s = _dot_nt(qh, k[:, hk * HEAD_DIM:(hk + 1) * HEAD_DIM]) + bt_ref[h]
        s = jnp.where(col_valid, s, NEG_INF)
        m = jnp.max(s, axis=-1, keepdims=True)
        p = jnp.exp(s - m)
        l = jnp.sum(p, axis=-1, keepdims=True)
        outs.append(_dot(p.astype(BF16), v[:, hk * HEAD_DIM:(hk + 1) * HEAD_DIM]) / l)
        if want_lse:
            lses.append(jnp.broadcast_to(m + jnp.log(l), (Ta, HEAD_DIM)))
    o_ref[...] = jnp.concatenate(outs, axis=1)
    if want_lse:
        lse_ref[...] = jnp.concatenate(lses, axis=1)


def band_attention(q, k, v, btab, Ta, n_prev, want_lse):
    R, A, QW = q.shape
    KW = k.shape[2]
    H = QW // HEAD_DIM
    shared_kv = KW == HEAD_DIM
    nA = A // Ta

    def prev_map(p):
        return lambda r, a: (r, jnp.maximum(a - n_prev + p, 0), 0)

    cur = lambda r, a: (r, a, 0)
    kv_specs = [pl.BlockSpec((None, Ta, KW), prev_map(p)) for p in range(n_prev)] + [pl.BlockSpec((None, Ta, KW), cur)]
    out_shape = [jax.ShapeDtypeStruct((R, A, QW), F32)]
    out_specs = [pl.BlockSpec((None, Ta, QW), cur)]
    if want_lse:
        out_shape.append(jax.ShapeDtypeStruct((R, A, QW), F32))
        out_specs.append(pl.BlockSpec((None, Ta, QW), cur))
    res = pl.pallas_call(
        functools.partial(_band_kernel, n_prev=n_prev, Ta=Ta, H=H, shared_kv=shared_kv, want_lse=want_lse),
        grid=(R, nA),
        in_specs=[pl.BlockSpec((None, Ta, QW), cur)] + kv_specs + kv_specs
                 + [pl.BlockSpec(btab.shape, lambda r, a: (0, 0, 0))],
        out_specs=out_specs,
        out_shape=out_shape,
        compiler_params=_cparams(("parallel", "arbitrary")),
    )(q, *([k] * (n_prev + 1)), *([v] * (n_prev + 1)), btab)
    return res if want_lse else res[0]


def _gelu_tanh(x):
    return 0.5 * x * (1.0 + jnp.tanh(math.sqrt(2.0 / math.pi) * (x + 0.044715 * (x * x * x))))


def _compress_kernel(ks_ref, vs_ref, pk_ref, pv_ref, w1k_ref, w2k_ref, w1v_ref, w2vt_ref, kc_ref, vct_ref):
    def hidden(src, pos, w1):
        x = src[...]
        n = x.shape[0]
        first = _dot((x + pos[0:1, :]).astype(BF16), w1[0])
        second = _dot((x + pos[1:2, :]).astype(BF16), w1[1])
        return _gelu_tanh(first + pltpu.roll(second, n - 1, axis=0)).astype(BF16)

    kc_ref[...] = _dot(hidden(ks_ref, pk_ref, w1k_ref), w2k_ref[...]).astype(BF16)
    vct_ref[...] = _dot_nt(w2vt_ref[...], hidden(vs_ref, pv_ref, w1v_ref)).astype(BF16)


def nsa_compress(k_chunks, v_chunks, pos_k, pos_v, w1k, w2k, w1v, w2v_t):
    n = k_chunks.shape[0]
    return pl.pallas_call(
        _compress_kernel,
        out_shape=[jax.ShapeDtypeStruct((n, HEAD_DIM), BF16), jax.ShapeDtypeStruct((HEAD_DIM, n), BF16)],
        compiler_params=_cparams(None),
    )(k_chunks, v_chunks, pos_k, pos_v, w1k, w2k, w1v, w2v_t)


def _cmp_topk_kernel(nq_ref, ks_ref, kc_ref, vct_ref, ovt_ref, eye_ref, ocmp_ref, qaug_ref, st_ref, *,
                     Tq, n_cmp, n_sel, nbg):
    q0 = pl.program_id(0) * Tq
    t = q0 + lax.broadcasted_iota(jnp.int32, (n_cmp, Tq), 1)
    cmp_end = NSA_CMP_STRIDE * lax.broadcasted_iota(jnp.int32, (n_cmp, Tq), 0) + (NSA_CMP_LEN - 1)
    c_mask = cmp_end <= t
    kc = kc_ref[...]
    vct = vct_ref[...]
    q = nq_ref[...]
    eye = eye_ref[...]
    ks_t = _dot_nt(eye, ks_ref[...])
    psum = jnp.zeros((n_cmp, Tq), F32)
    outs, q_ts, qn2, l_self = [], [], [], []
    for h in range(NSA_HEADS):
        qh = (q[:, h * HEAD_DIM:(h + 1) * HEAD_DIM] * (HEAD_DIM ** -0.5 * LOG2E)).astype(BF16)
        q_t = _dot_nt(eye, qh)
        q_ts.append(q_t.astype(BF16))
        qn2.append(jnp.sum(q_t * q_t, axis=0, keepdims=True))
        l_self.append(jnp.sum(q_t * ks_t, axis=0, keepdims=True))
        s = jnp.where(c_mask, _dot(kc, q_ts[h]), NEG_INF)
        p = jnp.exp2(s - jnp.max(s, axis=0, keepdims=True))
        p = jnp.where(c_mask, p * (1.0 / jnp.sum(p, axis=0, keepdims=True)), 0.0)
        psum = psum + p
        outs.append(_dot(vct, p.astype(BF16)).T)
    ocmp_ref[...] = jnp.concatenate(outs, axis=1)
    st_ref[0] = jnp.concatenate(qn2, axis=0)
    st_ref[1] = jnp.broadcast_to(jnp.sum(ks_t * ks_t, axis=0, keepdims=True), (NSA_HEADS, Tq))
    st_ref[2] = jnp.concatenate(l_self, axis=0)
    hi = psum.astype(BF16)
    lo = (psum - hi.astype(F32)).astype(BF16)
    imp = _dot(ovt_ref[...], hi) + _dot(ovt_ref[...], lo)
    t = q0 + lax.broadcasted_iota(jnp.int32, (n_sel, Tq), 1)
    j = lax.broadcasted_iota(jnp.int32, (n_sel, Tq), 0)
    start = j * NSA_SEL_BLOCK
    cur_blk = (start <= t) & (t < start + NSA_SEL_BLOCK)
    prev_blk = (start + NSA_SEL_BLOCK <= t) & (t < start + 2 * NSA_SEL_BLOCK)
    forced = (j == 0) | cur_blk | prev_blk
    score = jnp.where(forced, NSA_FORCE_SCORE, jnp.where(start <= t, imp, -NSA_FORCE_SCORE))
    jf = j.astype(F32)
    sel = jnp.zeros((n_sel, Tq), jnp.bool_)
    for _ in range(min(NSA_TOPK, n_sel)):
        mx = jnp.max(score, axis=0, keepdims=True)
        first = jnp.min(jnp.where(score == mx, jf, float(n_sel)), axis=0, keepdims=True)
        hit = jf == first
        sel = sel | hit
        score = jnp.where(hit, -3e38, score)
    sel_bias_t = jnp.where(sel, 0.0, SEL_NEG).astype(BF16)
    for h in range(NSA_HEADS):
        for g in range(n_sel // nbg):
            qaug_ref[g, h] = jnp.concatenate([q_ts[h], sel_bias_t[g * nbg:(g + 1) * nbg]], axis=0)


def nsa_cmp_topk(nq, k_sel, kc, vc_t, overlap_t, Tq, nbg):
    S = nq.shape[0]
    n_cmp = kc.shape[0]
    n_sel = S // NSA_SEL_BLOCK
    G = n_sel // nbg
    const = lambda a: pl.BlockSpec(a.shape, lambda i: (0,) * a.ndim)
    eye = jnp.eye(HEAD_DIM, dtype=BF16)
    return pl.pallas_call(
        functools.partial(_cmp_topk_kernel, Tq=Tq, n_cmp=n_cmp, n_sel=n_sel, nbg=nbg),
        grid=(S // Tq,),
        in_specs=[pl.BlockSpec((Tq, NSA_W), lambda i: (i, 0)),
                  pl.BlockSpec((Tq, HEAD_DIM), lambda i: (i, 0)),
                  const(kc), const(vc_t), const(overlap_t), const(eye)],
        out_specs=[pl.BlockSpec((Tq, NSA_W), lambda i: (i, 0)),
                   pl.BlockSpec((G, NSA_HEADS, HEAD_DIM + nbg, Tq), lambda i: (0, 0, 0, i)),
                   pl.BlockSpec((3, NSA_HEADS, Tq), lambda i: (0, 0, i))],
        out_shape=[jax.ShapeDtypeStruct((S, NSA_W), F32),
                   jax.ShapeDtypeStruct((G, NSA_HEADS, HEAD_DIM + nbg, S), BF16),
                   jax.ShapeDtypeStruct((3, NSA_HEADS, S), F32)],
        compiler_params=_cparams(("parallel",)),
    )(nq, k_sel, kc, vc_t, overlap_t, eye)


def _outproj_even_kernel(oa_ref, ob_ref, wa_ref, wb_ref, g_ref, h_ref, out_ref):
    m = _dot(oa_ref[...], wa_ref[...]) + _dot(ob_ref[...], wb_ref[...])
    out_ref[...] = h_ref[...] + _rms(m, g_ref[...])


def outproj_even(oa, ob, wa, wb, g, h, tm):
    S, D = h.shape
    rows = lambda w: pl.BlockSpec((tm, w), lambda i: (i, 0))
    full = lambda a: pl.BlockSpec(a.shape, lambda i: (0,) * a.ndim)
    return pl.pallas_call(
        _outproj_even_kernel,
        grid=(S // tm,),
        in_specs=[rows(oa.shape[1]), rows(ob.shape[1]), full(wa), full(wb), full(g), rows(D)],
        out_specs=rows(D),
        out_shape=jax.ShapeDtypeStruct((S, D), F32),
        compiler_params=_cparams(("parallel",)),
    )(oa, ob, wa, wb, g, h)


def _outproj_odd_kernel(o0_ref, o1_ref, o2_ref, l0_ref, l1_ref, l2_ref, oc_ref, os_ref, ow_ref, gt_ref,
                        wd_ref, wn_ref, g_ref, h_ref, out_ref):
    l0, l1, l2 = l0_ref[...], l1_ref[...], l2_ref[...]
    mx = jnp.maximum(jnp.maximum(l0, l1), l2)
    e0, e1, e2 = jnp.exp(l0 - mx), jnp.exp(l1 - mx), jnp.exp(l2 - mx)
    o_dil = (e0 * o0_ref[...] + e1 * o1_ref[...] + e2 * o2_ref[...]) / (e0 + e1 + e2)
    gt = jax.nn.sigmoid(gt_ref[...])
    o_nsa = (gt[:, 0:NSA_W] * oc_ref[...] + gt[:, NSA_W:2 * NSA_W] * os_ref[...].astype(F32)
             + gt[:, 2 * NSA_W:3 * NSA_W] * ow_ref[...])
    m = _dot(o_dil.astype(BF16), wd_ref[...]) + _dot(o_nsa.astype(BF16), wn_ref[...])
    out_ref[...] = h_ref[...] + _rms(m, g_ref[...])


def outproj_odd(o_dil, lse_dil, o_cmp, o_sel, o_win, gates, wd, wn, g, h, tm):
    S, D = h.shape
    rows = lambda w: pl.BlockSpec((tm, w), lambda i: (i, 0))
    full = lambda a: pl.BlockSpec(a.shape, lambda i: (0,) * a.ndim)
    return pl.pallas_call(
        _outproj_odd_kernel,
        grid=(S // tm,),
        in_specs=[rows(DIL_GW)] * 6 + [rows(NSA_W)] * 3 + [rows(3 * NSA_W), full(wd), full(wn), full(g), rows(D)],
        out_specs=rows(D),
        out_shape=jax.ShapeDtypeStruct((S, D), F32),
        compiler_params=_cparams(("parallel",)),
    )(*o_dil, *lse_dil, o_cmp, o_sel, o_win, gates, wd, wn, g, h)


def _mlp_ple_kernel(h_ref, g1_ref, wu_ref, wd_ref, g2_ref, g3_ref, wg_ref, p_ref, wp_ref, out_ref,
                    hn_ref, acc_ref):
    j = pl.program_id(1)

    @pl.when(j == 0)
    def _():
        hn_ref[...] = _rms(h_ref[...], g1_ref[...]).astype(BF16)
        acc_ref[...] = jnp.zeros(acc_ref.shape, F32)

    u = jnp.maximum(_dot(hn_ref[...], wu_ref[...]), 0.0)
    acc_ref[...] += _dot((u * u).astype(BF16), wd_ref[...])

    @pl.when(j == pl.num_programs(1) - 1)
    def _():
        h2 = h_ref[...] + _rms(acc_ref[...], g2_ref[...])
        gate = jax.nn.sigmoid(_dot(_rms(h2, g3_ref[...]).astype(BF16), wg_ref[...]))
        out_ref[...] = h2 + gate * _dot(p_ref[...].astype(BF16), wp_ref[...])


def mlp_ple(h, g1, wu, wd, g2, g3, wg, p, wp, tm, tf):
    S, D = h.shape
    FF = wu.shape[1]
    PD = p.shape[1]
    const = lambda a: pl.BlockSpec(a.shape, lambda i, j: (0,) * a.ndim)
    return pl.pallas_call(
        _mlp_ple_kernel,
        grid=(S // tm, FF // tf),
        in_specs=[pl.BlockSpec((tm, D), lambda i, j: (i, 0)), const(g1),
                  pl.BlockSpec((D, tf), lambda i, j: (0, j)),
                  pl.BlockSpec((tf, D), lambda i, j: (j, 0)),
                  const(g2), const(g3), const(wg),
                  pl.BlockSpec((tm, PD), lambda i, j: (i, 0)), const(wp)],
        out_specs=pl.BlockSpec((tm, D), lambda i, j: (i, 0)),
        out_shape=jax.ShapeDtypeStruct((S, D), F32),
        scratch_shapes=[pltpu.VMEM((tm, D), BF16), pltpu.VMEM((tm, D), F32)],
        compiler_params=_cparams(("parallel", "arbitrary")),
    )(h, g1, wu, wd, g2, g3, wg, p, wp)


def _t5_bucket_of(dist):
    n = jnp.maximum(dist, 0)
    max_exact = T5_BUCKETS // 2
    ratio = jnp.log(jnp.maximum(n, 1).astype(F32) / max_exact) / math.log(T5_MAX_DIST / max_exact)
    large = jnp.minimum(max_exact + (ratio * (T5_BUCKETS - max_exact)).astype(jnp.int32), T5_BUCKETS - 1)
    return jnp.where(n < max_exact, n, large)


def _bias_of_dist(bias, dist):
    bucket = _t5_bucket_of(dist)[None]
    out = jnp.zeros((bias.shape[1],) + dist.shape, F32)
    for b in range(T5_BUCKETS):
        out = jnp.where(bucket == b, bias[b].reshape((-1,) + (1,) * dist.ndim), out)
    return out


def _causal_tables(T):
    k = jnp.arange(T)[:, None]
    q = jnp.arange(T)[None, :]
    diag = jnp.where(q >= k, 0.0, NEG_INF).astype(F32)
    return jnp.stack([diag, jnp.zeros((T, T), F32)])[:, None]


def _t5_delta_tables(bias, T):
    n_delta = -(-(T5_MAX_DIST - 1) // T) + 2
    k = jnp.arange(T)[None, :, None]
    q = jnp.arange(T)[None, None, :]
    dist = jnp.arange(n_delta)[:, None, None] * T + q - k
    val = (_bias_of_dist(bias, dist) - bias[T5_BUCKETS - 1].reshape(-1, 1, 1, 1)) * LOG2E
    return jnp.transpose(jnp.where(dist[None] >= 0, val, NEG_INF), (1, 0, 2, 3))


def _band_table(bias, Ta, n_prev, max_rel, stride, inclusive):
    P = n_prev * Ta
    i = jnp.arange(Ta)[:, None]
    c = jnp.arange(P + Ta)[None, :]
    rel = i + P - c
    ok = (rel >= 0) & ((rel <= max_rel) if inclusive else (rel < max_rel))
    return jnp.where(ok[None], _bias_of_dist(bias, rel * stride), NEG_INF)


def _tile(S, pref):
    t = min(pref, S)
    assert S % t == 0
    return t


def even_mixer_core(y, cos, sin, w_uq, q_norm, w_ukv, kv_norm, forget_bias, T):
    S = y.shape[0]
    scale = MLA_QK_DIM ** -0.5 * LOG2E
    half = MLA_ROPE_DIM // 2
    cos_t, sin_t = jnp.transpose(cos), jnp.transpose(sin)
    ones = jnp.ones((MLA_NOPE_DIM, S), F32)
    cq = jnp.concatenate([ones, cos_t, cos_t], axis=0) * scale
    sq = jnp.concatenate([0.0 * ones, -sin_t, sin_t], axis=0) * scale
    ck = jnp.concatenate([cos, cos], axis=1)
    sk = jnp.concatenate([-sin, sin], axis=1)
    wq = jnp.transpose(w_uq.reshape(MLA_Q_RANK, MLA_HEADS, MLA_QK_DIM), (1, 2, 0))
    swap = np.concatenate([np.arange(MLA_NOPE_DIM), MLA_NOPE_DIM + half + np.arange(half),
                           MLA_NOPE_DIM + np.arange(half)])
    wqs = wq[:, swap, :]
    wkv = jnp.transpose(w_ukv.reshape(MLA_KV_RANK, MLA_HEADS, MLA_NOPE_DIM + MLA_V_DIM), (1, 0, 2))
    wk = jnp.pad(wkv[:, :, :MLA_NOPE_DIM], ((0, 0), (0, 0), (0, MLA_ROPE_DIM)))
    wv = jnp.transpose(wkv[:, :, MLA_NOPE_DIM:], (0, 2, 1))
    e96 = jnp.pad(jnp.eye(MLA_ROPE_DIM, dtype=F32), ((0, 0), (MLA_NOPE_DIM, 0)))
    qm, km, vm, fq, fk, fv, logf, st = even_prep(
        y, wq.astype(BF16), wqs.astype(BF16), wk.astype(BF16), e96.astype(BF16), wv.astype(BF16),
        jnp.eye(MLA_QK_DIM, dtype=BF16), q_norm[None, :], kv_norm[None, :], forget_bias[None, :],
        cq, sq, ck, sk, _tile(S, 512))
    neg_f = cumsum_split(jnp.transpose(logf))
    hps = 4
    causal = _causal_tables(T)
    r_mla, fast_mla = _reference_rows(st[0], st[1], st[2], 0.0, 0.0, 0.0, T, hps)
    o_mla = flash_causal(jnp.concatenate([qm, r_mla], axis=1)[None],
                         jnp.concatenate([km, _stat_cols(MLA_HEADS, S)], axis=2), vm, causal, fast_mla, T, hps)
    r_fox, fast_fox = _reference_rows(st[3], st[4], st[5], jnp.sum(neg_f.astype(F32), axis=0), 0.0, 0.0, T, hps)
    fk_aug = jnp.concatenate([fk, jnp.transpose(neg_f, (1, 2, 0)),
                              jnp.zeros((FOX_HEADS, S, FOX_PAD - FOX_SPLIT), BF16), _stat_cols(FOX_HEADS, S)], axis=2)
    o_fox = flash_causal(jnp.concatenate([fq, r_fox], axis=1)[None], fk_aug, fv, causal, fast_fox, T, hps)
    return o_mla, o_fox


def odd_mixer_core(y, t5_bias, pos_k, pos_v, w1k, w2k, w1v, w2v, T):
    S = y.shape[0]
    G = len(DIL_GROUPS)
    Ta = 128
    o_dil, lse_dil = [], []
    for g, (w, d) in enumerate(DIL_GROUPS):
        def deint(a):
            return jnp.transpose(a.reshape(S // d, d, DIL_GW), (1, 0, 2))
        q, k, v = (deint(y[:, (b * G + g) * DIL_GW:(b * G + g + 1) * DIL_GW]) for b in range(3))
        btab = _band_table(t5_bias[:, g * DIL_HEADS:(g + 1) * DIL_HEADS], Ta, 1, w // d, d, True)
        o, lse = band_attention(q, k, v, btab, _tile(S // d, Ta), 1, True)
        o_dil.append(jnp.transpose(o, (1, 0, 2)).reshape(S, DIL_GW))
        lse_dil.append(jnp.transpose(lse, (1, 0, 2)).reshape(S, DIL_GW))
    base = 3 * G * DIL_GW
    nq = y[:, base:base + NSA_W]
    k_cmp, v_cmp, k_sel, v_sel, k_win, v_win = (
        y[:, base + NSA_W + i * HEAD_DIM:base + NSA_W + (i + 1) * HEAD_DIM] for i in range(6))
    gates = y[:, base + NSA_W + 6 * HEAD_DIM:base + NSA_W + 6 * HEAD_DIM + 3 * NSA_W]
    bias_nsa = t5_bias[:, G * DIL_HEADS:]
    n_chunk = S // NSA_CMP_STRIDE
    cw = NSA_CMP_STRIDE * HEAD_DIM
    kc, vc = nsa_compress(k_cmp.reshape(n_chunk, cw), v_cmp.reshape(n_chunk, cw),
                          pos_k.reshape(2, cw), pos_v.reshape(2, cw),
                          w1k.reshape(2, cw, -1).astype(BF16), w2k.astype(BF16),
                          w1v.reshape(2, cw, -1).astype(BF16), jnp.transpose(w2v).astype(BF16))
    n_sel = S // NSA_SEL_BLOCK
    ci = np.arange(n_chunk)[None, :] * NSA_CMP_STRIDE
    sj = np.arange(n_sel)[:, None] * NSA_SEL_BLOCK
    overlap_t = jnp.asarray(((ci < sj + NSA_SEL_BLOCK) & (ci + NSA_CMP_LEN > sj)).astype(np.float32), BF16)
    nbg = min(n_sel, NSA_SEL_GROUP_BLOCKS)
    G = n_sel // nbg
    k_sel = k_sel.astype(BF16)
    o_cmp, qaug, st = nsa_cmp_topk(nq, k_sel, kc, vc, overlap_t, _tile(S, 256), nbg)
    onehot = jnp.asarray(((np.arange(S)[:, None] // NSA_SEL_BLOCK) % nbg == np.arange(nbg)[None, :])
                         .astype(np.float32), BF16)
    kaug = jnp.concatenate([k_sel, onehot, _stat_cols(1, S)[0]], axis=1)
    delta = (bias_nsa - bias_nsa[T5_BUCKETS - 1]) * LOG2E
    r_sel, fast_sel = _reference_rows(st[0], st[1, :1], st[2], 0.0, jnp.max(delta, axis=0)[:, None],
                                      delta[0][:, None], T, NSA_HEADS)
    qaug = jnp.concatenate([qaug, jnp.broadcast_to(r_sel, (G,) + r_sel.shape)], axis=2)
    ones_row = (jnp.arange(STAT_ROWS) < 1).astype(BF16)[:, None] * jnp.ones((1, S), BF16)
    v_sel_t = jnp.concatenate([jnp.transpose(v_sel).astype(BF16), ones_row], axis=0)
    o_sel = flash_causal(qaug, kaug[None], v_sel_t[None], _t5_delta_tables(bias_nsa, T), fast_sel, T,
                         hps=NSA_HEADS, key_group_tokens=nbg * NSA_SEL_BLOCK)
    Tw = _tile(S, 256)
    n_prev = NSA_WINDOW // Tw
    wtab = _band_table(bias_nsa, Tw, n_prev, NSA_WINDOW, 1, False)
    o_win = band_attention(nq[None], k_win[None], v_win[None], wtab, Tw, n_prev, False)[0]
    return o_dil, lse_dil, o_cmp, o_sel, o_win, gates


def _pad_cols(w, n):
    return jnp.pad(w, ((0, 0), (0, n - w.shape[1])))


def _even_w_in(w):
    cq, ckv, kr, fq, fk, fv, fl = jnp.split(w, np.cumsum([256, 128, 32, 512, 512, 512])[:], axis=1)
    half = MLA_ROPE_DIM // 2
    kr_sw = jnp.concatenate([kr[:, half:], kr[:, :half]], axis=1)
    head = _pad_cols(jnp.concatenate([cq, ckv, kr, kr_sw, fl], axis=1), 512)
    return jnp.concatenate([head, fq, fk, fv], axis=1).astype(BF16)


def _odd_w_in(w):
    main = w[:, :3 * 3 * DIL_GW + NSA_W + 6 * HEAD_DIM]
    gl = w[:, 3 * 3 * DIL_GW + NSA_W + 6 * HEAD_DIM:]
    gl = jnp.transpose(gl.reshape(-1, NSA_HEADS, 3), (0, 2, 1))
    gl = jnp.repeat(gl[..., None], HEAD_DIM, axis=-1).reshape(w.shape[0], 3 * NSA_W)
    return _pad_cols(jnp.concatenate([main, gl], axis=1), ODD_IN_PAD).astype(BF16)


def _trunk(x, p, positions, t5_bias, ev_w_in, ev_q_norm, ev_w_uq, ev_kv_norm, ev_w_ukv, ev_forget_bias,
           ev_w_out, od_w_in, od_cmp_pos_k, od_cmp_pos_v, od_cmp_w1_k, od_cmp_w2_k, od_cmp_w1_v, od_cmp_w2_v,
           od_w_out, norm_mix_pre, norm_mix_post, norm_mlp_pre, norm_mlp_post, w_mlp_up, w_mlp_down,
           ple_norm, w_ple_gate, w_ple_proj):
    S, D = x.shape
    depth = p.shape[0]
    T = _tile(S, 512)
    tm = _tile(S, 512)
    inv_freq = ROPE_THETA ** (-jnp.arange(0, MLA_ROPE_DIM, 2, dtype=F32) / MLA_ROPE_DIM)
    angles = positions.astype(F32)[:, None] * inv_freq
    cos, sin = jnp.cos(angles), jnp.sin(angles)
    h = x
    for i in range(depth):
        j = i // 2
        if i % 2 == 0:
            y = norm_matmul(h, norm_mix_pre[i][None], _even_w_in(ev_w_in[j]), tm, 512)
            o_mla, o_fox = even_mixer_core(y, cos, sin, ev_w_uq[j], ev_q_norm[j], ev_w_ukv[j], ev_kv_norm[j],
                                           ev_forget_bias[j], T)
            wo = ev_w_out[j].astype(BF16)
            na = MLA_HEADS * MLA_V_DIM
            h = outproj_even(o_mla, o_fox, wo[:na], wo[na:], norm_mix_post[i][None], h, tm)
        else:
            y = norm_matmul(h, norm_mix_pre[i][None], _odd_w_in(od_w_in[j]), tm, 768)
            o_dil, lse_dil, o_cmp, o_sel, o_win, gates = odd_mixer_core(
                y, t5_bias, od_cmp_pos_k[j], od_cmp_pos_v[j], od_cmp_w1_k[j], od_cmp_w2_k[j],
                od_cmp_w1_v[j], od_cmp_w2_v[j], T)
            wo = od_w_out[j].astype(BF16)
            h = outproj_odd(o_dil, lse_dil, o_cmp, o_sel, o_win, gates, wo[:DIL_GW], wo[DIL_GW:],
                            norm_mix_post[i][None], h, tm)
        h = mlp_ple(h, norm_mlp_pre[i][None], w_mlp_up[i].astype(BF16), w_mlp_down[i].astype(BF16),
                    norm_mlp_post[i][None], ple_norm[i][None], w_ple_gate[i].astype(BF16), p[i],
                    w_ple_proj[i].astype(BF16), tm, 1024)
    return h


def kernel(x, p, positions, t5_bias, ev_w_in, ev_q_norm, ev_w_uq, ev_kv_norm, ev_w_ukv, ev_forget_bias, ev_w_out, od_w_in, od_cmp_pos_k, od_cmp_pos_v, od_cmp_w1_k, od_cmp_w2_k, od_cmp_w1_v, od_cmp_w2_v, od_w_out, norm_mix_pre, norm_mix_post, norm_mlp_pre, norm_mlp_post, w_mlp_up, w_mlp_down, ple_norm, w_ple_gate, w_ple_proj):
    params = (t5_bias, ev_w_in, ev_q_norm, ev_w_uq, ev_kv_norm, ev_w_ukv, ev_forget_bias, ev_w_out, od_w_in,
              od_cmp_pos_k, od_cmp_pos_v, od_cmp_w1_k, od_cmp_w2_k, od_cmp_w1_v, od_cmp_w2_v, od_w_out,
              norm_mix_pre, norm_mix_post, norm_mlp_pre, norm_mlp_post, w_mlp_up, w_mlp_down, ple_norm,
              w_ple_gate, w_ple_proj)
    outs = [_trunk(x[b], p[:, b], positions[b], *params) for b in range(x.shape[0])]
    return jnp.stack(outs).astype(x.dtype)
```

```python
import functools
import math

import numpy as np
import jax
import jax.numpy as jnp
from jax import lax
from jax.experimental import pallas as pl
from jax.experimental.pallas import tpu as pltpu

F32 = jnp.float32
BF16 = jnp.bfloat16

HEAD_DIM = 64
RMS_EPS = 1e-6
NEG_INF = -1e30
SEL_NEG = -(2.0 ** 99)

MLA_HEADS = 8
MLA_NOPE_DIM = 64
MLA_ROPE_DIM = 32
MLA_V_DIM = 64
MLA_Q_RANK = 256
MLA_KV_RANK = 128
MLA_QK_DIM = MLA_NOPE_DIM + MLA_ROPE_DIM
ROPE_THETA = 10000.0
FOX_HEADS = 8
FOX_SPLIT = 3
FOX_PAD = 16
LOG2E = math.log2(math.e)
STAT_ROWS = 16
FAST_OVER = 64.0
FAST_MARGIN = 150.0
P_CHUNK = 32
DIL_GROUPS = ((128, 1), (512, 4), (2048, 16))
DIL_HEADS = 4
DIL_GW = DIL_HEADS * HEAD_DIM
NSA_HEADS = 4
NSA_W = NSA_HEADS * HEAD_DIM
NSA_CMP_LEN = 32
NSA_CMP_STRIDE = 16
NSA_SEL_BLOCK = 64
NSA_TOPK = 16
NSA_WINDOW = 512
NSA_FORCE_SCORE = 1e9
NSA_SEL_GROUP_BLOCKS = 128
T5_BUCKETS = 32
T5_MAX_DIST = 2048

EVEN_IN_PAD = 2048
ODD_IN_PAD = 3840

VMEM_LIMIT_BYTES = 56 * 1024 * 1024


def _cparams(sem):
    return pltpu.CompilerParams(dimension_semantics=sem, vmem_limit_bytes=VMEM_LIMIT_BYTES)


def _rms(x, g):
    return x * lax.rsqrt(jnp.mean(x * x, axis=-1, keepdims=True) + RMS_EPS) * g


def _dot(a, b):
    return jnp.dot(a, b, preferred_element_type=F32)


def _dot_nt(a, b):
    return lax.dot_general(a, b, (((1,), (1,)), ((), ())), preferred_element_type=F32)


def _norm_matmul_kernel(h_ref, g_ref, w_ref, o_ref, hn_ref):
    @pl.when(pl.program_id(1) == 0)
    def _():
        hn_ref[...] = _rms(h_ref[...], g_ref[...]).astype(BF16)

    o_ref[...] = _dot(hn_ref[...], w_ref[...])


def norm_matmul(h, g, w, tm, tn):
    S, D = h.shape
    N = w.shape[1]
    return pl.pallas_call(
        _norm_matmul_kernel,
        grid=(S // tm, N // tn),
        in_specs=[pl.BlockSpec((tm, D), lambda i, j: (i, 0)),
                  pl.BlockSpec((1, D), lambda i, j: (0, 0)),
                  pl.BlockSpec((D, tn), lambda i, j: (0, j))],
        out_specs=pl.BlockSpec((tm, tn), lambda i, j: (i, j)),
        out_shape=jax.ShapeDtypeStruct((S, N), F32),
        scratch_shapes=[pltpu.VMEM((tm, D), BF16)],
        compiler_params=_cparams(("parallel", "arbitrary")),
    )(h, g, w)


def _even_prep_kernel(h_ref, g_ref, win_ref, wq_ref, wqs_ref, wk_ref, e_ref, wv_ref, eye_ref, qn_ref, kvn_ref,
                      fb_ref, cq_ref, sq_ref, ck_ref, sk_ref,
                      qm_ref, km_ref, vm_ref, fq_ref, fk_ref, fv_ref, lf_ref, st_ref):
    y = _dot(_rms(h_ref[...], g_ref[...]).astype(BF16), win_ref[...])
    tm = y.shape[0]
    c_q = y[:, 0:MLA_Q_RANK]
    c_kv = y[:, MLA_Q_RANK:MLA_Q_RANK + MLA_KV_RANK]
    o = MLA_Q_RANK + MLA_KV_RANK
    k_r = y[:, o:o + MLA_ROPE_DIM]
    k_rs = y[:, o + MLA_ROPE_DIM:o + 2 * MLA_ROPE_DIM]
    f_logit = y[:, o + 2 * MLA_ROPE_DIM:o + 2 * MLA_ROPE_DIM + FOX_HEADS]
    nq = _rms(c_q, qn_ref[...]).astype(BF16)
    nkv = _rms(c_kv, kvn_ref[...]).astype(BF16)
    k_rot = (k_r * ck_ref[...] + k_rs * sk_ref[...]).astype(BF16)
    k_rot96 = _dot(k_rot, e_ref[...])
    cq = cq_ref[...]
    sq = sq_ref[...]
    eye = eye_ref[...]
    row = lax.broadcasted_iota(jnp.int32, (STAT_ROWS, tm), 0)
    ones_row = jnp.where(row < 1, 1.0, 0.0).astype(BF16)
    col = lax.broadcasted_iota(jnp.int32, (tm, STAT_ROWS), 1)
    stat_cols = jnp.where(col < 3, 1.0, 0.0).astype(BF16)
    stats = [[] for _ in range(6)]

    def record(slot, q_t, k):
        q_t = q_t.astype(F32)
        k_t = _dot_nt(eye[:k.shape[1], :k.shape[1]], k)
        stats[slot].append(jnp.sum(q_t * q_t, axis=0, keepdims=True))
        stats[slot + 1].append(jnp.sum(k_t * k_t, axis=0, keepdims=True))
        stats[slot + 2].append(jnp.sum(q_t * k_t, axis=0, keepdims=True))

    for h in range(MLA_HEADS):
        q = (_dot_nt(wq_ref[h], nq) * cq + _dot_nt(wqs_ref[h], nq) * sq).astype(BF16)
        k = (_dot(nkv, wk_ref[h]) + k_rot96).astype(BF16)
        qm_ref[h] = q
        km_ref[h] = jnp.concatenate([k, stat_cols], axis=1)
        vm_ref[h] = jnp.concatenate([_dot_nt(wv_ref[h], nkv).astype(BF16), ones_row], axis=0)
        record(0, q, k)
    base = 512
    fw = FOX_HEADS * HEAD_DIM
    eye64 = eye[:HEAD_DIM, :HEAD_DIM]
    ones_rows = jnp.where(row < FOX_SPLIT, 1.0, 0.0).astype(BF16)
    for h in range(FOX_HEADS):
        lo = base + h * HEAD_DIM
        fq = _dot_nt(eye64, (y[:, lo:lo + HEAD_DIM] * (HEAD_DIM ** -0.5 * LOG2E)).astype(BF16)).astype(BF16)
        fk = y[:, lo + fw:lo + fw + HEAD_DIM].astype(BF16)
        fq_ref[h] = jnp.concatenate([fq, ones_rows], axis=0)
        fk_ref[h] = fk
        fv = y[:, lo + 2 * fw:lo + 2 * fw + HEAD_DIM].astype(BF16)
        fv_ref[h] = jnp.concatenate([_dot_nt(eye64, fv).astype(BF16), ones_row], axis=0)
        record(3, fq, fk)
    for i in range(6):
        st_ref[i] = jnp.concatenate(stats[i], axis=0)
    z = f_logit + fb_ref[...]
    lf_ref[...] = jnp.minimum(z, 0.0) - jnp.log1p(jnp.exp(-jnp.abs(z)))


def even_prep(h, g, w_in, wq, wqs, wk, e96, wv, eye, qn, kvn, fb, cq, sq, ck, sk, tm):
    S, D = h.shape
    full = lambda a: pl.BlockSpec(a.shape, lambda i: (0,) * a.ndim)
    rows = lambda w: pl.BlockSpec((tm, w), lambda i: (i, 0))
    cols = lambda w: pl.BlockSpec((w, tm), lambda i: (0, i))
    heads = lambda n, w: pl.BlockSpec((n, tm, w), lambda i: (0, i, 0))
    heads_t = lambda n, w: pl.BlockSpec((n, w, tm), lambda i: (0, 0, i))
    out_shape = [jax.ShapeDtypeStruct((MLA_HEADS, MLA_QK_DIM, S), BF16),
                 jax.ShapeDtypeStruct((MLA_HEADS, S, MLA_QK_DIM + STAT_ROWS), BF16),
                 jax.ShapeDtypeStruct((MLA_HEADS, MLA_V_DIM + STAT_ROWS, S), BF16),
                 jax.ShapeDtypeStruct((FOX_HEADS, HEAD_DIM + FOX_PAD, S), BF16),
                 jax.ShapeDtypeStruct((FOX_HEADS, S, HEAD_DIM), BF16),
                 jax.ShapeDtypeStruct((FOX_HEADS, HEAD_DIM + STAT_ROWS, S), BF16),
                 jax.ShapeDtypeStruct((S, FOX_HEADS), F32),
                 jax.ShapeDtypeStruct((6, MLA_HEADS, S), F32)]
    out_specs = [heads_t(MLA_HEADS, MLA_QK_DIM), heads(MLA_HEADS, MLA_QK_DIM + STAT_ROWS),
                 heads_t(MLA_HEADS, MLA_V_DIM + STAT_ROWS),
                 heads_t(FOX_HEADS, HEAD_DIM + FOX_PAD), heads(FOX_HEADS, HEAD_DIM),
                 heads_t(FOX_HEADS, HEAD_DIM + STAT_ROWS),
                 rows(FOX_HEADS), heads_t(6, MLA_HEADS)]
    return pl.pallas_call(
        _even_prep_kernel,
        grid=(S // tm,),
        in_specs=[rows(D), full(g), full(w_in), full(wq), full(wqs), full(wk), full(e96), full(wv), full(eye),
                  full(qn), full(kvn), full(fb), cols(MLA_QK_DIM), cols(MLA_QK_DIM), rows(MLA_ROPE_DIM),
                  rows(MLA_ROPE_DIM)],
        out_specs=out_specs,
        out_shape=out_shape,
        compiler_params=_cparams(("parallel",)),
    )(h, g, w_in, wq, wqs, wk, e96, wv, eye, qn, kvn, fb, cq, sq, ck, sk)


def _cumsum_split_kernel(x_ref, o_ref):
    x = x_ref[...]
    n = x.shape[1]
    lane = lax.broadcasted_iota(jnp.int32, x.shape, 1)
    shift = 1
    while shift < n:
        x = x + jnp.where(lane >= shift, pltpu.roll(x, shift, axis=1), 0.0)
        shift *= 2
    r = -x * LOG2E
    for i in range(FOX_SPLIT):
        part = r.astype(BF16)
        o_ref[i] = part
        r = r - part.astype(F32)


def cumsum_split(x):
    return pl.pallas_call(
        _cumsum_split_kernel,
        out_shape=jax.ShapeDtypeStruct((FOX_SPLIT,) + x.shape, BF16),
        compiler_params=_cparams(None),
    )(x)


def _flash_kernel(qt_ref, kt_ref, fast_ref, q_ref, r_ref, k_ref, v_ref, bt_ref, o_ref, m_scr, acc_scr, p_scr, *,
                  hps, shared_kv, n_delta, nt):
    step = pl.program_id(1)
    qi = qt_ref[step]
    ki = kt_ref[step]
    fast = fast_ref[pl.program_id(0) * nt + qi] != 0
    near = qi - ki < n_delta - 1

    @pl.when(ki == 0)
    def _():
        m_scr[...] = jnp.full(m_scr.shape, -3e38, F32)
        acc_scr[...] = jnp.zeros(acc_scr.shape, F32)

    hb = bt_ref.shape[1]
    Tk = k_ref.shape[1]
    Dv = v_ref.shape[1] - STAT_ROWS

    def logits(h, with_table):
        q = jnp.concatenate([q_ref[0, h], r_ref[h]], axis=0)
        s = _dot(k_ref[0 if shared_kv else h], q)
        return s + bt_ref[0, h if hb > 1 else 0] if with_table else s

    def store_p(h, s, m):
        for c in range(Tk // P_CHUNK):
            rows = slice(c * P_CHUNK, (c + 1) * P_CHUNK)
            x = s[rows] if m is None else s[rows] - m
            p_scr[h, rows, :] = jnp.exp2(x).astype(BF16)

    def fast_update(with_table):
        acc_prev = [acc_scr[h] for h in range(hps)]
        for h in range(hps):
            store_p(h, logits(h, with_table), None)
        for h in range(hps):
            acc_scr[h] = acc_prev[h] + _dot(v_ref[0 if shared_kv else h], p_scr[h])

    def safe_update(with_table):
        m_prev = [m_scr[h] for h in range(hps)]
        acc_prev = [acc_scr[h] for h in range(hps)]
        ss = [logits(h, with_table) for h in range(hps)]
        ms = [jnp.maximum(m_prev[h], jnp.max(ss[h], axis=0, keepdims=True)) for h in range(hps)]
        for h in range(hps):
            store_p(h, ss[h], ms[h])
        for h in range(hps):
            alpha = jnp.exp2(m_prev[h] - ms[h])
            acc_scr[h] = alpha * acc_prev[h] + _dot(v_ref[0 if shared_kv else h], p_scr[h])
            m_scr[h] = ms[h]

    for take_fast, update in ((True, fast_update), (False, safe_update)):
        for with_table in (True, False):
            cond = jnp.logical_and(fast == take_fast, near == with_table)
            pl.when(cond)(functools.partial(update, with_table))

    @pl.when(ki == qi)
    def _():
        outs = [(acc_scr[h, :Dv, :] / acc_scr[h, Dv:Dv + 1, :]).T for h in range(hps)]
        o_ref[...] = jnp.concatenate(outs, axis=1).astype(o_ref.dtype)


def flash_causal(qT, r_rows, k, vT, btab, fast, T, hps, key_group_tokens=None, out_dtype=BF16):
    G, H, Dq, S = qT.shape
    Dk = Dq + STAT_ROWS
    Hk, Dv, _ = vT.shape
    Dv -= STAT_ROWS
    shared_kv = Hk == 1
    n_delta, Hb = btab.shape[0], btab.shape[1]
    nt = S // T
    qt = np.array([i for i in range(nt) for _ in range(i + 1)], np.int32)
    kt = np.array([j for i in range(nt) for j in range(i + 1)], np.int32)
    kv_h = 1 if shared_kv else hps
    tiles_per_group = (key_group_tokens // T) if G > 1 else 1

    def q_map(g, s, qt, kt, fast):
        return (kt[s] // tiles_per_group if G > 1 else 0, g, 0, qt[s])

    def bt_map(g, s, qt, kt, fast):
        return (jnp.minimum(qt[s] - kt[s], n_delta - 1), g if Hb > hps else 0, 0, 0)

    in_specs = [pl.BlockSpec((1, hps, Dq, T), q_map),
                pl.BlockSpec((hps, STAT_ROWS, T), lambda g, s, qt, kt, fast: (g, 0, qt[s])),
                pl.BlockSpec((kv_h, T, Dk), lambda g, s, qt, kt, fast: (0 if shared_kv else g, kt[s], 0)),
                pl.BlockSpec((kv_h, Dv + STAT_ROWS, T), lambda g, s, qt, kt, fast: (0 if shared_kv else g, 0, kt[s])),
                pl.BlockSpec((1, min(Hb, hps), T, T), bt_map)]
    grid_spec = pltpu.PrefetchScalarGridSpec(
        num_scalar_prefetch=3,
        grid=(H // hps, len(qt)),
        in_specs=in_specs,
        out_specs=pl.BlockSpec((T, hps * Dv), lambda g, s, qt, kt, fast: (qt[s], g)),
        scratch_shapes=[pltpu.VMEM((hps, 1, T), F32), pltpu.VMEM((hps, Dv + STAT_ROWS, T), F32),
                        pltpu.VMEM((hps, T, T), BF16)],
    )
    return pl.pallas_call(
        functools.partial(_flash_kernel, hps=hps, shared_kv=shared_kv, n_delta=n_delta, nt=nt),
        grid_spec=grid_spec,
        out_shape=jax.ShapeDtypeStruct((S, H * Dv), out_dtype),
        compiler_params=_cparams(("parallel", "arbitrary")),
    )(jnp.asarray(qt), jnp.asarray(kt), fast, qT, r_rows, k, vT, btab)


def _reference_rows(qn2, kn2, l_self, extra, bias_max, bias_self, T, hps):
    H, S = qn2.shape
    kmax = jnp.sqrt(jnp.max(kn2, axis=1, keepdims=True))
    bound = jnp.sqrt(qn2) * kmax + bias_max + extra
    ok = bound - (l_self + bias_self + extra) <= FAST_MARGIN
    fast = jnp.all(ok.reshape(H // hps, hps, S // T, T), axis=(1, 3)).astype(jnp.int32).reshape(-1)
    r = FAST_OVER - bound
    parts = []
    for _ in range(3):
        part = r.astype(BF16)
        parts.append(part)
        r = r - part.astype(F32)
    rows = jnp.stack(parts + [jnp.zeros_like(parts[0])] * (STAT_ROWS - 3), axis=1)
    return rows, fast


def _stat_cols(Hk, S):
    return jnp.broadcast_to((jnp.arange(STAT_ROWS) < 3).astype(BF16), (Hk, S, STAT_ROWS))


def _band_kernel(*refs, n_prev, Ta, H, shared_kv, want_lse):
    q_ref = refs[0]
    k_refs = refs[1:2 + n_prev]
    v_refs = refs[2 + n_prev:3 + 2 * n_prev]
    bt_ref = refs[3 + 2 * n_prev]
    o_ref = refs[4 + 2 * n_prev]
    lse_ref = refs[5 + 2 * n_prev] if want_lse else None
    ai = pl.program_id(1)
    k = jnp.concatenate([r[...] for r in k_refs], axis=0).astype(BF16)
    v = jnp.concatenate([r[...] for r in v_refs], axis=0).astype(BF16)
    W = (n_prev + 1) * Ta
    col = lax.broadcasted_iota(jnp.int32, (1, W), 1)
    col_valid = (ai - n_prev) * Ta + col >= 0
    q = q_ref[...]
    outs, lses = [], []
    for h in range(H):
        hk = 0 if shared_kv else h
        qh = (q[:, h * HEAD_DIM:(h + 1) * HEAD_DIM] * (HEAD_DIM ** -0.5)).astype(BF16)
        s = _dot_nt(qh, k[:, hk * HEAD_DIM:(hk + 1) * HEAD_DIM]) + bt_ref[h]
        s = jnp.where(col_valid, s, NEG_INF)
        m = jnp.max(s, axis=-1, keepdims=True)
        p = jnp.exp(s - m)
        l = jnp.sum(p, axis=-1, keepdims=True)
        outs.append(_dot(p.astype(BF16), v[:, hk * HEAD_DIM:(hk + 1) * HEAD_DIM]) / l)
        if want_lse:
            lses.append(jnp.broadcast_to(m + jnp.log(l), (Ta, HEAD_DIM)))
    o_ref[...] = jnp.concatenate(outs, axis=1)
    if want_lse:
        lse_ref[...] = jnp.concatenate(lses, axis=1)


def band_attention(q, k, v, btab, Ta, n_prev, want_lse, widths=None, col_blocks=(0, 0, 0)):
    R, A = q.shape[:2]
    QW, KW = widths if widths else (q.shape[2], k.shape[2])
    H = QW // HEAD_DIM
    shared_kv = KW == HEAD_DIM
    nA = A // Ta
    qc, kc, vc = col_blocks

    def prev_map(p, c):
        return lambda r, a: (r, jnp.maximum(a - n_prev + p, 0), c)

    cur = lambda r, a: (r, a, 0)
    k_specs = ([pl.BlockSpec((None, Ta, KW), prev_map(p, kc)) for p in range(n_prev)]
               + [pl.BlockSpec((None, Ta, KW), lambda r, a: (r, a, kc))])
    v_specs = ([pl.BlockSpec((None, Ta, KW), prev_map(p, vc)) for p in range(n_prev)]
               + [pl.BlockSpec((None, Ta, KW), lambda r, a: (r, a, vc))])
    out_shape = [jax.ShapeDtypeStruct((R, A, QW), F32)]
    out_specs = [pl.BlockSpec((None, Ta, QW), cur)]
    if want_lse:
        out_shape.append(jax.ShapeDtypeStruct((R, A, QW), F32))
        out_specs.append(pl.BlockSpec((None, Ta, QW), cur))
    res = pl.pallas_call(
        functools.partial(_band_kernel, n_prev=n_prev, Ta=Ta, H=H, shared_kv=shared_kv, want_lse=want_lse),
        grid=(R, nA),
        in_specs=[pl.BlockSpec((None, Ta, QW), lambda r, a: (r, a, qc))] + k_specs + v_specs
                 + [pl.BlockSpec(btab.shape, lambda r, a: (0, 0, 0))],
        out_specs=out_specs,
        out_shape=out_shape,
        compiler_params=_cparams(("parallel", "arbitrary")),
    )(q, *([k] * (n_prev + 1)), *([v] * (n_prev + 1)), btab)
    return res if want_lse else res[0]


def _gelu_tanh(x):
    return 0.5 * x * (1.0 + jnp.tanh(math.sqrt(2.0 / math.pi) * (x + 0.044715 * (x * x * x))))


def _compress_kernel(ks_ref, vs_ref, pk_ref, pv_ref, w1k_ref, w2k_ref, w1v_ref, w2vt_ref, kc_ref, vct_ref):
    def hidden(src, pos, w1):
        x = src[...]
        n = x.shape[0]
        first = _dot((x + pos[0:1, :]).astype(BF16), w1[0])
        second = _dot((x + pos[1:2, :]).astype(BF16), w1[1])
        return _gelu_tanh(first + pltpu.roll(second, n - 1, axis=0)).astype(BF16)

    kc_ref[...] = _dot(hidden(ks_ref, pk_ref, w1k_ref), w2k_ref[...]).astype(BF16)
    vct_ref[...] = _dot_nt(w2vt_ref[...], hidden(vs_ref, pv_ref, w1v_ref)).astype(BF16)


def nsa_compress(k_chunks, v_chunks, pos_k, pos_v, w1k, w2k, w1v, w2v_t):
    n = k_chunks.shape[0]
    return pl.pallas_call(
        _compress_kernel,
        out_shape=[jax.ShapeDtypeStruct((n, HEAD_DIM), BF16), jax.ShapeDtypeStruct((HEAD_DIM, n), BF16)],
        compiler_params=_cparams(None),
    )(k_chunks, v_chunks, pos_k, pos_v, w1k, w2k, w1v, w2v_t)


def _cmp_topk_kernel(nq_ref, ks_ref, vs_ref, kc_ref, vct_ref, ovt_ref, eye_ref,
                     ocmp_ref, qaug_ref, st_ref, kaug_ref, vst_ref, *, Tq, n_cmp, n_sel, nbg):
    q0 = pl.program_id(0) * Tq
    t = q0 + lax.broadcasted_iota(jnp.int32, (n_cmp, Tq), 1)
    cmp_end = NSA_CMP_STRIDE * lax.broadcasted_iota(jnp.int32, (n_cmp, Tq), 0) + (NSA_CMP_LEN - 1)
    c_mask = cmp_end <= t
    kc = kc_ref[...]
    vct = vct_ref[...]
    q = nq_ref[...]
    eye = eye_ref[...]
    ks = ks_ref[...]
    ks_t = _dot_nt(eye, ks)
    tok = q0 + lax.broadcasted_iota(jnp.int32, (Tq, nbg), 0)
    blk = jnp.bitwise_and(jnp.right_shift(tok, NSA_SEL_BLOCK.bit_length() - 1), nbg - 1)
    onehot = jnp.where(blk == lax.broadcasted_iota(jnp.int32, (Tq, nbg), 1), 1.0, 0.0).astype(BF16)
    stat_cols = jnp.where(lax.broadcasted_iota(jnp.int32, (Tq, STAT_ROWS), 1) < 3, 1.0, 0.0).astype(BF16)
    kaug_ref[...] = jnp.concatenate([ks, onehot, stat_cols], axis=1)
    ones_row = jnp.where(lax.broadcasted_iota(jnp.int32, (STAT_ROWS, Tq), 0) < 1, 1.0, 0.0).astype(BF16)
    vst_ref[...] = jnp.concatenate([_dot_nt(eye, vs_ref[...]).astype(BF16), ones_row], axis=0)
    psum = jnp.zeros((n_cmp, Tq), F32)
    outs, q_ts, qn2, l_self = [], [], [], []
    for h in range(NSA_HEADS):
        qh = (q[:, h * HEAD_DIM:(h + 1) * HEAD_DIM] * (HEAD_DIM ** -0.5 * LOG2E)).astype(BF16)
        q_t = _dot_nt(eye, qh)
        q_ts.append(q_t.astype(BF16))
        qn2.append(jnp.sum(q_t * q_t, axis=0, keepdims=True))
        l_self.append(jnp.sum(q_t * ks_t, axis=0, keepdims=True))
        s = jnp.where(c_mask, _dot(kc, q_ts[h]), NEG_INF)
        p = jnp.exp2(s - jnp.max(s, axis=0, keepdims=True))
        p = jnp.where(c_mask, p * (1.0 / jnp.sum(p, axis=0, keepdims=True)), 0.0)
        psum = psum + p
        outs.append(_dot(vct, p.astype(BF16)).T)
    ocmp_ref[...] = jnp.concatenate(outs, axis=1)
    st_ref[0] = jnp.concatenate(qn2, axis=0)
    st_ref[1] = jnp.broadcast_to(jnp.sum(ks_t * ks_t, axis=0, keepdims=True), (NSA_HEADS, Tq))
    st_ref[2] = jnp.concatenate(l_self, axis=0)
    hi = psum.astype(BF16)
    lo = (psum - hi.astype(F32)).astype(BF16)
    imp = _dot(ovt_ref[...], hi) + _dot(ovt_ref[...], lo)
    t = q0 + lax.broadcasted_iota(jnp.int32, (n_sel, Tq), 1)
    j = lax.broadcasted_iota(jnp.int32, (n_sel, Tq), 0)
    start = j * NSA_SEL_BLOCK
    cur_blk = (start <= t) & (t < start + NSA_SEL_BLOCK)
    prev_blk = (start + NSA_SEL_BLOCK <= t) & (t < start + 2 * NSA_SEL_BLOCK)
    forced = (j == 0) | cur_blk | prev_blk
    score = jnp.where(forced, NSA_FORCE_SCORE, jnp.where(start <= t, imp, -NSA_FORCE_SCORE))
    jf = j.astype(F32)
    sel = jnp.zeros((n_sel, Tq), jnp.bool_)
    for _ in range(min(NSA_TOPK, n_sel)):
        mx = jnp.max(score, axis=0, keepdims=True)
        first = jnp.min(jnp.where(score == mx, jf, float(n_sel)), axis=0, keepdims=True)
        hit = jf == first
        sel = sel | hit
        score = jnp.where(hit, -3e38, score)
    sel_bias_t = jnp.where(sel, 0.0, SEL_NEG).astype(BF16)
    for h in range(NSA_HEADS):
        for g in range(n_sel // nbg):
            qaug_ref[g, h] = jnp.concatenate([q_ts[h], sel_bias_t[g * nbg:(g + 1) * nbg]], axis=0)


def nsa_cmp_topk(y, nq_block, k_sel, v_sel, kc, vc_t, overlap_t, Tq, nbg):
    S = y.shape[0]
    n_cmp = kc.shape[0]
    n_sel = S // NSA_SEL_BLOCK
    G = n_sel // nbg
    assert nbg & (nbg - 1) == 0 and NSA_SEL_BLOCK & (NSA_SEL_BLOCK - 1) == 0
    const = lambda a: pl.BlockSpec(a.shape, lambda i: (0,) * a.ndim)
    rows = lambda w: pl.BlockSpec((Tq, w), lambda i: (i, 0))
    eye = jnp.eye(HEAD_DIM, dtype=BF16)
    kw = HEAD_DIM + nbg + STAT_ROWS
    return pl.pallas_call(
        functools.partial(_cmp_topk_kernel, Tq=Tq, n_cmp=n_cmp, n_sel=n_sel, nbg=nbg),
        grid=(S // Tq,),
        in_specs=[pl.BlockSpec((Tq, NSA_W), lambda i: (i, nq_block)), rows(HEAD_DIM), rows(HEAD_DIM),
                  const(kc), const(vc_t), const(overlap_t), const(eye)],
        out_specs=[rows(NSA_W),
                   pl.BlockSpec((G, NSA_HEADS, HEAD_DIM + nbg, Tq), lambda i: (0, 0, 0, i)),
                   pl.BlockSpec((3, NSA_HEADS, Tq), lambda i: (0, 0, i)),
                   rows(kw),
                   pl.BlockSpec((HEAD_DIM + STAT_ROWS, Tq), lambda i: (0, i))],
        out_shape=[jax.ShapeDtypeStruct((S, NSA_W), F32),
                   jax.ShapeDtypeStruct((G, NSA_HEADS, HEAD_DIM + nbg, S), BF16),
                   jax.ShapeDtypeStruct((3, NSA_HEADS, S), F32),
                   jax.ShapeDtypeStruct((S, kw), BF16),
                   jax.ShapeDtypeStruct((HEAD_DIM + STAT_ROWS, S), BF16)],
        compiler_params=_cparams(("parallel",)),
    )(y, k_sel, v_sel, kc, vc_t, overlap_t, eye)


def _outproj_even_kernel(oa_ref, ob_ref, wa_ref, wb_ref, g_ref, h_ref, out_ref):
    m = _dot(oa_ref[...], wa_ref[...]) + _dot(ob_ref[...], wb_ref[...])
    out_ref[...] = h_ref[...] + _rms(m, g_ref[...])


def outproj_even(oa, ob, wa, wb, g, h, tm):
    S, D = h.shape
    rows = lambda w: pl.BlockSpec((tm, w), lambda i: (i, 0))
    full = lambda a: pl.BlockSpec(a.shape, lambda i: (0,) * a.ndim)
    return pl.pallas_call(
        _outproj_even_kernel,
        grid=(S // tm,),
        in_specs=[rows(oa.shape[1]), rows(ob.shape[1]), full(wa), full(wb), full(g), rows(D)],
        out_specs=rows(D),
        out_shape=jax.ShapeDtypeStruct((S, D), F32),
        compiler_params=_cparams(("parallel",)),
    )(oa, ob, wa, wb, g, h)


def _outproj_odd_kernel(o0_ref, o1_ref, o2_ref, l0_ref, l1_ref, l2_ref, oc_ref, os_ref, ow_ref,
                        gc_ref, gs_ref, gw_ref, wd_ref, wn_ref, g_ref, h_ref, out_ref):
    l0, l1, l2 = l0_ref[...], l1_ref[...], l2_ref[...]
    mx = jnp.maximum(jnp.maximum(l0, l1), l2)
    e0, e1, e2 = jnp.exp(l0 - mx), jnp.exp(l1 - mx), jnp.exp(l2 - mx)
    o_dil = (e0 * o0_ref[...] + e1 * o1_ref[...] + e2 * o2_ref[...]) / (e0 + e1 + e2)
    o_nsa = (jax.nn.sigmoid(gc_ref[...]) * oc_ref[...] + jax.nn.sigmoid(gs_ref[...]) * os_ref[...].astype(F32)
             + jax.nn.sigmoid(gw_ref[...]) * ow_ref[...])
    m = _dot(o_dil.astype(BF16), wd_ref[...]) + _dot(o_nsa.astype(BF16), wn_ref[...])
    out_ref[...] = h_ref[...] + _rms(m, g_ref[...])


def outproj_odd(o_dil, lse_dil, o_cmp, o_sel, o_win, y, gate_block, wd, wn, g, h, tm):
    S, D = h.shape
    rows = lambda w: pl.BlockSpec((tm, w), lambda i: (i, 0))
    full = lambda a: pl.BlockSpec(a.shape, lambda i: (0,) * a.ndim)
    gate = lambda b: pl.BlockSpec((tm, NSA_W), lambda i: (i, gate_block + b))
    return pl.pallas_call(
        _outproj_odd_kernel,
        grid=(S // tm,),
        in_specs=[rows(DIL_GW)] * 6 + [rows(NSA_W)] * 3 + [gate(0), gate(1), gate(2), full(wd), full(wn), full(g),
                                                            rows(D)],
        out_specs=rows(D),
        out_shape=jax.ShapeDtypeStruct((S, D), F32),
        compiler_params=_cparams(("parallel",)),
    )(*o_dil, *lse_dil, o_cmp, o_sel, o_win, y, y, y, wd, wn, g, h)


def _mlp_ple_kernel(h_ref, g1_ref, wu_ref, wd_ref, g2_ref, g3_ref, wg_ref, p_ref, wp_ref, out_ref,
                    hn_ref, acc_ref):
    j = pl.program_id(1)

    @pl.when(j == 0)
    def _():
        hn_ref[...] = _rms(h_ref[...], g1_ref[...]).astype(BF16)
        acc_ref[...] = jnp.zeros(acc_ref.shape, F32)

    u = jnp.maximum(_dot(hn_ref[...], wu_ref[...]), 0.0)
    acc_ref[...] += _dot((u * u).astype(BF16), wd_ref[...])

    @pl.when(j == pl.num_programs(1) - 1)
    def _():
        h2 = h_ref[...] + _rms(acc_ref[...], g2_ref[...])
        gate = jax.nn.sigmoid(_dot(_rms(h2, g3_ref[...]).astype(BF16), wg_ref[...]))
        out_ref[...] = h2 + gate * _dot(p_ref[...].astype(BF16), wp_ref[...])


def mlp_ple(h, g1, wu, wd, g2, g3, wg, p, wp, tm, tf):
    S, D = h.shape
    FF = wu.shape[1]
    PD = p.shape[1]
    const = lambda a: pl.BlockSpec(a.shape, lambda i, j: (0,) * a.ndim)
    return pl.pallas_call(
        _mlp_ple_kernel,
        grid=(S // tm, FF // tf),
        in_specs=[pl.BlockSpec((tm, D), lambda i, j: (i, 0)), const(g1),
                  pl.BlockSpec((D, tf), lambda i, j: (0, j)),
                  pl.BlockSpec((tf, D), lambda i, j: (j, 0)),
                  const(g2), const(g3), const(wg),
                  pl.BlockSpec((tm, PD), lambda i, j: (i, 0)), const(wp)],
        out_specs=pl.BlockSpec((tm, D), lambda i, j: (i, 0)),
        out_shape=jax.ShapeDtypeStruct((S, D), F32),
        scratch_shapes=[pltpu.VMEM((tm, D), BF16), pltpu.VMEM((tm, D), F32)],
        compiler_params=_cparams(("parallel", "arbitrary")),
    )(h, g1, wu, wd, g2, g3, wg, p, wp)


def _t5_bucket_of(dist):
    n = jnp.maximum(dist, 0)
    max_exact = T5_BUCKETS // 2
    ratio = jnp.log(jnp.maximum(n, 1).astype(F32) / max_exact) / math.log(T5_MAX_DIST / max_exact)
    large = jnp.minimum(max_exact + (ratio * (T5_BUCKETS - max_exact)).astype(jnp.int32), T5_BUCKETS - 1)
    return jnp.where(n < max_exact, n, large)


def _bias_of_dist(bias, dist):
    bucket = _t5_bucket_of(dist)[None]
    out = jnp.zeros((bias.shape[1],) + dist.shape, F32)
    for b in range(T5_BUCKETS):
        out = jnp.where(bucket == b, bias[b].reshape((-1,) + (1,) * dist.ndim), out)
    return out


def _causal_tables(T):
    k = jnp.arange(T)[:, None]
    q = jnp.arange(T)[None, :]
    diag = jnp.where(q >= k, 0.0, NEG_INF).astype(F32)
    return jnp.stack([diag, jnp.zeros((T, T), F32)])[:, None]


def _t5_delta_tables(bias, T):
    n_delta = -(-(T5_MAX_DIST - 1) // T) + 2
    k = jnp.arange(T)[None, :, None]
    q = jnp.arange(T)[None, None, :]
    dist = jnp.arange(n_delta)[:, None, None] * T + q - k
    val = (_bias_of_dist(bias, dist) - bias[T5_BUCKETS - 1].reshape(-1, 1, 1, 1)) * LOG2E
    return jnp.transpose(jnp.where(dist[None] >= 0, val, NEG_INF), (1, 0, 2, 3))


def _band_table(bias, Ta, n_prev, max_rel, stride, inclusive):
    P = n_prev * Ta
    i = jnp.arange(Ta)[:, None]
    c = jnp.arange(P + Ta)[None, :]
    rel = i + P - c
    ok = (rel >= 0) & ((rel <= max_rel) if inclusive else (rel < max_rel))
    return jnp.where(ok[None], _bias_of_dist(bias, rel * stride), NEG_INF)


def _tile(S, pref):
    t = min(pref, S)
    assert S % t == 0
    return t


def even_mixer_core(h, g_pre, w_in, cos, sin, w_uq, q_norm, w_ukv, kv_norm, forget_bias, T):
    S = h.shape[0]
    scale = MLA_QK_DIM ** -0.5 * LOG2E
    half = MLA_ROPE_DIM // 2
    cos_t, sin_t = jnp.transpose(cos), jnp.transpose(sin)
    ones = jnp.ones((MLA_NOPE_DIM, S), F32)
    cq = jnp.concatenate([ones, cos_t, cos_t], axis=0) * scale
    sq = jnp.concatenate([0.0 * ones, -sin_t, sin_t], axis=0) * scale
    ck = jnp.concatenate([cos, cos], axis=1)
    sk = jnp.concatenate([-sin, sin], axis=1)
    wq = jnp.transpose(w_uq.reshape(MLA_Q_RANK, MLA_HEADS, MLA_QK_DIM), (1, 2, 0))
    swap = np.concatenate([np.arange(MLA_NOPE_DIM), MLA_NOPE_DIM + half + np.arange(half),
                           MLA_NOPE_DIM + np.arange(half)])
    wqs = wq[:, swap, :]
    wkv = jnp.transpose(w_ukv.reshape(MLA_KV_RANK, MLA_HEADS, MLA_NOPE_DIM + MLA_V_DIM), (1, 0, 2))
    wk = jnp.pad(wkv[:, :, :MLA_NOPE_DIM], ((0, 0), (0, 0), (0, MLA_ROPE_DIM)))
    wv = jnp.transpose(wkv[:, :, MLA_NOPE_DIM:], (0, 2, 1))
    e96 = jnp.pad(jnp.eye(MLA_ROPE_DIM, dtype=F32), ((0, 0), (MLA_NOPE_DIM, 0)))
    qm, km, vm, fq, fk, fv, logf, st = even_prep(
        h, g_pre, w_in, wq.astype(BF16), wqs.astype(BF16), wk.astype(BF16), e96.astype(BF16), wv.astype(BF16),
        jnp.eye(MLA_QK_DIM, dtype=BF16), q_norm[None, :], kv_norm[None, :], forget_bias[None, :],
        cq, sq, ck, sk, _tile(S, 512))
    neg_f = cumsum_split(jnp.transpose(logf))
    hps = 4
    causal = _causal_tables(T)
    r_mla, fast_mla = _reference_rows(st[0], st[1], st[2], 0.0, 0.0, 0.0, T, hps)
    o_mla = flash_causal(qm[None], r_mla, km, vm, causal, fast_mla, T, hps)
    r_fox, fast_fox = _reference_rows(st[3], st[4], st[5], jnp.sum(neg_f.astype(F32), axis=0), 0.0, 0.0, T, hps)
    fk_aug = jnp.concatenate([fk, jnp.transpose(neg_f, (1, 2, 0)),
                              jnp.zeros((FOX_HEADS, S, FOX_PAD - FOX_SPLIT), BF16), _stat_cols(FOX_HEADS, S)], axis=2)
    o_fox = flash_causal(fq[None], r_fox, fk_aug, fv, causal, fast_fox, T, hps)
    return o_mla, o_fox


def odd_mixer_core(y, t5_bias, pos_k, pos_v, w1k, w2k, w1v, w2v, T):
    S = y.shape[0]
    G = len(DIL_GROUPS)
    Ta = 128
    o_dil, lse_dil = [], []
    for g, (w, d) in enumerate(DIL_GROUPS):
        btab = _band_table(t5_bias[:, g * DIL_HEADS:(g + 1) * DIL_HEADS], Ta, 1, w // d, d, True)
        if d == 1:
            o, lse = band_attention(y[None], y[None], y[None], btab, _tile(S, Ta), 1, True,
                                    widths=(DIL_GW, DIL_GW), col_blocks=(g, G + g, 2 * G + g))
        else:
            def deint(a):
                return jnp.transpose(a.reshape(S // d, d, DIL_GW), (1, 0, 2))
            q, k, v = (deint(y[:, (b * G + g) * DIL_GW:(b * G + g + 1) * DIL_GW]) for b in range(3))
            o, lse = band_attention(q, k, v, btab, _tile(S // d, Ta), 1, True)
        o_dil.append(jnp.transpose(o, (1, 0, 2)).reshape(S, DIL_GW))
        lse_dil.append(jnp.transpose(lse, (1, 0, 2)).reshape(S, DIL_GW))
    nq_block = 3 * G
    gate_block = nq_block + 1
    base = (gate_block + 3) * NSA_W
    k_cmp, v_cmp, k_sel, v_sel, k_win, v_win = (y[:, base + i * HEAD_DIM:base + (i + 1) * HEAD_DIM] for i in range(6))
    bias_nsa = t5_bias[:, G * DIL_HEADS:]
    n_chunk = S // NSA_CMP_STRIDE
    cw = NSA_CMP_STRIDE * HEAD_DIM
    kc, vc = nsa_compress(k_cmp.reshape(n_chunk, cw), v_cmp.reshape(n_chunk, cw),
                          pos_k.reshape(2, cw), pos_v.reshape(2, cw),
                          w1k.reshape(2, cw, -1).astype(BF16), w2k.astype(BF16),
                          w1v.reshape(2, cw, -1).astype(BF16), jnp.transpose(w2v).astype(BF16))
    n_sel = S // NSA_SEL_BLOCK
    ci = np.arange(n_chunk)[None, :] * NSA_CMP_STRIDE
    sj = np.arange(n_sel)[:, None] * NSA_SEL_BLOCK
    overlap_t = jnp.asarray(((ci < sj + NSA_SEL_BLOCK) & (ci + NSA_CMP_LEN > sj)).astype(np.float32), BF16)
    nbg = min(n_sel, NSA_SEL_GROUP_BLOCKS)
    o_cmp, qaug, st, kaug, v_sel_t = nsa_cmp_topk(y, nq_block, k_sel.astype(BF16), v_sel.astype(BF16), kc, vc,
                                                  overlap_t, _tile(S, 256), nbg)
    delta = (bias_nsa - bias_nsa[T5_BUCKETS - 1]) * LOG2E
    r_sel, fast_sel = _reference_rows(st[0], st[1, :1], st[2], 0.0, jnp.max(delta, axis=0)[:, None],
                                      delta[0][:, None], T, NSA_HEADS)
    o_sel = flash_causal(qaug, r_sel, kaug[None], v_sel_t[None], _t5_delta_tables(bias_nsa, T), fast_sel, T,
                         hps=NSA_HEADS, key_group_tokens=nbg * NSA_SEL_BLOCK)
    Tw = _tile(S, 256)
    n_prev = NSA_WINDOW // Tw
    wtab = _band_table(bias_nsa, Tw, n_prev, NSA_WINDOW, 1, False)
    o_win = band_attention(y[None], k_win[None], v_win[None], wtab, Tw, n_prev, False,
                           widths=(NSA_W, HEAD_DIM), col_blocks=(nq_block, 0, 0))[0]
    return o_dil, lse_dil, o_cmp, o_sel, o_win, gate_block


def _pad_cols(w, n):
    return jnp.pad(w, ((0, 0), (0, n - w.shape[1])))


def _even_w_in(w):
    cq, ckv, kr, fq, fk, fv, fl = jnp.split(w, np.cumsum([256, 128, 32, 512, 512, 512])[:], axis=1)
    half = MLA_ROPE_DIM // 2
    kr_sw = jnp.concatenate([kr[:, half:], kr[:, :half]], axis=1)
    head = _pad_cols(jnp.concatenate([cq, ckv, kr, kr_sw, fl], axis=1), 512)
    return jnp.concatenate([head, fq, fk, fv], axis=1).astype(BF16)


def _odd_w_in(w):
    n_main = 3 * 3 * DIL_GW + NSA_W
    main, six = w[:, :n_main], w[:, n_main:n_main + 6 * HEAD_DIM]
    gl = w[:, n_main + 6 * HEAD_DIM:]
    gl = jnp.transpose(gl.reshape(-1, NSA_HEADS, 3), (0, 2, 1))
    gl = jnp.repeat(gl[..., None], HEAD_DIM, axis=-1).reshape(w.shape[0], 3 * NSA_W)
    return _pad_cols(jnp.concatenate([main, gl, six], axis=1), ODD_IN_PAD).astype(BF16)


def _trunk(x, p, positions, t5_bias, ev_w_in, ev_q_norm, ev_w_uq, ev_kv_norm, ev_w_ukv, ev_forget_bias,
           ev_w_out, od_w_in, od_cmp_pos_k, od_cmp_pos_v, od_cmp_w1_k, od_cmp_w2_k, od_cmp_w1_v, od_cmp_w2_v,
           od_w_out, norm_mix_pre, norm_mix_post, norm_mlp_pre, norm_mlp_post, w_mlp_up, w_mlp_down,
           ple_norm, w_ple_gate, w_ple_proj):
    S, D = x.shape
    depth = p.shape[0]
    T = _tile(S, 512)
    tm = _tile(S, 512)
    inv_freq = ROPE_THETA ** (-jnp.arange(0, MLA_ROPE_DIM, 2, dtype=F32) / MLA_ROPE_DIM)
    angles = positions.astype(F32)[:, None] * inv_freq
    cos, sin = jnp.cos(angles), jnp.sin(angles)
    h = x
    for i in range(depth):
        j = i // 2
        if i % 2 == 0:
            o_mla, o_fox = even_mixer_core(h, norm_mix_pre[i][None], _even_w_in(ev_w_in[j]), cos, sin, ev_w_uq[j],
                                           ev_q_norm[j], ev_w_ukv[j], ev_kv_norm[j], ev_forget_bias[j], T)
            wo = ev_w_out[j].astype(BF16)
            na = MLA_HEADS * MLA_V_DIM
            h = outproj_even(o_mla, o_fox, wo[:na], wo[na:], norm_mix_post[i][None], h, tm)
        else:
            y = norm_matmul(h, norm_mix_pre[i][None], _odd_w_in(od_w_in[j]), tm, 768)
            o_dil, lse_dil, o_cmp, o_sel, o_win, gate_block = odd_mixer_core(
                y, t5_bias, od_cmp_pos_k[j], od_cmp_pos_v[j], od_cmp_w1_k[j], od_cmp_w2_k[j],
                od_cmp_w1_v[j], od_cmp_w2_v[j], T)
            wo = od_w_out[j].astype(BF16)
            h = outproj_odd(o_dil, lse_dil, o_cmp, o_sel, o_win, y, gate_block, wo[:DIL_GW], wo[DIL_GW:],
                            norm_mix_post[i][None], h, tm)
        h = mlp_ple(h, norm_mlp_pre[i][None], w_mlp_up[i].astype(BF16), w_mlp_down[i].astype(BF16),
                    norm_mlp_post[i][None], ple_norm[i][None], w_ple_gate[i].astype(BF16), p[i],
                    w_ple_proj[i].astype(BF16), tm, 1024)
    return h


def kernel(x, p, positions, t5_bias, ev_w_in, ev_q_norm, ev_w_uq, ev_kv_norm, ev_w_ukv, ev_forget_bias, ev_w_out, od_w_in, od_cmp_pos_k, od_cmp_pos_v, od_cmp_w1_k, od_cmp_w2_k, od_cmp_w1_v, od_cmp_w2_v, od_w_out, norm_mix_pre, norm_mix_post, norm_mlp_pre, norm_mlp_post, w_mlp_up, w_mlp_down, ple_norm, w_ple_gate, w_ple_proj):
    params = (t5_bias, ev_w_in, ev_q_norm, ev_w_uq, ev_kv_norm, ev_w_ukv, ev_forget_bias, ev_w_out, od_w_in,
              od_cmp_pos_k, od_cmp_pos_v, od_cmp_w1_k, od_cmp_w2_k, od_cmp_w1_v, od_cmp_w2_v, od_w_out,
              norm_mix_pre, norm_mix_post, norm_mlp_pre, norm_mlp_post, w_mlp_up, w_mlp_down, ple_norm,
              w_ple_gate, w_ple_proj)
    outs = [_trunk(x[b], p[:, b], positions[b], *params) for b in range(x.shape[0])]
    return jnp.stack(outs).astype(x.dtype)
```

```python
import functools
import math

import numpy as np
import jax
import jax.numpy as jnp
from jax import lax
from jax.experimental import pallas as pl
from jax.experimental.pallas import tpu as pltpu

F32 = jnp.float32
BF16 = jnp.bfloat16

HEAD_DIM = 64
RMS_EPS = 1e-6
NEG_INF = -1e30
SEL_NEG = -(2.0 ** 99)

MLA_HEADS = 8
MLA_NOPE_DIM = 64
MLA_ROPE_DIM = 32
MLA_V_DIM = 64
MLA_Q_RANK = 256
MLA_KV_RANK = 128
MLA_QK_DIM = MLA_NOPE_DIM + MLA_ROPE_DIM
ROPE_THETA = 10000.0
FOX_HEADS = 8
FOX_SPLIT = 3
FOX_PAD = 16
LOG2E = math.log2(math.e)
STAT_ROWS = 16
FAST_OVER = 64.0
FAST_MARGIN = 150.0
P_CHUNK = 32
DIL_GROUPS = ((128, 1), (512, 4), (2048, 16))
DIL_HEADS = 4
DIL_GW = DIL_HEADS * HEAD_DIM
NSA_HEADS = 4
NSA_W = NSA_HEADS * HEAD_DIM
NSA_CMP_LEN = 32
NSA_CMP_STRIDE = 16
NSA_SEL_BLOCK = 64
NSA_TOPK = 16
NSA_WINDOW = 512
NSA_FORCE_SCORE = 1e9
NSA_SEL_GROUP_BLOCKS = 128
T5_BUCKETS = 32
T5_MAX_DIST = 2048

EVEN_IN_PAD = 2048
ODD_IN_PAD = 3840

VMEM_LIMIT_BYTES = 56 * 1024 * 1024


def _cparams(sem):
    return pltpu.CompilerParams(dimension_semantics=sem, vmem_limit_bytes=VMEM_LIMIT_BYTES)


def _rms(x, g):
    return x * lax.rsqrt(jnp.mean(x * x, axis=-1, keepdims=True) + RMS_EPS) * g


def _dot(a, b):
    return jnp.dot(a, b, preferred_element_type=F32)


def _dot_nt(a, b):
    return lax.dot_general(a, b, (((1,), (1,)), ((), ())), preferred_element_type=F32)


def _norm_matmul_kernel(h_ref, g_ref, w_ref, o_ref, hn_ref):
    @pl.when(pl.program_id(1) == 0)
    def _():
        hn_ref[...] = _rms(h_ref[...], g_ref[...]).astype(BF16)

    o_ref[...] = _dot(hn_ref[...], w_ref[...])


def norm_matmul(h, g, w, tm, tn):
    S, D = h.shape
    N = w.shape[1]
    return pl.pallas_call(
        _norm_matmul_kernel,
        grid=(S // tm, N // tn),
        in_specs=[pl.BlockSpec((tm, D), lambda i, j: (i, 0)),
                  pl.BlockSpec((1, D), lambda i, j: (0, 0)),
                  pl.BlockSpec((D, tn), lambda i, j: (0, j))],
        out_specs=pl.BlockSpec((tm, tn), lambda i, j: (i, j)),
        out_shape=jax.ShapeDtypeStruct((S, N), F32),
        scratch_shapes=[pltpu.VMEM((tm, D), BF16)],
        compiler_params=_cparams(("parallel", "arbitrary")),
    )(h, g, w)


def _even_prep_kernel(h_ref, g_ref, win_ref, wq_ref, wqs_ref, wk_ref, e_ref, wv_ref, eye_ref, qn_ref, kvn_ref,
                      fb_ref, cq_ref, sq_ref, ck_ref, sk_ref,
                      qm_ref, km_ref, vm_ref, fq_ref, fk_ref, fv_ref, lf_ref, st_ref):
    y = _dot(_rms(h_ref[...], g_ref[...]).astype(BF16), win_ref[...])
    tm = y.shape[0]
    c_q = y[:, 0:MLA_Q_RANK]
    c_kv = y[:, MLA_Q_RANK:MLA_Q_RANK + MLA_KV_RANK]
    o = MLA_Q_RANK + MLA_KV_RANK
    k_r = y[:, o:o + MLA_ROPE_DIM]
    k_rs = y[:, o + MLA_ROPE_DIM:o + 2 * MLA_ROPE_DIM]
    f_logit = y[:, o + 2 * MLA_ROPE_DIM:o + 2 * MLA_ROPE_DIM + FOX_HEADS]
    nq = _rms(c_q, qn_ref[...]).astype(BF16)
    nkv = _rms(c_kv, kvn_ref[...]).astype(BF16)
    k_rot = (k_r * ck_ref[...] + k_rs * sk_ref[...]).astype(BF16)
    k_rot96 = _dot(k_rot, e_ref[...])
    cq = cq_ref[...]
    sq = sq_ref[...]
    eye = eye_ref[...]
    row = lax.broadcasted_iota(jnp.int32, (STAT_ROWS, tm), 0)
    ones_row = jnp.where(row < 1, 1.0, 0.0).astype(BF16)
    col = lax.broadcasted_iota(jnp.int32, (tm, STAT_ROWS), 1)
    stat_cols = jnp.where(col < 3, 1.0, 0.0).astype(BF16)
    stats = [[] for _ in range(6)]

    def record(slot, q_t, k):
        q_t = q_t.astype(F32)
        k_t = _dot_nt(eye[:k.shape[1], :k.shape[1]], k)
        stats[slot].append(jnp.sum(q_t * q_t, axis=0, keepdims=True))
        stats[slot + 1].append(jnp.sum(k_t * k_t, axis=0, keepdims=True))
        stats[slot + 2].append(jnp.sum(q_t * k_t, axis=0, keepdims=True))

    for h in range(MLA_HEADS):
        q = (_dot_nt(wq_ref[h], nq) * cq + _dot_nt(wqs_ref[h], nq) * sq).astype(BF16)
        k = (_dot(nkv, wk_ref[h]) + k_rot96).astype(BF16)
        qm_ref[h] = q
        km_ref[h] = jnp.concatenate([k, stat_cols], axis=1)
        vm_ref[h] = jnp.concatenate([_dot_nt(wv_ref[h], nkv).astype(BF16), ones_row], axis=0)
        record(0, q, k)
    base = 512
    fw = FOX_HEADS * HEAD_DIM
    eye64 = eye[:HEAD_DIM, :HEAD_DIM]
    ones_rows = jnp.where(row < FOX_SPLIT, 1.0, 0.0).astype(BF16)
    for h in range(FOX_HEADS):
        lo = base + h * HEAD_DIM
        fq = _dot_nt(eye64, (y[:, lo:lo + HEAD_DIM] * (HEAD_DIM ** -0.5 * LOG2E)).astype(BF16)).astype(BF16)
        fk = y[:, lo + fw:lo + fw + HEAD_DIM].astype(BF16)
        fq_ref[h] = jnp.concatenate([fq, ones_rows], axis=0)
        fk_ref[h] = fk
        fv = y[:, lo + 2 * fw:lo + 2 * fw + HEAD_DIM].astype(BF16)
        fv_ref[h] = jnp.concatenate([_dot_nt(eye64, fv).astype(BF16), ones_row], axis=0)
        record(3, fq, fk)
    for i in range(6):
        st_ref[i] = jnp.concatenate(stats[i], axis=0)
    z = f_logit + fb_ref[...]
    lf_ref[...] = jnp.minimum(z, 0.0) - jnp.log1p(jnp.exp(-jnp.abs(z)))


def even_prep(h, g, w_in, wq, wqs, wk, e96, wv, eye, qn, kvn, fb, cq, sq, ck, sk, tm):
    S, D = h.shape
    full = lambda a: pl.BlockSpec(a.shape, lambda i: (0,) * a.ndim)
    rows = lambda w: pl.BlockSpec((tm, w), lambda i: (i, 0))
    cols = lambda w: pl.BlockSpec((w, tm), lambda i: (0, i))
    heads = lambda n, w: pl.BlockSpec((n, tm, w), lambda i: (0, i, 0))
    heads_t = lambda n, w: pl.BlockSpec((n, w, tm), lambda i: (0, 0, i))
    out_shape = [jax.ShapeDtypeStruct((MLA_HEADS, MLA_QK_DIM, S), BF16),
                 jax.ShapeDtypeStruct((MLA_HEADS, S, MLA_QK_DIM + STAT_ROWS), BF16),
                 jax.ShapeDtypeStruct((MLA_HEADS, MLA_V_DIM + STAT_ROWS, S), BF16),
                 jax.ShapeDtypeStruct((FOX_HEADS, HEAD_DIM + FOX_PAD, S), BF16),
                 jax.ShapeDtypeStruct((FOX_HEADS, S, HEAD_DIM), BF16),
                 jax.ShapeDtypeStruct((FOX_HEADS, HEAD_DIM + STAT_ROWS, S), BF16),
                 jax.ShapeDtypeStruct((S, FOX_HEADS), F32),
                 jax.ShapeDtypeStruct((6, MLA_HEADS, S), F32)]
    out_specs = [heads_t(MLA_HEADS, MLA_QK_DIM), heads(MLA_HEADS, MLA_QK_DIM + STAT_ROWS),
                 heads_t(MLA_HEADS, MLA_V_DIM + STAT_ROWS),
                 heads_t(FOX_HEADS, HEAD_DIM + FOX_PAD), heads(FOX_HEADS, HEAD_DIM),
                 heads_t(FOX_HEADS, HEAD_DIM + STAT_ROWS),
                 rows(FOX_HEADS), heads_t(6, MLA_HEADS)]
    return pl.pallas_call(
        _even_prep_kernel,
        grid=(S // tm,),
        in_specs=[rows(D), full(g), full(w_in), full(wq), full(wqs), full(wk), full(e96), full(wv), full(eye),
                  full(qn), full(kvn), full(fb), cols(MLA_QK_DIM), cols(MLA_QK_DIM), rows(MLA_ROPE_DIM),
                  rows(MLA_ROPE_DIM)],
        out_specs=out_specs,
        out_shape=out_shape,
        compiler_params=_cparams(("parallel",)),
    )(h, g, w_in, wq, wqs, wk, e96, wv, eye, qn, kvn, fb, cq, sq, ck, sk)


def _cumsum_split_kernel(x_ref, o_ref):
    x = x_ref[...]
    n = x.shape[1]
    lane = lax.broadcasted_iota(jnp.int32, x.shape, 1)
    shift = 1
    while shift < n:
        x = x + jnp.where(lane >= shift, pltpu.roll(x, shift, axis=1), 0.0)
        shift *= 2
    r = -x * LOG2E
    for i in range(FOX_SPLIT):
        part = r.astype(BF16)
        o_ref[i] = part
        r = r - part.astype(F32)


def cumsum_split(x):
    return pl.pallas_call(
        _cumsum_split_kernel,
        out_shape=jax.ShapeDtypeStruct((FOX_SPLIT,) + x.shape, BF16),
        compiler_params=_cparams(None),
    )(x)


def _flash_kernel(qt_ref, kt_ref, fast_ref, q_ref, r_ref, k_ref, v_ref, bt_ref, o_ref, m_scr, acc_scr, p_scr, *,
                  hps, shared_kv, n_delta, nt):
    step = pl.program_id(1)
    qi = qt_ref[step]
    ki = kt_ref[step]
    fast = fast_ref[pl.program_id(0) * nt + qi] != 0
    near = qi - ki < n_delta - 1

    @pl.when(ki == 0)
    def _():
        m_scr[...] = jnp.full(m_scr.shape, -3e38, F32)
        acc_scr[...] = jnp.zeros(acc_scr.shape, F32)

    hb = bt_ref.shape[1]
    Tk = k_ref.shape[1]
    Dv = v_ref.shape[1] - STAT_ROWS

    def logits(h, with_table):
        q = jnp.concatenate([q_ref[0, h], r_ref[h]], axis=0)
        s = _dot(k_ref[0 if shared_kv else h], q)
        return s + bt_ref[0, h if hb > 1 else 0] if with_table else s

    def store_p(h, s, m):
        for c in range(Tk // P_CHUNK):
            rows = slice(c * P_CHUNK, (c + 1) * P_CHUNK)
            x = s[rows] if m is None else s[rows] - m
            p_scr[h, rows, :] = jnp.exp2(x).astype(BF16)

    def fast_update(with_table):
        acc_prev = [acc_scr[h] for h in range(hps)]
        for h in range(hps):
            store_p(h, logits(h, with_table), None)
        for h in range(hps):
            acc_scr[h] = acc_prev[h] + _dot(v_ref[0 if shared_kv else h], p_scr[h])

    def safe_update(with_table):
        m_prev = [m_scr[h] for h in range(hps)]
        acc_prev = [acc_scr[h] for h in range(hps)]
        ss = [logits(h, with_table) for h in range(hps)]
        ms = [jnp.maximum(m_prev[h], jnp.max(ss[h], axis=0, keepdims=True)) for h in range(hps)]
        for h in range(hps):
            store_p(h, ss[h], ms[h])
        for h in range(hps):
            alpha = jnp.exp2(m_prev[h] - ms[h])
            acc_scr[h] = alpha * acc_prev[h] + _dot(v_ref[0 if shared_kv else h], p_scr[h])
            m_scr[h] = ms[h]

    for take_fast, update in ((True, fast_update), (False, safe_update)):
        for with_table in (True, False):
            cond = jnp.logical_and(fast == take_fast, near == with_table)
            pl.when(cond)(functools.partial(update, with_table))

    @pl.when(ki == qi)
    def _():
        outs = [(acc_scr[h, :Dv, :] / acc_scr[h, Dv:Dv + 1, :]).T for h in range(hps)]
        o_ref[...] = jnp.concatenate(outs, axis=1).astype(o_ref.dtype)


def flash_causal(qT, r_rows, k, vT, btab, fast, T, hps, key_group_tokens=None, out_dtype=BF16):
    G, H, Dq, S = qT.shape
    Dk = Dq + STAT_ROWS
    Hk, Dv, _ = vT.shape
    Dv -= STAT_ROWS
    shared_kv = Hk == 1
    n_delta, Hb = btab.shape[0], btab.shape[1]
    nt = S // T
    qt = np.array([i for i in range(nt) for _ in range(i + 1)], np.int32)
    kt = np.array([j for i in range(nt) for j in range(i + 1)], np.int32)
    kv_h = 1 if shared_kv else hps
    tiles_per_group = (key_group_tokens // T) if G > 1 else 1

    def q_map(g, s, qt, kt, fast):
        return (kt[s] // tiles_per_group if G > 1 else 0, g, 0, qt[s])

    def bt_map(g, s, qt, kt, fast):
        return (jnp.minimum(qt[s] - kt[s], n_delta - 1), g if Hb > hps else 0, 0, 0)

    in_specs = [pl.BlockSpec((1, hps, Dq, T), q_map),
                pl.BlockSpec((hps, STAT_ROWS, T), lambda g, s, qt, kt, fast: (g, 0, qt[s])),
                pl.BlockSpec((kv_h, T, Dk), lambda g, s, qt, kt, fast: (0 if shared_kv else g, kt[s], 0)),
                pl.BlockSpec((kv_h, Dv + STAT_ROWS, T), lambda g, s, qt, kt, fast: (0 if shared_kv else g, 0, kt[s])),
                pl.BlockSpec((1, min(Hb, hps), T, T), bt_map)]
    grid_spec = pltpu.PrefetchScalarGridSpec(
        num_scalar_prefetch=3,
        grid=(H // hps, len(qt)),
        in_specs=in_specs,
        out_specs=pl.BlockSpec((T, hps * Dv), lambda g, s, qt, kt, fast: (qt[s], g)),
        scratch_shapes=[pltpu.VMEM((hps, 1, T), F32), pltpu.VMEM((hps, Dv + STAT_ROWS, T), F32),
                        pltpu.VMEM((hps, T, T), BF16)],
    )
    return pl.pallas_call(
        functools.partial(_flash_kernel, hps=hps, shared_kv=shared_kv, n_delta=n_delta, nt=nt),
        grid_spec=grid_spec,
        out_shape=jax.ShapeDtypeStruct((S, H * Dv), out_dtype),
        compiler_params=_cparams(("parallel", "arbitrary")),
    )(jnp.asarray(qt), jnp.asarray(kt), fast, qT, r_rows, k, vT, btab)


def _reference_rows(qn2, kn2, l_self, extra, bias_max, bias_self, T, hps):
    H, S = qn2.shape
    kmax = jnp.sqrt(jnp.max(kn2, axis=1, keepdims=True))
    bound = jnp.sqrt(qn2) * kmax + bias_max + extra
    ok = bound - (l_self + bias_self + extra) <= FAST_MARGIN
    fast = jnp.all(ok.reshape(H // hps, hps, S // T, T), axis=(1, 3)).astype(jnp.int32).reshape(-1)
    r = FAST_OVER - bound
    parts = []
    for _ in range(3):
        part = r.astype(BF16)
        parts.append(part)
        r = r - part.astype(F32)
    rows = jnp.stack(parts + [jnp.zeros_like(parts[0])] * (STAT_ROWS - 3), axis=1)
    return rows, fast


def _stat_cols(Hk, S):
    return jnp.broadcast_to((jnp.arange(STAT_ROWS) < 3).astype(BF16), (Hk, S, STAT_ROWS))


def _band_kernel(*refs, n_prev, Ta, H, shared_kv, want_lse):
    q_ref = refs[0]
    k_refs = refs[1:2 + n_prev]
    v_refs = refs[2 + n_prev:3 + 2 * n_prev]
    bt_ref = refs[3 + 2 * n_prev]
    o_ref = refs[4 + 2 * n_prev]
    lse_ref = refs[5 + 2 * n_prev] if want_lse else None
    ai = pl.program_id(1)
    k = jnp.concatenate([r[...] for r in k_refs], axis=0).astype(BF16)
    v = jnp.concatenate([r[...] for r in v_refs], axis=0).astype(BF16)
    W = (n_prev + 1) * Ta
    col = lax.broadcasted_iota(jnp.int32, (1, W), 1)
    col_valid = (ai - n_prev) * Ta + col >= 0
    q = q_ref[...]
    outs, lses = [], []
    for h in range(H):
        hk = 0 if shared_kv else h
        qh = (q[:, h * HEAD_DIM:(h + 1) * HEAD_DIM] * (HEAD_DIM ** -0.5)).astype(BF16)
        s = _dot_nt(qh, k[:, hk * HEAD_DIM:(hk + 1) * HEAD_DIM]) + bt_ref[h]
        s = jnp.where(col_valid, s, NEG_INF)
        m = jnp.max(s, axis=-1, keepdims=True)
        p = jnp.exp(s - m)
        l = jnp.sum(p, axis=-1, keepdims=True)
        outs.append(_dot(p.astype(BF16), v[:, hk * HEAD_DIM:(hk + 1) * HEAD_DIM]) / l)
        if want_lse:
            lses.append(jnp.broadcast_to(m + jnp.log(l), (Ta, HEAD_DIM)))
    o_ref[...] = jnp.concatenate(outs, axis=1)
    if want_lse:
        lse_ref[...] = jnp.concatenate(lses, axis=1)


def band_attention(q, k, v, btab, Ta, n_prev, want_lse, widths=None, col_blocks=(0, 0, 0)):
    R, A = q.shape[:2]
    QW, KW = widths if widths else (q.shape[2], k.shape[2])
    H = QW // HEAD_DIM
    shared_kv = KW == HEAD_DIM
    nA = A // Ta
    qc, kc, vc = col_blocks

    def prev_map(p, c):
        return lambda r, a: (r, jnp.maximum(a - n_prev + p, 0), c)

    cur = lambda r, a: (r, a, 0)
    k_specs = ([pl.BlockSpec((None, Ta, KW), prev_map(p, kc)) for p in range(n_prev)]
               + [pl.BlockSpec((None, Ta, KW), lambda r, a: (r, a, kc))])
    v_specs = ([pl.BlockSpec((None, Ta, KW), prev_map(p, vc)) for p in range(n_prev)]
               + [pl.BlockSpec((None, Ta, KW), lambda r, a: (r, a, vc))])
    out_shape = [jax.ShapeDtypeStruct((R, A, QW), F32)]
    out_specs = [pl.BlockSpec((None, Ta, QW), cur)]
    if want_lse:
        out_shape.append(jax.ShapeDtypeStruct((R, A, QW), F32))
        out_specs.append(pl.BlockSpec((None, Ta, QW), cur))
    res = pl.pallas_call(
        functools.partial(_band_kernel, n_prev=n_prev, Ta=Ta, H=H, shared_kv=shared_kv, want_lse=want_lse),
        grid=(R, nA),
        in_specs=[pl.BlockSpec((None, Ta, QW), lambda r, a: (r, a, qc))] + k_specs + v_specs
                 + [pl.BlockSpec(btab.shape, lambda r, a: (0, 0, 0))],
        out_specs=out_specs,
        out_shape=out_shape,
        compiler_params=_cparams(("parallel", "arbitrary")),
    )(q, *([k] * (n_prev + 1)), *([v] * (n_prev + 1)), btab)
    return res if want_lse else res[0]


def _gelu_tanh(x):
    return 0.5 * x * (1.0 + jnp.tanh(math.sqrt(2.0 / math.pi) * (x + 0.044715 * (x * x * x))))


def _compress_kernel(ks_ref, vs_ref, pk_ref, pv_ref, w1k_ref, w2k_ref, w1v_ref, w2vt_ref, kc_ref, vct_ref):
    def hidden(src, pos, w1):
        x = src[...]
        n = x.shape[0]
        first = _dot((x + pos[0:1, :]).astype(BF16), w1[0])
        second = _dot((x + pos[1:2, :]).astype(BF16), w1[1])
        return _gelu_tanh(first + pltpu.roll(second, n - 1, axis=0)).astype(BF16)

    kc_ref[...] = _dot(hidden(ks_ref, pk_ref, w1k_ref), w2k_ref[...]).astype(BF16)
    vct_ref[...] = _dot_nt(w2vt_ref[...], hidden(vs_ref, pv_ref, w1v_ref)).astype(BF16)


def nsa_compress(k_chunks, v_chunks, pos_k, pos_v, w1k, w2k, w1v, w2v_t):
    n = k_chunks.shape[0]
    return pl.pallas_call(
        _compress_kernel,
        out_shape=[jax.ShapeDtypeStruct((n, HEAD_DIM), BF16), jax.ShapeDtypeStruct((HEAD_DIM, n), BF16)],
        compiler_params=_cparams(None),
    )(k_chunks, v_chunks, pos_k, pos_v, w1k, w2k, w1v, w2v_t)


def _cmp_topk_kernel(nq_ref, ks_ref, vs_ref, kc_ref, vct_ref, ovt_ref, eye_ref,
                     ocmp_ref, qaug_ref, st_ref, kaug_ref, vst_ref, *, Tq, n_cmp, n_sel, nbg):
    q0 = pl.program_id(0) * Tq
    t = q0 + lax.broadcasted_iota(jnp.int32, (n_cmp, Tq), 1)
    cmp_end = NSA_CMP_STRIDE * lax.broadcasted_iota(jnp.int32, (n_cmp, Tq), 0) + (NSA_CMP_LEN - 1)
    c_mask = cmp_end <= t
    kc = kc_ref[...]
    vct = vct_ref[...]
    q = nq_ref[...]
    eye = eye_ref[...]
    ks = ks_ref[...]
    ks_t = _dot_nt(eye, ks)
    tok = q0 + lax.broadcasted_iota(jnp.int32, (Tq, nbg), 0)
    blk = jnp.bitwise_and(jnp.right_shift(tok, NSA_SEL_BLOCK.bit_length() - 1), nbg - 1)
    onehot = jnp.where(blk == lax.broadcasted_iota(jnp.int32, (Tq, nbg), 1), 1.0, 0.0).astype(BF16)
    stat_cols = jnp.where(lax.broadcasted_iota(jnp.int32, (Tq, STAT_ROWS), 1) < 3, 1.0, 0.0).astype(BF16)
    kaug_ref[...] = jnp.concatenate([ks, onehot, stat_cols], axis=1)
    ones_row = jnp.where(lax.broadcasted_iota(jnp.int32, (STAT_ROWS, Tq), 0) < 1, 1.0, 0.0).astype(BF16)
    vst_ref[...] = jnp.concatenate([_dot_nt(eye, vs_ref[...]).astype(BF16), ones_row], axis=0)
    psum = jnp.zeros((n_cmp, Tq), F32)
    outs, q_ts, qn2, l_self = [], [], [], []
    for h in range(NSA_HEADS):
        qh = (q[:, h * HEAD_DIM:(h + 1) * HEAD_DIM] * (HEAD_DIM ** -0.5 * LOG2E)).astype(BF16)
        q_t = _dot_nt(eye, qh)
        q_ts.append(q_t.astype(BF16))
        qn2.append(jnp.sum(q_t * q_t, axis=0, keepdims=True))
        l_self.append(jnp.sum(q_t * ks_t, axis=0, keepdims=True))
        s = jnp.where(c_mask, _dot(kc, q_ts[h]), NEG_INF)
        p = jnp.exp2(s - jnp.max(s, axis=0, keepdims=True))
        p = jnp.where(c_mask, p * (1.0 / jnp.sum(p, axis=0, keepdims=True)), 0.0)
        psum = psum + p
        outs.append(_dot(vct, p.astype(BF16)).T)
    ocmp_ref[...] = jnp.concatenate(outs, axis=1)
    st_ref[0] = jnp.concatenate(qn2, axis=0)
    st_ref[1] = jnp.broadcast_to(jnp.sum(ks_t * ks_t, axis=0, keepdims=True), (NSA_HEADS, Tq))
    st_ref[2] = jnp.concatenate(l_self, axis=0)
    hi = psum.astype(BF16)
    lo = (psum - hi.astype(F32)).astype(BF16)
    imp = _dot(ovt_ref[...], hi) + _dot(ovt_ref[...], lo)
    t = q0 + lax.broadcasted_iota(jnp.int32, (n_sel, Tq), 1)
    j = lax.broadcasted_iota(jnp.int32, (n_sel, Tq), 0)
    start = j * NSA_SEL_BLOCK
    cur_blk = (start <= t) & (t < start + NSA_SEL_BLOCK)
    prev_blk = (start + NSA_SEL_BLOCK <= t) & (t < start + 2 * NSA_SEL_BLOCK)
    forced = (j == 0) | cur_blk | prev_blk
    score = jnp.where(forced, NSA_FORCE_SCORE, jnp.where(start <= t, imp, -NSA_FORCE_SCORE))
    jf = j.astype(F32)
    sel = jnp.zeros((n_sel, Tq), jnp.bool_)
    for _ in range(min(NSA_TOPK, n_sel)):
        mx = jnp.max(score, axis=0, keepdims=True)
        first = jnp.min(jnp.where(score == mx, jf, float(n_sel)), axis=0, keepdims=True)
        hit = jf == first
        sel = sel | hit
        score = jnp.where(hit, -3e38, score)
    sel_bias_t = jnp.where(sel, 0.0, SEL_NEG).astype(BF16)
    for h in range(NSA_HEADS):
        for g in range(n_sel // nbg):
            qaug_ref[g, h] = jnp.concatenate([q_ts[h], sel_bias_t[g * nbg:(g + 1) * nbg]], axis=0)


def nsa_cmp_topk(y, nq_block, k_sel, v_sel, kc, vc_t, overlap_t, Tq, nbg):
    S = y.shape[0]
    n_cmp = kc.shape[0]
    n_sel = S // NSA_SEL_BLOCK
    G = n_sel // nbg
    assert nbg & (nbg - 1) == 0 and NSA_SEL_BLOCK & (NSA_SEL_BLOCK - 1) == 0
    const = lambda a: pl.BlockSpec(a.shape, lambda i: (0,) * a.ndim)
    rows = lambda w: pl.BlockSpec((Tq, w), lambda i: (i, 0))
    eye = jnp.eye(HEAD_DIM, dtype=BF16)
    kw = HEAD_DIM + nbg + STAT_ROWS
    return pl.pallas_call(
        functools.partial(_cmp_topk_kernel, Tq=Tq, n_cmp=n_cmp, n_sel=n_sel, nbg=nbg),
        grid=(S // Tq,),
        in_specs=[pl.BlockSpec((Tq, NSA_W), lambda i: (i, nq_block)), rows(HEAD_DIM), rows(HEAD_DIM),
                  const(kc), const(vc_t), const(overlap_t), const(eye)],
        out_specs=[rows(NSA_W),
                   pl.BlockSpec((G, NSA_HEADS, HEAD_DIM + nbg, Tq), lambda i: (0, 0, 0, i)),
                   pl.BlockSpec((3, NSA_HEADS, Tq), lambda i: (0, 0, i)),
                   rows(kw),
                   pl.BlockSpec((HEAD_DIM + STAT_ROWS, Tq), lambda i: (0, i))],
        out_shape=[jax.ShapeDtypeStruct((S, NSA_W), F32),
                   jax.ShapeDtypeStruct((G, NSA_HEADS, HEAD_DIM + nbg, S), BF16),
                   jax.ShapeDtypeStruct((3, NSA_HEADS, S), F32),
                   jax.ShapeDtypeStruct((S, kw), BF16),
                   jax.ShapeDtypeStruct((HEAD_DIM + STAT_ROWS, S), BF16)],
        compiler_params=_cparams(("parallel",)),
    )(y, k_sel, v_sel, kc, vc_t, overlap_t, eye)


def _outproj_even_kernel(oa_ref, ob_ref, wa_ref, wb_ref, g_ref, h_ref, out_ref):
    m = _dot(oa_ref[...], wa_ref[...]) + _dot(ob_ref[...], wb_ref[...])
    out_ref[...] = h_ref[...] + _rms(m, g_ref[...])


def outproj_even(oa, ob, wa, wb, g, h, tm):
    S, D = h.shape
    rows = lambda w: pl.BlockSpec((tm, w), lambda i: (i, 0))
    full = lambda a: pl.BlockSpec(a.shape, lambda i: (0,) * a.ndim)
    return pl.pallas_call(
        _outproj_even_kernel,
        grid=(S // tm,),
        in_specs=[rows(oa.shape[1]), rows(ob.shape[1]), full(wa), full(wb), full(g), rows(D)],
        out_specs=rows(D),
        out_shape=jax.ShapeDtypeStruct((S, D), F32),
        compiler_params=_cparams(("parallel",)),
    )(oa, ob, wa, wb, g, h)


def _outproj_odd_kernel(o0_ref, o1_ref, o2_ref, l0_ref, l1_ref, l2_ref, oc_ref, os_ref, ow_ref,
                        gc_ref, gs_ref, gw_ref, wd_ref, wn_ref, g_ref, h_ref, out_ref):
    l0, l1, l2 = l0_ref[...], l1_ref[...], l2_ref[...]
    mx = jnp.maximum(jnp.maximum(l0, l1), l2)
    e0, e1, e2 = jnp.exp(l0 - mx), jnp.exp(l1 - mx), jnp.exp(l2 - mx)
    o_dil = (e0 * o0_ref[...] + e1 * o1_ref[...] + e2 * o2_ref[...]) / (e0 + e1 + e2)
    o_nsa = (jax.nn.sigmoid(gc_ref[...]) * oc_ref[...] + jax.nn.sigmoid(gs_ref[...]) * os_ref[...].astype(F32)
             + jax.nn.sigmoid(gw_ref[...]) * ow_ref[...])
    m = _dot(o_dil.astype(BF16), wd_ref[...]) + _dot(o_nsa.astype(BF16), wn_ref[...])
    out_ref[...] = h_ref[...] + _rms(m, g_ref[...])


def outproj_odd(o_dil, lse_dil, o_cmp, o_sel, o_win, y, gate_block, wd, wn, g, h, tm):
    S, D = h.shape
    rows = lambda w: pl.BlockSpec((tm, w), lambda i: (i, 0))
    full = lambda a: pl.BlockSpec(a.shape, lambda i: (0,) * a.ndim)
    gate = lambda b: pl.BlockSpec((tm, NSA_W), lambda i: (i, gate_block + b))
    return pl.pallas_call(
        _outproj_odd_kernel,
        grid=(S // tm,),
        in_specs=[rows(DIL_GW)] * 6 + [rows(NSA_W)] * 3 + [gate(0), gate(1), gate(2), full(wd), full(wn), full(g),
                                                            rows(D)],
        out_specs=rows(D),
        out_shape=jax.ShapeDtypeStruct((S, D), F32),
        compiler_params=_cparams(("parallel",)),
    )(*o_dil, *lse_dil, o_cmp, o_sel, o_win, y, y, y, wd, wn, g, h)


def _mlp_ple_kernel(h_ref, g1_ref, wu_ref, wd_ref, g2_ref, g3_ref, wg_ref, p_ref, wp_ref, out_ref,
                    hn_ref, acc_ref):
    j = pl.program_id(1)

    @pl.when(j == 0)
    def _():
        hn_ref[...] = _rms(h_ref[...], g1_ref[...]).astype(BF16)
        acc_ref[...] = jnp.zeros(acc_ref.shape, F32)

    u = jnp.maximum(_dot(hn_ref[...], wu_ref[...]), 0.0)
    acc_ref[...] += _dot((u * u).astype(BF16), wd_ref[...])

    @pl.when(j == pl.num_programs(1) - 1)
    def _():
        h2 = h_ref[...] + _rms(acc_ref[...], g2_ref[...])
        gate = jax.nn.sigmoid(_dot(_rms(h2, g3_ref[...]).astype(BF16), wg_ref[...]))
        out_ref[...] = h2 + gate * _dot(p_ref[...].astype(BF16), wp_ref[...])


def mlp_ple(h, g1, wu, wd, g2, g3, wg, p, wp, tm, tf):
    S, D = h.shape
    FF = wu.shape[1]
    PD = p.shape[1]
    const = lambda a: pl.BlockSpec(a.shape, lambda i, j: (0,) * a.ndim)
    return pl.pallas_call(
        _mlp_ple_kernel,
        grid=(S // tm, FF // tf),
        in_specs=[pl.BlockSpec((tm, D), lambda i, j: (i, 0)), const(g1),
                  pl.BlockSpec((D, tf), lambda i, j: (0, j)),
                  pl.BlockSpec((tf, D), lambda i, j: (j, 0)),
                  const(g2), const(g3), const(wg),
                  pl.BlockSpec((tm, PD), lambda i, j: (i, 0)), const(wp)],
        out_specs=pl.BlockSpec((tm, D), lambda i, j: (i, 0)),
        out_shape=jax.ShapeDtypeStruct((S, D), F32),
        scratch_shapes=[pltpu.VMEM((tm, D), BF16), pltpu.VMEM((tm, D), F32)],
        compiler_params=_cparams(("parallel", "arbitrary")),
    )(h, g1, wu, wd, g2, g3, wg, p, wp)


def _t5_bucket_of(dist):
    n = jnp.maximum(dist, 0)
    max_exact = T5_BUCKETS // 2
    ratio = jnp.log(jnp.maximum(n, 1).astype(F32) / max_exact) / math.log(T5_MAX_DIST / max_exact)
    large = jnp.minimum(max_exact + (ratio * (T5_BUCKETS - max_exact)).astype(jnp.int32), T5_BUCKETS - 1)
    return jnp.where(n < max_exact, n, large)


def _bias_of_dist(bias, dist):
    bucket = _t5_bucket_of(dist)[None]
    out = jnp.zeros((bias.shape[1],) + dist.shape, F32)
    for b in range(T5_BUCKETS):
        out = jnp.where(bucket == b, bias[b].reshape((-1,) + (1,) * dist.ndim), out)
    return out


def _causal_tables(T):
    k = jnp.arange(T)[:, None]
    q = jnp.arange(T)[None, :]
    diag = jnp.where(q >= k, 0.0, NEG_INF).astype(F32)
    return jnp.stack([diag, jnp.zeros((T, T), F32)])[:, None]


def _t5_delta_tables(bias, T):
    n_delta = -(-(T5_MAX_DIST - 1) // T) + 2
    k = jnp.arange(T)[None, :, None]
    q = jnp.arange(T)[None, None, :]
    dist = jnp.arange(n_delta)[:, None, None] * T + q - k
    val = (_bias_of_dist(bias, dist) - bias[T5_BUCKETS - 1].reshape(-1, 1, 1, 1)) * LOG2E
    return jnp.transpose(jnp.where(dist[None] >= 0, val, NEG_INF), (1, 0, 2, 3))


def _band_table(bias, Ta, n_prev, max_rel, stride, inclusive):
    P = n_prev * Ta
    i = jnp.arange(Ta)[:, None]
    c = jnp.arange(P + Ta)[None, :]
    rel = i + P - c
    ok = (rel >= 0) & ((rel <= max_rel) if inclusive else (rel < max_rel))
    return jnp.where(ok[None], _bias_of_dist(bias, rel * stride), NEG_INF)


def _tile(S, pref):
    t = min(pref, S)
    assert S % t == 0
    return t


def even_mixer_core(h, g_pre, w_in, cos, sin, w_uq, q_norm, w_ukv, kv_norm, forget_bias, T):
    S = h.shape[0]
    scale = MLA_QK_DIM ** -0.5 * LOG2E
    half = MLA_ROPE_DIM // 2
    cos_t, sin_t = jnp.transpose(cos), jnp.transpose(sin)
    ones = jnp.ones((MLA_NOPE_DIM, S), F32)
    cq = jnp.concatenate([ones, cos_t, cos_t], axis=0) * scale
    sq = jnp.concatenate([0.0 * ones, -sin_t, sin_t], axis=0) * scale
    ck = jnp.concatenate([cos, cos], axis=1)
    sk = jnp.concatenate([-sin, sin], axis=1)
    wq = jnp.transpose(w_uq.reshape(MLA_Q_RANK, MLA_HEADS, MLA_QK_DIM), (1, 2, 0))
    swap = np.concatenate([np.arange(MLA_NOPE_DIM), MLA_NOPE_DIM + half + np.arange(half),
                           MLA_NOPE_DIM + np.arange(half)])
    wqs = wq[:, swap, :]
    wkv = jnp.transpose(w_ukv.reshape(MLA_KV_RANK, MLA_HEADS, MLA_NOPE_DIM + MLA_V_DIM), (1, 0, 2))
    wk = jnp.pad(wkv[:, :, :MLA_NOPE_DIM], ((0, 0), (0, 0), (0, MLA_ROPE_DIM)))
    wv = jnp.transpose(wkv[:, :, MLA_NOPE_DIM:], (0, 2, 1))
    e96 = jnp.pad(jnp.eye(MLA_ROPE_DIM, dtype=F32), ((0, 0), (MLA_NOPE_DIM, 0)))
    qm, km, vm, fq, fk, fv, logf, st = even_prep(
        h, g_pre, w_in, wq.astype(BF16), wqs.astype(BF16), wk.astype(BF16), e96.astype(BF16), wv.astype(BF16),
        jnp.eye(MLA_QK_DIM, dtype=BF16), q_norm[None, :], kv_norm[None, :], forget_bias[None, :],
        cq, sq, ck, sk, _tile(S, 512))
    neg_f = cumsum_split(jnp.transpose(logf))
    hps = MLA_HEADS
    causal = _causal_tables(T)
    r_mla, fast_mla = _reference_rows(st[0], st[1], st[2], 0.0, 0.0, 0.0, T, hps)
    o_mla = flash_causal(qm[None], r_mla, km, vm, causal, fast_mla, T, hps)
    r_fox, fast_fox = _reference_rows(st[3], st[4], st[5], jnp.sum(neg_f.astype(F32), axis=0), 0.0, 0.0, T, hps)
    fk_aug = jnp.concatenate([fk, jnp.transpose(neg_f, (1, 2, 0)),
                              jnp.zeros((FOX_HEADS, S, FOX_PAD - FOX_SPLIT), BF16), _stat_cols(FOX_HEADS, S)], axis=2)
    o_fox = flash_causal(fq[None], r_fox, fk_aug, fv, causal, fast_fox, T, hps)
    return o_mla, o_fox


def odd_mixer_core(y, t5_bias, pos_k, pos_v, w1k, w2k, w1v, w2v, T):
    S = y.shape[0]
    G = len(DIL_GROUPS)
    Ta = 256
    o_dil, lse_dil = [], []
    for g, (w, d) in enumerate(DIL_GROUPS):
        ta = _tile(S // d, Ta)
        assert w // d <= ta
        btab = _band_table(t5_bias[:, g * DIL_HEADS:(g + 1) * DIL_HEADS], ta, 1, w // d, d, True)
        if d == 1:
            o, lse = band_attention(y[None], y[None], y[None], btab, ta, 1, True,
                                    widths=(DIL_GW, DIL_GW), col_blocks=(g, G + g, 2 * G + g))
        else:
            def deint(a):
                return jnp.transpose(a.reshape(S // d, d, DIL_GW), (1, 0, 2))
            q, k, v = (deint(y[:, (b * G + g) * DIL_GW:(b * G + g + 1) * DIL_GW]) for b in range(3))
            o, lse = band_attention(q, k, v, btab, ta, 1, True)
        o_dil.append(jnp.transpose(o, (1, 0, 2)).reshape(S, DIL_GW))
        lse_dil.append(jnp.transpose(lse, (1, 0, 2)).reshape(S, DIL_GW))
    nq_block = 3 * G
    gate_block = nq_block + 1
    base = (gate_block + 3) * NSA_W
    k_cmp, v_cmp, k_sel, v_sel, k_win, v_win = (y[:, base + i * HEAD_DIM:base + (i + 1) * HEAD_DIM] for i in range(6))
    bias_nsa = t5_bias[:, G * DIL_HEADS:]
    n_chunk = S // NSA_CMP_STRIDE
    cw = NSA_CMP_STRIDE * HEAD_DIM
    kc, vc = nsa_compress(k_cmp.reshape(n_chunk, cw), v_cmp.reshape(n_chunk, cw),
                          pos_k.reshape(2, cw), pos_v.reshape(2, cw),
                          w1k.reshape(2, cw, -1).astype(BF16), w2k.astype(BF16),
                          w1v.reshape(2, cw, -1).astype(BF16), jnp.transpose(w2v).astype(BF16))
    n_sel = S // NSA_SEL_BLOCK
    ci = np.arange(n_chunk)[None, :] * NSA_CMP_STRIDE
    sj = np.arange(n_sel)[:, None] * NSA_SEL_BLOCK
    overlap_t = jnp.asarray(((ci < sj + NSA_SEL_BLOCK) & (ci + NSA_CMP_LEN > sj)).astype(np.float32), BF16)
    nbg = min(n_sel, NSA_SEL_GROUP_BLOCKS)
    o_cmp, qaug, st, kaug, v_sel_t = nsa_cmp_topk(y, nq_block, k_sel.astype(BF16), v_sel.astype(BF16), kc, vc,
                                                  overlap_t, _tile(S, 256), nbg)
    delta = (bias_nsa - bias_nsa[T5_BUCKETS - 1]) * LOG2E
    r_sel, fast_sel = _reference_rows(st[0], st[1, :1], st[2], 0.0, jnp.max(delta, axis=0)[:, None],
                                      delta[0][:, None], T, NSA_HEADS)
    o_sel = flash_causal(qaug, r_sel, kaug[None], v_sel_t[None], _t5_delta_tables(bias_nsa, T), fast_sel, T,
                         hps=NSA_HEADS, key_group_tokens=nbg * NSA_SEL_BLOCK)
    Tw = _tile(S, NSA_WINDOW)
    n_prev = NSA_WINDOW // Tw
    wtab = _band_table(bias_nsa, Tw, n_prev, NSA_WINDOW, 1, False)
    o_win = band_attention(y[None], k_win[None], v_win[None], wtab, Tw, n_prev, False,
                           widths=(NSA_W, HEAD_DIM), col_blocks=(nq_block, 0, 0))[0]
    return o_dil, lse_dil, o_cmp, o_sel, o_win, gate_block


def _pad_cols(w, n):
    return jnp.pad(w, ((0, 0), (0, n - w.shape[1])))


def _even_w_in(w):
    cq, ckv, kr, fq, fk, fv, fl = jnp.split(w, np.cumsum([256, 128, 32, 512, 512, 512])[:], axis=1)
    half = MLA_ROPE_DIM // 2
    kr_sw = jnp.concatenate([kr[:, half:], kr[:, :half]], axis=1)
    head = _pad_cols(jnp.concatenate([cq, ckv, kr, kr_sw, fl], axis=1), 512)
    return jnp.concatenate([head, fq, fk, fv], axis=1).astype(BF16)


def _odd_w_in(w):
    n_main = 3 * 3 * DIL_GW + NSA_W
    main, six = w[:, :n_main], w[:, n_main:n_main + 6 * HEAD_DIM]
    gl = w[:, n_main + 6 * HEAD_DIM:]
    gl = jnp.transpose(gl.reshape(-1, NSA_HEADS, 3), (0, 2, 1))
    gl = jnp.repeat(gl[..., None], HEAD_DIM, axis=-1).reshape(w.shape[0], 3 * NSA_W)
    return _pad_cols(jnp.concatenate([main, gl, six], axis=1), ODD_IN_PAD).astype(BF16)


def _trunk(x, p, positions, t5_bias, ev_w_in, ev_q_norm, ev_w_uq, ev_kv_norm, ev_w_ukv, ev_forget_bias,
           ev_w_out, od_w_in, od_cmp_pos_k, od_cmp_pos_v, od_cmp_w1_k, od_cmp_w2_k, od_cmp_w1_v, od_cmp_w2_v,
           od_w_out, norm_mix_pre, norm_mix_post, norm_mlp_pre, norm_mlp_post, w_mlp_up, w_mlp_down,
           ple_norm, w_ple_gate, w_ple_proj):
    S, D = x.shape
    depth = p.shape[0]
    T = _tile(S, 512)
    tm = _tile(S, 512)
    inv_freq = ROPE_THETA ** (-jnp.arange(0, MLA_ROPE_DIM, 2, dtype=F32) / MLA_ROPE_DIM)
    angles = positions.astype(F32)[:, None] * inv_freq
    cos, sin = jnp.cos(angles), jnp.sin(angles)
    h = x
    for i in range(depth):
        j = i // 2
        if i % 2 == 0:
            o_mla, o_fox = even_mixer_core(h, norm_mix_pre[i][None], _even_w_in(ev_w_in[j]), cos, sin, ev_w_uq[j],
                                           ev_q_norm[j], ev_w_ukv[j], ev_kv_norm[j], ev_forget_bias[j], T)
            wo = ev_w_out[j].astype(BF16)
            na = MLA_HEADS * MLA_V_DIM
            h = outproj_even(o_mla, o_fox, wo[:na], wo[na:], norm_mix_post[i][None], h, tm)
        else:
            y = norm_matmul(h, norm_mix_pre[i][None], _odd_w_in(od_w_in[j]), _tile(S, 256), ODD_IN_PAD)
            o_dil, lse_dil, o_cmp, o_sel, o_win, gate_block = odd_mixer_core(
                y, t5_bias, od_cmp_pos_k[j], od_cmp_pos_v[j], od_cmp_w1_k[j], od_cmp_w2_k[j],
                od_cmp_w1_v[j], od_cmp_w2_v[j], T)
            wo = od_w_out[j].astype(BF16)
            h = outproj_odd(o_dil, lse_dil, o_cmp, o_sel, o_win, y, gate_block, wo[:DIL_GW], wo[DIL_GW:],
                            norm_mix_post[i][None], h, tm)
        h = mlp_ple(h, norm_mlp_pre[i][None], w_mlp_up[i].astype(BF16), w_mlp_down[i].astype(BF16),
                    norm_mlp_post[i][None], ple_norm[i][None], w_ple_gate[i].astype(BF16), p[i],
                    w_ple_proj[i].astype(BF16), _tile(S, 1024), 512)
    return h


def kernel(x, p, positions, t5_bias, ev_w_in, ev_q_norm, ev_w_uq, ev_kv_norm, ev_w_ukv, ev_forget_bias, ev_w_out, od_w_in, od_cmp_pos_k, od_cmp_pos_v, od_cmp_w1_k, od_cmp_w2_k, od_cmp_w1_v, od_cmp_w2_v, od_w_out, norm_mix_pre, norm_mix_post, norm_mlp_pre, norm_mlp_post, w_mlp_up, w_mlp_down, ple_norm, w_ple_gate, w_ple_proj):
    params = (t5_bias, ev_w_in, ev_q_norm, ev_w_uq, ev_kv_norm, ev_w_ukv, ev_forget_bias, ev_w_out, od_w_in,
              od_cmp_pos_k, od_cmp_pos_v, od_cmp_w1_k, od_cmp_w2_k, od_cmp_w1_v, od_cmp_w2_v, od_w_out,
              norm_mix_pre, norm_mix_post, norm_mlp_pre, norm_mlp_post, w_mlp_up, w_mlp_down, ple_norm,
              w_ple_gate, w_ple_proj)
    outs = [_trunk(x[b], p[:, b], positions[b], *params) for b in range(x.shape[0])]
    return jnp.stack(outs).astype(x.dtype)
```

```python
import functools
import math

import numpy as np
import jax
import jax.numpy as jnp
from jax import lax
from jax.experimental import pallas as pl
from jax.experimental.pallas import tpu as pltpu

F32 = jnp.float32
BF16 = jnp.bfloat16

HEAD_DIM = 64
RMS_EPS = 1e-6
NEG_INF = -1e30
SEL_NEG = -(2.0 ** 99)

MLA_HEADS = 8
MLA_NOPE_DIM = 64
MLA_ROPE_DIM = 32
MLA_V_DIM = 64
MLA_Q_RANK = 256
MLA_KV_RANK = 128
MLA_QK_DIM = MLA_NOPE_DIM + MLA_ROPE_DIM
ROPE_THETA = 10000.0
FOX_HEADS = 8
FOX_SPLIT = 3
FOX_PAD = 16
LOG2E = math.log2(math.e)
STAT_ROWS = 16
FAST_OVER = 64.0
FAST_MARGIN = 150.0
P_CHUNK = 32
DIL_GROUPS = ((128, 1), (512, 4), (2048, 16))
DIL_HEADS = 4
DIL_GW = DIL_HEADS * HEAD_DIM
NSA_HEADS = 4
NSA_W = NSA_HEADS * HEAD_DIM
NSA_CMP_LEN = 32
NSA_CMP_STRIDE = 16
NSA_SEL_BLOCK = 64
NSA_TOPK = 16
NSA_WINDOW = 512
NSA_FORCE_SCORE = 1e9
NSA_SEL_GROUP_BLOCKS = 128
T5_BUCKETS = 32
T5_MAX_DIST = 2048

EVEN_IN_PAD = 2048
ODD_IN_PAD = 3840

VMEM_LIMIT_BYTES = 56 * 1024 * 1024


def _cparams(sem):
    return pltpu.CompilerParams(dimension_semantics=sem, vmem_limit_bytes=VMEM_LIMIT_BYTES)


def _rms(x, g):
    return x * lax.rsqrt(jnp.mean(x * x, axis=-1, keepdims=True) + RMS_EPS) * g


def _dot(a, b):
    return jnp.dot(a, b, preferred_element_type=F32)


def _dot_nt(a, b):
    return lax.dot_general(a, b, (((1,), (1,)), ((), ())), preferred_element_type=F32)


LANES = 128


def _odd_inproj_kernel(h_ref, g_ref, w_ref, y_ref, *rest, deint):
    out_refs, scr = rest[:-1], rest[-1]
    y = _dot(_rms(h_ref[...], g_ref[...]).astype(BF16), w_ref[...])
    y_ref[...] = y
    tm = y.shape[0]
    for out, (col, width, d) in zip(out_refs, deint):
        n = width // LANES
        for c in range(n):
            scr[c] = y[:, col + c * LANES:col + (c + 1) * LANES]
        for r in range(d):
            out[r] = jnp.concatenate([scr[c, pl.ds(r, tm // d, stride=d), :] for c in range(n)], axis=1)


def odd_inproj(h, g, w, tm, deint):
    S, D = h.shape
    N = w.shape[1]
    out_shape = [jax.ShapeDtypeStruct((S, N), F32)]
    out_specs = [pl.BlockSpec((tm, N), lambda i: (i, 0))]
    for _, width, d in deint:
        out_shape.append(jax.ShapeDtypeStruct((d, S // d, width), F32))
        out_specs.append(pl.BlockSpec((d, tm // d, width), lambda i: (0, i, 0)))
    return pl.pallas_call(
        functools.partial(_odd_inproj_kernel, deint=deint),
        grid=(S // tm,),
        in_specs=[pl.BlockSpec((tm, D), lambda i: (i, 0)),
                  pl.BlockSpec((1, D), lambda i: (0, 0)),
                  pl.BlockSpec((D, N), lambda i: (0, 0))],
        out_specs=out_specs,
        out_shape=out_shape,
        scratch_shapes=[pltpu.VMEM((max(wd for _, wd, _ in deint) // LANES, tm, LANES), F32)],
        compiler_params=_cparams(("parallel",)),
    )(h, g, w)


def _even_prep_kernel(h_ref, g_ref, win_ref, wq_ref, wqs_ref, wk_ref, e_ref, wv_ref, eye_ref, qn_ref, kvn_ref,
                      fb_ref, cq_ref, sq_ref, ck_ref, sk_ref,
                      qm_ref, km_ref, vm_ref, fq_ref, fk_ref, fv_ref, lf_ref, st_ref):
    y = _dot(_rms(h_ref[...], g_ref[...]).astype(BF16), win_ref[...])
    tm = y.shape[0]
    c_q = y[:, 0:MLA_Q_RANK]
    c_kv = y[:, MLA_Q_RANK:MLA_Q_RANK + MLA_KV_RANK]
    o = MLA_Q_RANK + MLA_KV_RANK
    k_r = y[:, o:o + MLA_ROPE_DIM]
    k_rs = y[:, o + MLA_ROPE_DIM:o + 2 * MLA_ROPE_DIM]
    f_logit = y[:, o + 2 * MLA_ROPE_DIM:o + 2 * MLA_ROPE_DIM + FOX_HEADS]
    nq = _rms(c_q, qn_ref[...]).astype(BF16)
    nkv = _rms(c_kv, kvn_ref[...]).astype(BF16)
    k_rot = (k_r * ck_ref[...] + k_rs * sk_ref[...]).astype(BF16)
    k_rot96 = _dot(k_rot, e_ref[...])
    cq = cq_ref[...]
    sq = sq_ref[...]
    eye = eye_ref[...]
    row = lax.broadcasted_iota(jnp.int32, (STAT_ROWS, tm), 0)
    ones_row = jnp.where(row < 1, 1.0, 0.0).astype(BF16)
    col = lax.broadcasted_iota(jnp.int32, (tm, STAT_ROWS), 1)
    stat_cols = jnp.where(col < 3, 1.0, 0.0).astype(BF16)
    stats = [[] for _ in range(6)]

    def record(slot, q_t, k):
        q_t = q_t.astype(F32)
        k_t = _dot_nt(eye[:k.shape[1], :k.shape[1]], k)
        stats[slot].append(jnp.sum(q_t * q_t, axis=0, keepdims=True))
        stats[slot + 1].append(jnp.sum(k_t * k_t, axis=0, keepdims=True))
        stats[slot + 2].append(jnp.sum(q_t * k_t, axis=0, keepdims=True))

    for h in range(MLA_HEADS):
        q = (_dot_nt(wq_ref[h], nq) * cq + _dot_nt(wqs_ref[h], nq) * sq).astype(BF16)
        k = (_dot(nkv, wk_ref[h]) + k_rot96).astype(BF16)
        qm_ref[h] = q
        km_ref[h] = jnp.concatenate([k, stat_cols], axis=1)
        vm_ref[h] = jnp.concatenate([_dot_nt(wv_ref[h], nkv).astype(BF16), ones_row], axis=0)
        record(0, q, k)
    base = 512
    fw = FOX_HEADS * HEAD_DIM
    eye64 = eye[:HEAD_DIM, :HEAD_DIM]
    ones_rows = jnp.where(row < FOX_SPLIT, 1.0, 0.0).astype(BF16)
    for h in range(FOX_HEADS):
        lo = base + h * HEAD_DIM
        fq = _dot_nt(eye64, (y[:, lo:lo + HEAD_DIM] * (HEAD_DIM ** -0.5 * LOG2E)).astype(BF16)).astype(BF16)
        fk = y[:, lo + fw:lo + fw + HEAD_DIM].astype(BF16)
        fq_ref[h] = jnp.concatenate([fq, ones_rows], axis=0)
        fk_ref[h] = fk
        fv = y[:, lo + 2 * fw:lo + 2 * fw + HEAD_DIM].astype(BF16)
        fv_ref[h] = jnp.concatenate([_dot_nt(eye64, fv).astype(BF16), ones_row], axis=0)
        record(3, fq, fk)
    for i in range(6):
        st_ref[i] = jnp.concatenate(stats[i], axis=0)
    z = f_logit + fb_ref[...]
    lf_ref[...] = jnp.minimum(z, 0.0) - jnp.log1p(jnp.exp(-jnp.abs(z)))


def even_prep(h, g, w_in, wq, wqs, wk, e96, wv, eye, qn, kvn, fb, cq, sq, ck, sk, tm):
    S, D = h.shape
    full = lambda a: pl.BlockSpec(a.shape, lambda i: (0,) * a.ndim)
    rows = lambda w: pl.BlockSpec((tm, w), lambda i: (i, 0))
    cols = lambda w: pl.BlockSpec((w, tm), lambda i: (0, i))
    heads = lambda n, w: pl.BlockSpec((n, tm, w), lambda i: (0, i, 0))
    heads_t = lambda n, w: pl.BlockSpec((n, w, tm), lambda i: (0, 0, i))
    out_shape = [jax.ShapeDtypeStruct((MLA_HEADS, MLA_QK_DIM, S), BF16),
                 jax.ShapeDtypeStruct((MLA_HEADS, S, MLA_QK_DIM + STAT_ROWS), BF16),
                 jax.ShapeDtypeStruct((MLA_HEADS, MLA_V_DIM + STAT_ROWS, S), BF16),
                 jax.ShapeDtypeStruct((FOX_HEADS, HEAD_DIM + FOX_PAD, S), BF16),
                 jax.ShapeDtypeStruct((FOX_HEADS, S, HEAD_DIM), BF16),
                 jax.ShapeDtypeStruct((FOX_HEADS, HEAD_DIM + STAT_ROWS, S), BF16),
                 jax.ShapeDtypeStruct((S, FOX_HEADS), F32),
                 jax.ShapeDtypeStruct((6, MLA_HEADS, S), F32)]
    out_specs = [heads_t(MLA_HEADS, MLA_QK_DIM), heads(MLA_HEADS, MLA_QK_DIM + STAT_ROWS),
                 heads_t(MLA_HEADS, MLA_V_DIM + STAT_ROWS),
                 heads_t(FOX_HEADS, HEAD_DIM + FOX_PAD), heads(FOX_HEADS, HEAD_DIM),
                 heads_t(FOX_HEADS, HEAD_DIM + STAT_ROWS),
                 rows(FOX_HEADS), heads_t(6, MLA_HEADS)]
    return pl.pallas_call(
        _even_prep_kernel,
        grid=(S // tm,),
        in_specs=[rows(D), full(g), full(w_in), full(wq), full(wqs), full(wk), full(e96), full(wv), full(eye),
                  full(qn), full(kvn), full(fb), cols(MLA_QK_DIM), cols(MLA_QK_DIM), rows(MLA_ROPE_DIM),
                  rows(MLA_ROPE_DIM)],
        out_specs=out_specs,
        out_shape=out_shape,
        compiler_params=_cparams(("parallel",)),
    )(h, g, w_in, wq, wqs, wk, e96, wv, eye, qn, kvn, fb, cq, sq, ck, sk)


def _cumsum_split_kernel(x_ref, o_ref):
    x = x_ref[...]
    n = x.shape[1]
    lane = lax.broadcasted_iota(jnp.int32, x.shape, 1)
    shift = 1
    while shift < n:
        x = x + jnp.where(lane >= shift, pltpu.roll(x, shift, axis=1), 0.0)
        shift *= 2
    r = -x * LOG2E
    for i in range(FOX_SPLIT):
        part = r.astype(BF16)
        o_ref[i] = part
        r = r - part.astype(F32)


def cumsum_split(x):
    return pl.pallas_call(
        _cumsum_split_kernel,
        out_shape=jax.ShapeDtypeStruct((FOX_SPLIT,) + x.shape, BF16),
        compiler_params=_cparams(None),
    )(x)


def _flash_kernel(qt_ref, kt_ref, fast_ref, q_ref, r_ref, k_ref, v_ref, bt_ref, o_ref, m_scr, acc_scr, p_scr, *,
                  hps, shared_kv, n_delta, nt):
    step = pl.program_id(1)
    qi = qt_ref[step]
    ki = kt_ref[step]
    fast = fast_ref[pl.program_id(0) * nt + qi] != 0
    near = qi - ki < n_delta - 1

    @pl.when(ki == 0)
    def _():
        m_scr[...] = jnp.full(m_scr.shape, -3e38, F32)
        acc_scr[...] = jnp.zeros(acc_scr.shape, F32)

    hb = bt_ref.shape[1]
    Tk = k_ref.shape[1]
    Dv = v_ref.shape[1] - STAT_ROWS

    def logits(h, with_table):
        q = jnp.concatenate([q_ref[0, h], r_ref[h]], axis=0)
        s = _dot(k_ref[0 if shared_kv else h], q)
        return s + bt_ref[0, h if hb > 1 else 0] if with_table else s

    def store_p(h, s, m):
        for c in range(Tk // P_CHUNK):
            rows = slice(c * P_CHUNK, (c + 1) * P_CHUNK)
            x = s[rows] if m is None else s[rows] - m
            p_scr[h, rows, :] = jnp.exp2(x).astype(BF16)

    def fast_update(with_table):
        acc_prev = [acc_scr[h] for h in range(hps)]
        for h in range(hps):
            store_p(h, logits(h, with_table), None)
        for h in range(hps):
            acc_scr[h] = acc_prev[h] + _dot(v_ref[0 if shared_kv else h], p_scr[h])

    def safe_update(with_table):
        m_prev = [m_scr[h] for h in range(hps)]
        acc_prev = [acc_scr[h] for h in range(hps)]
        ss = [logits(h, with_table) for h in range(hps)]
        ms = [jnp.maximum(m_prev[h], jnp.max(ss[h], axis=0, keepdims=True)) for h in range(hps)]
        for h in range(hps):
            store_p(h, ss[h], ms[h])
        for h in range(hps):
            alpha = jnp.exp2(m_prev[h] - ms[h])
            acc_scr[h] = alpha * acc_prev[h] + _dot(v_ref[0 if shared_kv else h], p_scr[h])
            m_scr[h] = ms[h]

    for take_fast, update in ((True, fast_update), (False, safe_update)):
        for with_table in (True, False):
            cond = jnp.logical_and(fast == take_fast, near == with_table)
            pl.when(cond)(functools.partial(update, with_table))

    @pl.when(ki == qi)
    def _():
        outs = [(acc_scr[h, :Dv, :] / acc_scr[h, Dv:Dv + 1, :]).T for h in range(hps)]
        o_ref[...] = jnp.concatenate(outs, axis=1).astype(o_ref.dtype)


def flash_causal(qT, r_rows, k, vT, btab, fast, T, hps, key_group_tokens=None, out_dtype=BF16):
    G, H, Dq, S = qT.shape
    Dk = Dq + STAT_ROWS
    Hk, Dv, _ = vT.shape
    Dv -= STAT_ROWS
    shared_kv = Hk == 1
    n_delta, Hb = btab.shape[0], btab.shape[1]
    nt = S // T
    qt = np.array([i for i in range(nt) for _ in range(i + 1)], np.int32)
    kt = np.array([j for i in range(nt) for j in range(i + 1)], np.int32)
    kv_h = 1 if shared_kv else hps
    tiles_per_group = (key_group_tokens // T) if G > 1 else 1

    def q_map(g, s, qt, kt, fast):
        return (kt[s] // tiles_per_group if G > 1 else 0, g, 0, qt[s])

    def bt_map(g, s, qt, kt, fast):
        return (jnp.minimum(qt[s] - kt[s], n_delta - 1), g if Hb > hps else 0, 0, 0)

    in_specs = [pl.BlockSpec((1, hps, Dq, T), q_map),
                pl.BlockSpec((hps, STAT_ROWS, T), lambda g, s, qt, kt, fast: (g, 0, qt[s])),
                pl.BlockSpec((kv_h, T, Dk), lambda g, s, qt, kt, fast: (0 if shared_kv else g, kt[s], 0)),
                pl.BlockSpec((kv_h, Dv + STAT_ROWS, T), lambda g, s, qt, kt, fast: (0 if shared_kv else g, 0, kt[s])),
                pl.BlockSpec((1, min(Hb, hps), T, T), bt_map)]
    grid_spec = pltpu.PrefetchScalarGridSpec(
        num_scalar_prefetch=3,
        grid=(H // hps, len(qt)),
        in_specs=in_specs,
        out_specs=pl.BlockSpec((T, hps * Dv), lambda g, s, qt, kt, fast: (qt[s], g)),
        scratch_shapes=[pltpu.VMEM((hps, 1, T), F32), pltpu.VMEM((hps, Dv + STAT_ROWS, T), F32),
                        pltpu.VMEM((hps, T, T), BF16)],
    )
    return pl.pallas_call(
        functools.partial(_flash_kernel, hps=hps, shared_kv=shared_kv, n_delta=n_delta, nt=nt),
        grid_spec=grid_spec,
        out_shape=jax.ShapeDtypeStruct((S, H * Dv), out_dtype),
        compiler_params=_cparams(("parallel", "arbitrary")),
    )(jnp.asarray(qt), jnp.asarray(kt), fast, qT, r_rows, k, vT, btab)


def _reference_rows(qn2, kn2, l_self, extra, bias_max, bias_self, T, hps):
    H, S = qn2.shape
    kmax = jnp.sqrt(jnp.max(kn2, axis=1, keepdims=True))
    bound = jnp.sqrt(qn2) * kmax + bias_max + extra
    ok = bound - (l_self + bias_self + extra) <= FAST_MARGIN
    fast = jnp.all(ok.reshape(H // hps, hps, S // T, T), axis=(1, 3)).astype(jnp.int32).reshape(-1)
    r = FAST_OVER - bound
    parts = []
    for _ in range(3):
        part = r.astype(BF16)
        parts.append(part)
        r = r - part.astype(F32)
    rows = jnp.stack(parts + [jnp.zeros_like(parts[0])] * (STAT_ROWS - 3), axis=1)
    return rows, fast


def _stat_cols(Hk, S):
    return jnp.broadcast_to((jnp.arange(STAT_ROWS) < 3).astype(BF16), (Hk, S, STAT_ROWS))


def _band_kernel(*refs, n_prev, Ta, H, shared_kv, want_lse):
    q_ref = refs[0]
    k_refs = refs[1:2 + n_prev]
    v_refs = refs[2 + n_prev:3 + 2 * n_prev]
    bt_ref = refs[3 + 2 * n_prev]
    o_ref = refs[4 + 2 * n_prev]
    lse_ref = refs[5 + 2 * n_prev] if want_lse else None
    ai = pl.program_id(1)
    k = jnp.concatenate([r[...] for r in k_refs], axis=0).astype(BF16)
    v = jnp.concatenate([r[...] for r in v_refs], axis=0).astype(BF16)
    W = (n_prev + 1) * Ta
    col = lax.broadcasted_iota(jnp.int32, (1, W), 1)
    col_valid = (ai - n_prev) * Ta + col >= 0
    q = q_ref[...]
    outs, lses = [], []
    for h in range(H):
        hk = 0 if shared_kv else h
        qh = (q[:, h * HEAD_DIM:(h + 1) * HEAD_DIM] * (HEAD_DIM ** -0.5)).astype(BF16)
        s = _dot_nt(qh, k[:, hk * HEAD_DIM:(hk + 1) * HEAD_DIM]) + bt_ref[h]
        s = jnp.where(col_valid, s, NEG_INF)
        m = jnp.max(s, axis=-1, keepdims=True)
        p = jnp.exp(s - m)
        l = jnp.sum(p, axis=-1, keepdims=True)
        outs.append(_dot(p.astype(BF16), v[:, hk * HEAD_DIM:(hk + 1) * HEAD_DIM]) / l)
        if want_lse:
            lses.append(jnp.broadcast_to(m + jnp.log(l), (Ta, HEAD_DIM)))
    o_ref[...] = jnp.concatenate(outs, axis=1)
    if want_lse:
        lse_ref[...] = jnp.concatenate(lses, axis=1)


def band_attention(q, k, v, btab, Ta, n_prev, want_lse, widths=None, col_blocks=(0, 0, 0)):
    R, A = q.shape[:2]
    QW, KW = widths if widths else (q.shape[2], k.shape[2])
    H = QW // HEAD_DIM
    shared_kv = KW == HEAD_DIM
    nA = A // Ta
    qc, kc, vc = col_blocks

    def prev_map(p, c):
        return lambda r, a: (r, jnp.maximum(a - n_prev + p, 0), c)

    cur = lambda r, a: (r, a, 0)
    k_specs = ([pl.BlockSpec((None, Ta, KW), prev_map(p, kc)) for p in range(n_prev)]
               + [pl.BlockSpec((None, Ta, KW), lambda r, a: (r, a, kc))])
    v_specs = ([pl.BlockSpec((None, Ta, KW), prev_map(p, vc)) for p in range(n_prev)]
               + [pl.BlockSpec((None, Ta, KW), lambda r, a: (r, a, vc))])
    out_shape = [jax.ShapeDtypeStruct((R, A, QW), F32)]
    out_specs = [pl.BlockSpec((None, Ta, QW), cur)]
    if want_lse:
        out_shape.append(jax.ShapeDtypeStruct((R, A, QW), F32))
        out_specs.append(pl.BlockSpec((None, Ta, QW), cur))
    res = pl.pallas_call(
        functools.partial(_band_kernel, n_prev=n_prev, Ta=Ta, H=H, shared_kv=shared_kv, want_lse=want_lse),
        grid=(R, nA),
        in_specs=[pl.BlockSpec((None, Ta, QW), lambda r, a: (r, a, qc))] + k_specs + v_specs
                 + [pl.BlockSpec(btab.shape, lambda r, a: (0, 0, 0))],
        out_specs=out_specs,
        out_shape=out_shape,
        compiler_params=_cparams(("parallel", "arbitrary")),
    )(q, *([k] * (n_prev + 1)), *([v] * (n_prev + 1)), btab)
    return res if want_lse else res[0]


def _gelu_tanh(x):
    return 0.5 * x * (1.0 + jnp.tanh(math.sqrt(2.0 / math.pi) * (x + 0.044715 * (x * x * x))))


def _compress_kernel(ks_ref, vs_ref, pk_ref, pv_ref, w1k_ref, w2k_ref, w1v_ref, w2vt_ref, kc_ref, vct_ref):
    def hidden(src, pos, w1):
        x = src[...]
        n = x.shape[0]
        first = _dot((x + pos[0:1, :]).astype(BF16), w1[0])
        second = _dot((x + pos[1:2, :]).astype(BF16), w1[1])
        return _gelu_tanh(first + pltpu.roll(second, n - 1, axis=0)).astype(BF16)

    kc_ref[...] = _dot(hidden(ks_ref, pk_ref, w1k_ref), w2k_ref[...]).astype(BF16)
    vct_ref[...] = _dot_nt(w2vt_ref[...], hidden(vs_ref, pv_ref, w1v_ref)).astype(BF16)


def nsa_compress(k_chunks, v_chunks, pos_k, pos_v, w1k, w2k, w1v, w2v_t):
    n = k_chunks.shape[0]
    return pl.pallas_call(
        _compress_kernel,
        out_shape=[jax.ShapeDtypeStruct((n, HEAD_DIM), BF16), jax.ShapeDtypeStruct((HEAD_DIM, n), BF16)],
        compiler_params=_cparams(None),
    )(k_chunks, v_chunks, pos_k, pos_v, w1k, w2k, w1v, w2v_t)


def _cmp_topk_kernel(nq_ref, ks_ref, vs_ref, kc_ref, vct_ref, ovt_ref, eye_ref,
                     ocmp_ref, qaug_ref, st_ref, kaug_ref, vst_ref, *, Tq, n_cmp, n_sel, nbg):
    q0 = pl.program_id(0) * Tq
    t = q0 + lax.broadcasted_iota(jnp.int32, (n_cmp, Tq), 1)
    cmp_end = NSA_CMP_STRIDE * lax.broadcasted_iota(jnp.int32, (n_cmp, Tq), 0) + (NSA_CMP_LEN - 1)
    c_neg = jnp.where(cmp_end <= t, 0.0, NEG_INF)
    sees_any = q0 + lax.broadcasted_iota(jnp.int32, (1, Tq), 1) >= NSA_CMP_LEN - 1
    kc = kc_ref[...]
    vct = vct_ref[...]
    q = nq_ref[...]
    eye = eye_ref[...]
    ks = ks_ref[...]
    ks_t = _dot_nt(eye, ks)
    tok = q0 + lax.broadcasted_iota(jnp.int32, (Tq, nbg), 0)
    blk = jnp.bitwise_and(jnp.right_shift(tok, NSA_SEL_BLOCK.bit_length() - 1), nbg - 1)
    onehot = jnp.where(blk == lax.broadcasted_iota(jnp.int32, (Tq, nbg), 1), 1.0, 0.0).astype(BF16)
    stat_cols = jnp.where(lax.broadcasted_iota(jnp.int32, (Tq, STAT_ROWS), 1) < 3, 1.0, 0.0).astype(BF16)
    kaug_ref[...] = jnp.concatenate([ks, onehot, stat_cols], axis=1)
    ones_row = jnp.where(lax.broadcasted_iota(jnp.int32, (STAT_ROWS, Tq), 0) < 1, 1.0, 0.0).astype(BF16)
    vst_ref[...] = jnp.concatenate([_dot_nt(eye, vs_ref[...]).astype(BF16), ones_row], axis=0)
    psum = jnp.zeros((n_cmp, Tq), F32)
    outs, q_ts, qn2, l_self = [], [], [], []
    for h in range(NSA_HEADS):
        qh = (q[:, h * HEAD_DIM:(h + 1) * HEAD_DIM] * (HEAD_DIM ** -0.5 * LOG2E)).astype(BF16)
        q_t = _dot_nt(eye, qh)
        q_ts.append(q_t.astype(BF16))
        qn2.append(jnp.sum(q_t * q_t, axis=0, keepdims=True))
        l_self.append(jnp.sum(q_t * ks_t, axis=0, keepdims=True))
        s = _dot(kc, q_ts[h]) + c_neg
        p = jnp.exp2(s - jnp.max(s, axis=0, keepdims=True))
        p = p * jnp.where(sees_any, 1.0 / jnp.sum(p, axis=0, keepdims=True), 0.0)
        psum = psum + p
        outs.append(_dot(vct, p.astype(BF16)).T)
    ocmp_ref[...] = jnp.concatenate(outs, axis=1)
    st_ref[0] = jnp.concatenate(qn2, axis=0)
    st_ref[1] = jnp.broadcast_to(jnp.sum(ks_t * ks_t, axis=0, keepdims=True), (NSA_HEADS, Tq))
    st_ref[2] = jnp.concatenate(l_self, axis=0)
    hi = psum.astype(BF16)
    lo = (psum - hi.astype(F32)).astype(BF16)
    imp = _dot(ovt_ref[...], hi) + _dot(ovt_ref[...], lo)
    t = q0 + lax.broadcasted_iota(jnp.int32, (n_sel, Tq), 1)
    j = lax.broadcasted_iota(jnp.int32, (n_sel, Tq), 0)
    start = j * NSA_SEL_BLOCK
    cur_blk = (start <= t) & (t < start + NSA_SEL_BLOCK)
    prev_blk = (start + NSA_SEL_BLOCK <= t) & (t < start + 2 * NSA_SEL_BLOCK)
    forced = (j == 0) | cur_blk | prev_blk
    score = jnp.where(forced, NSA_FORCE_SCORE, jnp.where(start <= t, imp, -NSA_FORCE_SCORE))
    jf = j.astype(F32)
    taken = jnp.float32(-3e38)
    for _ in range(min(NSA_TOPK, n_sel)):
        mx = jnp.max(score, axis=0, keepdims=True)
        first = jnp.min(jnp.where(score == mx, jf, float(n_sel)), axis=0, keepdims=True)
        score = jnp.where(jf == first, taken, score)
    sel_bias_t = jnp.where(score == taken, 0.0, SEL_NEG).astype(BF16)
    for h in range(NSA_HEADS):
        for g in range(n_sel // nbg):
            qaug_ref[g, h] = jnp.concatenate([q_ts[h], sel_bias_t[g * nbg:(g + 1) * nbg]], axis=0)


def nsa_cmp_topk(y, nq_block, k_sel, v_sel, kc, vc_t, overlap_t, Tq, nbg):
    S = y.shape[0]
    n_cmp = kc.shape[0]
    n_sel = S // NSA_SEL_BLOCK
    G = n_sel // nbg
    assert nbg & (nbg - 1) == 0 and NSA_SEL_BLOCK & (NSA_SEL_BLOCK - 1) == 0
    const = lambda a: pl.BlockSpec(a.shape, lambda i: (0,) * a.ndim)
    rows = lambda w: pl.BlockSpec((Tq, w), lambda i: (i, 0))
    eye = jnp.eye(HEAD_DIM, dtype=BF16)
    kw = HEAD_DIM + nbg + STAT_ROWS
    return pl.pallas_call(
        functools.partial(_cmp_topk_kernel, Tq=Tq, n_cmp=n_cmp, n_sel=n_sel, nbg=nbg),
        grid=(S // Tq,),
        in_specs=[pl.BlockSpec((Tq, NSA_W), lambda i: (i, nq_block)), rows(HEAD_DIM), rows(HEAD_DIM),
                  const(kc), const(vc_t), const(overlap_t), const(eye)],
        out_specs=[rows(NSA_W),
                   pl.BlockSpec((G, NSA_HEADS, HEAD_DIM + nbg, Tq), lambda i: (0, 0, 0, i)),
                   pl.BlockSpec((3, NSA_HEADS, Tq), lambda i: (0, 0, i)),
                   rows(kw),
                   pl.BlockSpec((HEAD_DIM + STAT_ROWS, Tq), lambda i: (0, i))],
        out_shape=[jax.ShapeDtypeStruct((S, NSA_W), F32),
                   jax.ShapeDtypeStruct((G, NSA_HEADS, HEAD_DIM + nbg, S), BF16),
                   jax.ShapeDtypeStruct((3, NSA_HEADS, S), F32),
                   jax.ShapeDtypeStruct((S, kw), BF16),
                   jax.ShapeDtypeStruct((HEAD_DIM + STAT_ROWS, S), BF16)],
        compiler_params=_cparams(("parallel",)),
    )(y, k_sel, v_sel, kc, vc_t, overlap_t, eye)


def _outproj_even_kernel(oa_ref, ob_ref, wa_ref, wb_ref, g_ref, h_ref, out_ref):
    m = _dot(oa_ref[...], wa_ref[...]) + _dot(ob_ref[...], wb_ref[...])
    out_ref[...] = h_ref[...] + _rms(m, g_ref[...])


def outproj_even(oa, ob, wa, wb, g, h, tm):
    S, D = h.shape
    rows = lambda w: pl.BlockSpec((tm, w), lambda i: (i, 0))
    full = lambda a: pl.BlockSpec(a.shape, lambda i: (0,) * a.ndim)
    return pl.pallas_call(
        _outproj_even_kernel,
        grid=(S // tm,),
        in_specs=[rows(oa.shape[1]), rows(ob.shape[1]), full(wa), full(wb), full(g), rows(D)],
        out_specs=rows(D),
        out_shape=jax.ShapeDtypeStruct((S, D), F32),
        compiler_params=_cparams(("parallel",)),
    )(oa, ob, wa, wb, g, h)


def _outproj_odd_kernel(o0_ref, o1_ref, o2_ref, l0_ref, l1_ref, l2_ref, oc_ref, os_ref, ow_ref,
                        gc_ref, gs_ref, gw_ref, wd_ref, wn_ref, g_ref, h_ref, out_ref, scr):
    tm = h_ref.shape[0]

    def rows_of(ref):
        if len(ref.shape) == 2:
            return ref[...]
        d, _, width = ref.shape
        for r in range(d):
            v = ref[r]
            for c in range(width // LANES):
                scr[c, pl.ds(r, tm // d, stride=d), :] = v[:, c * LANES:(c + 1) * LANES]
        return jnp.concatenate([scr[c] for c in range(width // LANES)], axis=1)

    l0, l1, l2 = rows_of(l0_ref), rows_of(l1_ref), rows_of(l2_ref)
    mx = jnp.maximum(jnp.maximum(l0, l1), l2)
    e0, e1, e2 = jnp.exp(l0 - mx), jnp.exp(l1 - mx), jnp.exp(l2 - mx)
    o_dil = (e0 * rows_of(o0_ref) + e1 * rows_of(o1_ref) + e2 * rows_of(o2_ref)) / (e0 + e1 + e2)
    o_nsa = (jax.nn.sigmoid(gc_ref[...]) * oc_ref[...] + jax.nn.sigmoid(gs_ref[...]) * os_ref[...].astype(F32)
             + jax.nn.sigmoid(gw_ref[...]) * ow_ref[...])
    m = _dot(o_dil.astype(BF16), wd_ref[...]) + _dot(o_nsa.astype(BF16), wn_ref[...])
    out_ref[...] = h_ref[...] + _rms(m, g_ref[...])


def outproj_odd(o_dil, lse_dil, o_cmp, o_sel, o_win, y, gate_block, wd, wn, g, h, tm):
    S, D = h.shape
    rows = lambda w: pl.BlockSpec((tm, w), lambda i: (i, 0))
    full = lambda a: pl.BlockSpec(a.shape, lambda i: (0,) * a.ndim)
    gate = lambda b: pl.BlockSpec((tm, NSA_W), lambda i: (i, gate_block + b))

    def dil(a):
        if a.ndim == 2:
            return rows(DIL_GW)
        return pl.BlockSpec((a.shape[0], tm // a.shape[0], DIL_GW), lambda i: (0, i, 0))

    return pl.pallas_call(
        _outproj_odd_kernel,
        grid=(S // tm,),
        in_specs=[dil(a) for a in (*o_dil, *lse_dil)] + [rows(NSA_W)] * 3
                 + [gate(0), gate(1), gate(2), full(wd), full(wn), full(g), rows(D)],
        out_specs=rows(D),
        out_shape=jax.ShapeDtypeStruct((S, D), F32),
        scratch_shapes=[pltpu.VMEM((DIL_GW // LANES, tm, LANES), F32)],
        compiler_params=_cparams(("parallel",)),
    )(*o_dil, *lse_dil, o_cmp, o_sel, o_win, y, y, y, wd, wn, g, h)


def _mlp_ple_kernel(h_ref, g1_ref, wu_ref, wd_ref, g2_ref, g3_ref, wg_ref, p_ref, wp_ref, out_ref,
                    hn_ref, acc_ref):
    j = pl.program_id(1)

    @pl.when(j == 0)
    def _():
        hn_ref[...] = _rms(h_ref[...], g1_ref[...]).astype(BF16)
        acc_ref[...] = jnp.zeros(acc_ref.shape, F32)

    u = jnp.maximum(_dot(hn_ref[...], wu_ref[...]), 0.0)
    acc_ref[...] += _dot((u * u).astype(BF16), wd_ref[...])

    @pl.when(j == pl.num_programs(1) - 1)
    def _():
        h2 = h_ref[...] + _rms(acc_ref[...], g2_ref[...])
        gate = jax.nn.sigmoid(_dot(_rms(h2, g3_ref[...]).astype(BF16), wg_ref[...]))
        out_ref[...] = h2 + gate * _dot(p_ref[...].astype(BF16), wp_ref[...])


def mlp_ple(h, g1, wu, wd, g2, g3, wg, p, wp, tm, tf):
    S, D = h.shape
    FF = wu.shape[1]
    PD = p.shape[1]
    const = lambda a: pl.BlockSpec(a.shape, lambda i, j: (0,) * a.ndim)
    return pl.pallas_call(
        _mlp_ple_kernel,
        grid=(S // tm, FF // tf),
        in_specs=[pl.BlockSpec((tm, D), lambda i, j: (i, 0)), const(g1),
                  pl.BlockSpec((D, tf), lambda i, j: (0, j)),
                  pl.BlockSpec((tf, D), lambda i, j: (j, 0)),
                  const(g2), const(g3), const(wg),
                  pl.BlockSpec((tm, PD), lambda i, j: (i, 0)), const(wp)],
        out_specs=pl.BlockSpec((tm, D), lambda i, j: (i, 0)),
        out_shape=jax.ShapeDtypeStruct((S, D), F32),
        scratch_shapes=[pltpu.VMEM((tm, D), BF16), pltpu.VMEM((tm, D), F32)],
        compiler_params=_cparams(("parallel", "arbitrary")),
    )(h, g1, wu, wd, g2, g3, wg, p, wp)


def _t5_bucket_of(dist):
    n = jnp.maximum(dist, 0)
    max_exact = T5_BUCKETS // 2
    ratio = jnp.log(jnp.maximum(n, 1).astype(F32) / max_exact) / math.log(T5_MAX_DIST / max_exact)
    large = jnp.minimum(max_exact + (ratio * (T5_BUCKETS - max_exact)).astype(jnp.int32), T5_BUCKETS - 1)
    return jnp.where(n < max_exact, n, large)


def _bias_of_dist(bias, dist):
    bucket = _t5_bucket_of(dist)[None]
    out = jnp.zeros((bias.shape[1],) + dist.shape, F32)
    for b in range(T5_BUCKETS):
        out = jnp.where(bucket == b, bias[b].reshape((-1,) + (1,) * dist.ndim), out)
    return out


def _causal_tables(T):
    k = jnp.arange(T)[:, None]
    q = jnp.arange(T)[None, :]
    diag = jnp.where(q >= k, 0.0, NEG_INF).astype(F32)
    return jnp.stack([diag, jnp.zeros((T, T), F32)])[:, None]


def _t5_delta_tables(bias, T):
    n_delta = -(-(T5_MAX_DIST - 1) // T) + 2
    k = jnp.arange(T)[None, :, None]
    q = jnp.arange(T)[None, None, :]
    dist = jnp.arange(n_delta)[:, None, None] * T + q - k
    val = (_bias_of_dist(bias, dist) - bias[T5_BUCKETS - 1].reshape(-1, 1, 1, 1)) * LOG2E
    return jnp.transpose(jnp.where(dist[None] >= 0, val, NEG_INF), (1, 0, 2, 3))


def _band_table(bias, Ta, n_prev, max_rel, stride, inclusive):
    P = n_prev * Ta
    i = jnp.arange(Ta)[:, None]
    c = jnp.arange(P + Ta)[None, :]
    rel = i + P - c
    ok = (rel >= 0) & ((rel <= max_rel) if inclusive else (rel < max_rel))
    return jnp.where(ok[None], _bias_of_dist(bias, rel * stride), NEG_INF)


def _tile(S, pref):
    t = min(pref, S)
    assert S % t == 0
    return t


def even_mixer_core(h, g_pre, w_in, cos, sin, w_uq, q_norm, w_ukv, kv_norm, forget_bias, T):
    S = h.shape[0]
    scale = MLA_QK_DIM ** -0.5 * LOG2E
    half = MLA_ROPE_DIM // 2
    cos_t, sin_t = jnp.transpose(cos), jnp.transpose(sin)
    ones = jnp.ones((MLA_NOPE_DIM, S), F32)
    cq = jnp.concatenate([ones, cos_t, cos_t], axis=0) * scale
    sq = jnp.concatenate([0.0 * ones, -sin_t, sin_t], axis=0) * scale
    ck = jnp.concatenate([cos, cos], axis=1)
    sk = jnp.concatenate([-sin, sin], axis=1)
    wq = jnp.transpose(w_uq.reshape(MLA_Q_RANK, MLA_HEADS, MLA_QK_DIM), (1, 2, 0))
    swap = np.concatenate([np.arange(MLA_NOPE_DIM), MLA_NOPE_DIM + half + np.arange(half),
                           MLA_NOPE_DIM + np.arange(half)])
    wqs = wq[:, swap, :]
    wkv = jnp.transpose(w_ukv.reshape(MLA_KV_RANK, MLA_HEADS, MLA_NOPE_DIM + MLA_V_DIM), (1, 0, 2))
    wk = jnp.pad(wkv[:, :, :MLA_NOPE_DIM], ((0, 0), (0, 0), (0, MLA_ROPE_DIM)))
    wv = jnp.transpose(wkv[:, :, MLA_NOPE_DIM:], (0, 2, 1))
    e96 = jnp.pad(jnp.eye(MLA_ROPE_DIM, dtype=F32), ((0, 0), (MLA_NOPE_DIM, 0)))
    qm, km, vm, fq, fk, fv, logf, st = even_prep(
        h, g_pre, w_in, wq.astype(BF16), wqs.astype(BF16), wk.astype(BF16), e96.astype(BF16), wv.astype(BF16),
        jnp.eye(MLA_QK_DIM, dtype=BF16), q_norm[None, :], kv_norm[None, :], forget_bias[None, :],
        cq, sq, ck, sk, _tile(S, 512))
    neg_f = cumsum_split(jnp.transpose(logf))
    hps = MLA_HEADS
    causal = _causal_tables(T)
    r_mla, fast_mla = _reference_rows(st[0], st[1], st[2], 0.0, 0.0, 0.0, T, hps)
    o_mla = flash_causal(qm[None], r_mla, km, vm, causal, fast_mla, T, hps)
    r_fox, fast_fox = _reference_rows(st[3], st[4], st[5], jnp.sum(neg_f.astype(F32), axis=0), 0.0, 0.0, T, hps)
    fk_aug = jnp.concatenate([fk, jnp.transpose(neg_f, (1, 2, 0)),
                              jnp.zeros((FOX_HEADS, S, FOX_PAD - FOX_SPLIT), BF16), _stat_cols(FOX_HEADS, S)], axis=2)
    o_fox = flash_causal(fq[None], r_fox, fk_aug, fv, causal, fast_fox, T, hps)
    return o_mla, o_fox


def _odd_deint_plan():
    G = len(DIL_GROUPS)
    return tuple(((b * G + g) * DIL_GW, DIL_GW, d) for g, (_, d) in enumerate(DIL_GROUPS) if d > 1 for b in range(3))


def odd_mixer_core(y, deint, t5_bias, pos_k, pos_v, w1k, w2k, w1v, w2v, T):
    S = y.shape[0]
    G = len(DIL_GROUPS)
    Ta = 256
    o_dil, lse_dil = [], []
    for g, (w, d) in enumerate(DIL_GROUPS):
        ta = _tile(S // d, Ta)
        assert w // d <= ta
        btab = _band_table(t5_bias[:, g * DIL_HEADS:(g + 1) * DIL_HEADS], ta, 1, w // d, d, True)
        if d == 1:
            o, lse = band_attention(y[None], y[None], y[None], btab, ta, 1, True,
                                    widths=(DIL_GW, DIL_GW), col_blocks=(g, G + g, 2 * G + g))
            o, lse = o[0], lse[0]
        else:
            q, k, v = (deint[(b * G + g) * DIL_GW] for b in range(3))
            o, lse = band_attention(q, k, v, btab, ta, 1, True)
        o_dil.append(o)
        lse_dil.append(lse)
    nq_block = 3 * G
    gate_block = nq_block + 1
    base = (gate_block + 3) * NSA_W
    k_cmp, v_cmp, k_sel, v_sel, k_win, v_win = (y[:, base + i * HEAD_DIM:base + (i + 1) * HEAD_DIM] for i in range(6))
    bias_nsa = t5_bias[:, G * DIL_HEADS:]
    n_chunk = S // NSA_CMP_STRIDE
    cw = NSA_CMP_STRIDE * HEAD_DIM
    kc, vc = nsa_compress(k_cmp.reshape(n_chunk, cw), v_cmp.reshape(n_chunk, cw),
                          pos_k.reshape(2, cw), pos_v.reshape(2, cw),
                          w1k.reshape(2, cw, -1).astype(BF16), w2k.astype(BF16),
                          w1v.reshape(2, cw, -1).astype(BF16), jnp.transpose(w2v).astype(BF16))
    n_sel = S // NSA_SEL_BLOCK
    ci = np.arange(n_chunk)[None, :] * NSA_CMP_STRIDE
    sj = np.arange(n_sel)[:, None] * NSA_SEL_BLOCK
    overlap_t = jnp.asarray(((ci < sj + NSA_SEL_BLOCK) & (ci + NSA_CMP_LEN > sj)).astype(np.float32), BF16)
    nbg = min(n_sel, NSA_SEL_GROUP_BLOCKS)
    o_cmp, qaug, st, kaug, v_sel_t = nsa_cmp_topk(y, nq_block, k_sel.astype(BF16), v_sel.astype(BF16), kc, vc,
                                                  overlap_t, _tile(S, 256), nbg)
    delta = (bias_nsa - bias_nsa[T5_BUCKETS - 1]) * LOG2E
    r_sel, fast_sel = _reference_rows(st[0], st[1, :1], st[2], 0.0, jnp.max(delta, axis=0)[:, None],
                                      delta[0][:, None], T, NSA_HEADS)
    o_sel = flash_causal(qaug, r_sel, kaug[None], v_sel_t[None], _t5_delta_tables(bias_nsa, T), fast_sel, T,
                         hps=NSA_HEADS, key_group_tokens=nbg * NSA_SEL_BLOCK)
    Tw = _tile(S, NSA_WINDOW)
    n_prev = NSA_WINDOW // Tw
    wtab = _band_table(bias_nsa, Tw, n_prev, NSA_WINDOW, 1, False)
    o_win = band_attention(y[None], k_win[None], v_win[None], wtab, Tw, n_prev, False,
                           widths=(NSA_W, HEAD_DIM), col_blocks=(nq_block, 0, 0))[0]
    return o_dil, lse_dil, o_cmp, o_sel, o_win, gate_block


def _pad_cols(w, n):
    return jnp.pad(w, ((0, 0), (0, n - w.shape[1])))


def _even_w_in(w):
    cq, ckv, kr, fq, fk, fv, fl = jnp.split(w, np.cumsum([256, 128, 32, 512, 512, 512])[:], axis=1)
    half = MLA_ROPE_DIM // 2
    kr_sw = jnp.concatenate([kr[:, half:], kr[:, :half]], axis=1)
    head = _pad_cols(jnp.concatenate([cq, ckv, kr, kr_sw, fl], axis=1), 512)
    return jnp.concatenate([head, fq, fk, fv], axis=1).astype(BF16)


def _odd_w_in(w):
    n_main = 3 * 3 * DIL_GW + NSA_W
    main, six = w[:, :n_main], w[:, n_main:n_main + 6 * HEAD_DIM]
    gl = w[:, n_main + 6 * HEAD_DIM:]
    gl = jnp.transpose(gl.reshape(-1, NSA_HEADS, 3), (0, 2, 1))
    gl = jnp.repeat(gl[..., None], HEAD_DIM, axis=-1).reshape(w.shape[0], 3 * NSA_W)
    return _pad_cols(jnp.concatenate([main, gl, six], axis=1), ODD_IN_PAD).astype(BF16)


def _trunk(x, p, positions, t5_bias, ev_w_in, ev_q_norm, ev_w_uq, ev_kv_norm, ev_w_ukv, ev_forget_bias,
           ev_w_out, od_w_in, od_cmp_pos_k, od_cmp_pos_v, od_cmp_w1_k, od_cmp_w2_k, od_cmp_w1_v, od_cmp_w2_v,
           od_w_out, norm_mix_pre, norm_mix_post, norm_mlp_pre, norm_mlp_post, w_mlp_up, w_mlp_down,
           ple_norm, w_ple_gate, w_ple_proj):
    S, D = x.shape
    depth = p.shape[0]
    T = _tile(S, 512)
    tm = _tile(S, 512)
    inv_freq = ROPE_THETA ** (-jnp.arange(0, MLA_ROPE_DIM, 2, dtype=F32) / MLA_ROPE_DIM)
    angles = positions.astype(F32)[:, None] * inv_freq
    cos, sin = jnp.cos(angles), jnp.sin(angles)
    h = x
    for i in range(depth):
        j = i // 2
        if i % 2 == 0:
            o_mla, o_fox = even_mixer_core(h, norm_mix_pre[i][None], _even_w_in(ev_w_in[j]), cos, sin, ev_w_uq[j],
                                           ev_q_norm[j], ev_w_ukv[j], ev_kv_norm[j], ev_forget_bias[j], T)
            wo = ev_w_out[j].astype(BF16)
            na = MLA_HEADS * MLA_V_DIM
            h = outproj_even(o_mla, o_fox, wo[:na], wo[na:], norm_mix_post[i][None], h, tm)
        else:
            plan = _odd_deint_plan()
            y, *parts = odd_inproj(h, norm_mix_pre[i][None], _odd_w_in(od_w_in[j]), _tile(S, 256), plan)
            o_dil, lse_dil, o_cmp, o_sel, o_win, gate_block = odd_mixer_core(
                y, {col: a for (col, _, _), a in zip(plan, parts)}, t5_bias, od_cmp_pos_k[j], od_cmp_pos_v[j], od_cmp_w1_k[j], od_cmp_w2_k[j],
                od_cmp_w1_v[j], od_cmp_w2_v[j], T)
            wo = od_w_out[j].astype(BF16)
            h = outproj_odd(o_dil, lse_dil, o_cmp, o_sel, o_win, y, gate_block, wo[:DIL_GW], wo[DIL_GW:],
                            norm_mix_post[i][None], h, tm)
        h = mlp_ple(h, norm_mlp_pre[i][None], w_mlp_up[i].astype(BF16), w_mlp_down[i].astype(BF16),
                    norm_mlp_post[i][None], ple_norm[i][None], w_ple_gate[i].astype(BF16), p[i],
                    w_ple_proj[i].astype(BF16), _tile(S, 1024), 512)
    return h


def kernel(x, p, positions, t5_bias, ev_w_in, ev_q_norm, ev_w_uq, ev_kv_norm, ev_w_ukv, ev_forget_bias, ev_w_out, od_w_in, od_cmp_pos_k, od_cmp_pos_v, od_cmp_w1_k, od_cmp_w2_k, od_cmp_w1_v, od_cmp_w2_v, od_w_out, norm_mix_pre, norm_mix_post, norm_mlp_pre, norm_mlp_post, w_mlp_up, w_mlp_down, ple_norm, w_ple_gate, w_ple_proj):
    params = (t5_bias, ev_w_in, ev_q_norm, ev_w_uq, ev_kv_norm, ev_w_ukv, ev_forget_bias, ev_w_out, od_w_in,
              od_cmp_pos_k, od_cmp_pos_v, od_cmp_w1_k, od_cmp_w2_k, od_cmp_w1_v, od_cmp_w2_v, od_w_out,
              norm_mix_pre, norm_mix_post, norm_mlp_pre, norm_mlp_post, w_mlp_up, w_mlp_down, ple_norm,
              w_ple_gate, w_ple_proj)
    outs = [_trunk(x[b], p[:, b], positions[b], *params) for b in range(x.shape[0])]
    return jnp.stack(outs).astype(x.dtype)
```

```python
import functools
import math

import numpy as np
import jax
import jax.numpy as jnp
from jax import lax
from jax.experimental import pallas as pl
from jax.experimental.pallas import tpu as pltpu

F32 = jnp.float32
BF16 = jnp.bfloat16

HEAD_DIM = 64
RMS_EPS = 1e-6
NEG_INF = -1e30
SEL_NEG = -(2.0 ** 99)

MLA_HEADS = 8
MLA_NOPE_DIM = 64
MLA_ROPE_DIM = 32
MLA_V_DIM = 64
MLA_Q_RANK = 256
MLA_KV_RANK = 128
MLA_QK_DIM = MLA_NOPE_DIM + MLA_ROPE_DIM
ROPE_THETA = 10000.0
FOX_HEADS = 8
FOX_SPLIT = 3
FOX_PAD = 16
LOG2E = math.log2(math.e)
STAT_ROWS = 16
FAST_OVER = 64.0
FAST_MARGIN = 150.0
P_CHUNK = 32
DIL_GROUPS = ((128, 1), (512, 4), (2048, 16))
DIL_HEADS = 4
DIL_GW = DIL_HEADS * HEAD_DIM
NSA_HEADS = 4
NSA_W = NSA_HEADS * HEAD_DIM
NSA_CMP_LEN = 32
NSA_CMP_STRIDE = 16
NSA_SEL_BLOCK = 64
NSA_TOPK = 16
NSA_WINDOW = 512
NSA_FORCE_SCORE = 1e9
NSA_SEL_GROUP_BLOCKS = 128
T5_BUCKETS = 32
T5_MAX_DIST = 2048

EVEN_IN_PAD = 2048
ODD_IN_PAD = 3840

VMEM_LIMIT_BYTES = 56 * 1024 * 1024


def _cparams(sem):
    return pltpu.CompilerParams(dimension_semantics=sem, vmem_limit_bytes=VMEM_LIMIT_BYTES)


def _rms(x, g):
    return x * lax.rsqrt(jnp.mean(x * x, axis=-1, keepdims=True) + RMS_EPS) * g


def _dot(a, b):
    return jnp.dot(a, b, preferred_element_type=F32)


def _dot_nt(a, b):
    return lax.dot_general(a, b, (((1,), (1,)), ((), ())), preferred_element_type=F32)


LANES = 128


def _odd_inproj_kernel(h_ref, g_ref, w_ref, y_ref, *rest, deint):
    out_refs, scr = rest[:-1], rest[-1]
    y = _dot(_rms(h_ref[...], g_ref[...]).astype(BF16), w_ref[...])
    y_ref[...] = y
    tm = y.shape[0]
    for out, (col, width, d) in zip(out_refs, deint):
        n = width // LANES
        for c in range(n):
            scr[c] = y[:, col + c * LANES:col + (c + 1) * LANES]
        for r in range(d):
            out[r] = jnp.concatenate([scr[c, pl.ds(r, tm // d, stride=d), :] for c in range(n)], axis=1)


def odd_inproj(h, g, w, tm, deint):
    S, D = h.shape
    N = w.shape[1]
    out_shape = [jax.ShapeDtypeStruct((S, N), F32)]
    out_specs = [pl.BlockSpec((tm, N), lambda i: (i, 0))]
    for _, width, d in deint:
        out_shape.append(jax.ShapeDtypeStruct((d, S // d, width), F32))
        out_specs.append(pl.BlockSpec((d, tm // d, width), lambda i: (0, i, 0)))
    return pl.pallas_call(
        functools.partial(_odd_inproj_kernel, deint=deint),
        grid=(S // tm,),
        in_specs=[pl.BlockSpec((tm, D), lambda i: (i, 0)),
                  pl.BlockSpec((1, D), lambda i: (0, 0)),
                  pl.BlockSpec((D, N), lambda i: (0, 0))],
        out_specs=out_specs,
        out_shape=out_shape,
        scratch_shapes=[pltpu.VMEM((max(wd for _, wd, _ in deint) // LANES, tm, LANES), F32)],
        compiler_params=_cparams(("parallel",)),
    )(h, g, w)


def _even_prep_kernel(h_ref, g_ref, win_ref, wq_ref, wqs_ref, wk_ref, e_ref, wv_ref, eye_ref, qn_ref, kvn_ref,
                      fb_ref, cq_ref, sq_ref, ck_ref, sk_ref,
                      qm_ref, km_ref, vm_ref, fq_ref, fk_ref, fv_ref, lf_ref, st_ref):
    y = _dot(_rms(h_ref[...], g_ref[...]).astype(BF16), win_ref[...])
    tm = y.shape[0]
    c_q = y[:, 0:MLA_Q_RANK]
    c_kv = y[:, MLA_Q_RANK:MLA_Q_RANK + MLA_KV_RANK]
    o = MLA_Q_RANK + MLA_KV_RANK
    k_r = y[:, o:o + MLA_ROPE_DIM]
    k_rs = y[:, o + MLA_ROPE_DIM:o + 2 * MLA_ROPE_DIM]
    f_logit = y[:, o + 2 * MLA_ROPE_DIM:o + 2 * MLA_ROPE_DIM + FOX_HEADS]
    nq = _rms(c_q, qn_ref[...]).astype(BF16)
    nkv = _rms(c_kv, kvn_ref[...]).astype(BF16)
    k_rot = (k_r * ck_ref[...] + k_rs * sk_ref[...]).astype(BF16)
    k_rot96 = _dot(k_rot, e_ref[...])
    cq = cq_ref[...]
    sq = sq_ref[...]
    eye = eye_ref[...]
    row = lax.broadcasted_iota(jnp.int32, (STAT_ROWS, tm), 0)
    ones_row = jnp.where(row < 1, 1.0, 0.0).astype(BF16)
    col = lax.broadcasted_iota(jnp.int32, (tm, STAT_ROWS), 1)
    stat_cols = jnp.where(col < 3, 1.0, 0.0).astype(BF16)
    stats = [[] for _ in range(6)]

    def record(slot, q_t, k):
        q_t = q_t.astype(F32)
        k_t = _dot_nt(eye[:k.shape[1], :k.shape[1]], k)
        stats[slot].append(jnp.sum(q_t * q_t, axis=0, keepdims=True))
        stats[slot + 1].append(jnp.sum(k_t * k_t, axis=0, keepdims=True))
        stats[slot + 2].append(jnp.sum(q_t * k_t, axis=0, keepdims=True))

    for h in range(MLA_HEADS):
        q = (_dot_nt(wq_ref[h], nq) * cq + _dot_nt(wqs_ref[h], nq) * sq).astype(BF16)
        k = (_dot(nkv, wk_ref[h]) + k_rot96).astype(BF16)
        qm_ref[h] = q
        km_ref[h] = jnp.concatenate([k, stat_cols], axis=1)
        vm_ref[h] = jnp.concatenate([_dot_nt(wv_ref[h], nkv).astype(BF16), ones_row], axis=0)
        record(0, q, k)
    base = 512
    fw = FOX_HEADS * HEAD_DIM
    eye64 = eye[:HEAD_DIM, :HEAD_DIM]
    ones_rows = jnp.where(row < FOX_SPLIT, 1.0, 0.0).astype(BF16)
    for h in range(FOX_HEADS):
        lo = base + h * HEAD_DIM
        fq = _dot_nt(eye64, (y[:, lo:lo + HEAD_DIM] * (HEAD_DIM ** -0.5 * LOG2E)).astype(BF16)).astype(BF16)
        fk = y[:, lo + fw:lo + fw + HEAD_DIM].astype(BF16)
        fq_ref[h] = jnp.concatenate([fq, ones_rows], axis=0)
        fk_ref[h] = fk
        fv = y[:, lo + 2 * fw:lo + 2 * fw + HEAD_DIM].astype(BF16)
        fv_ref[h] = jnp.concatenate([_dot_nt(eye64, fv).astype(BF16), ones_row], axis=0)
        record(3, fq, fk)
    for i in range(6):
        st_ref[i] = jnp.concatenate(stats[i], axis=0)
    z = f_logit + fb_ref[...]
    lf_ref[...] = jnp.minimum(z, 0.0) - jnp.log1p(jnp.exp(-jnp.abs(z)))


def even_prep(h, g, w_in, wq, wqs, wk, e96, wv, eye, qn, kvn, fb, cq, sq, ck, sk, tm):
    S, D = h.shape
    full = lambda a: pl.BlockSpec(a.shape, lambda i: (0,) * a.ndim)
    rows = lambda w: pl.BlockSpec((tm, w), lambda i: (i, 0))
    cols = lambda w: pl.BlockSpec((w, tm), lambda i: (0, i))
    heads = lambda n, w: pl.BlockSpec((n, tm, w), lambda i: (0, i, 0))
    heads_t = lambda n, w: pl.BlockSpec((n, w, tm), lambda i: (0, 0, i))
    out_shape = [jax.ShapeDtypeStruct((MLA_HEADS, MLA_QK_DIM, S), BF16),
                 jax.ShapeDtypeStruct((MLA_HEADS, S, MLA_QK_DIM + STAT_ROWS), BF16),
                 jax.ShapeDtypeStruct((MLA_HEADS, MLA_V_DIM + STAT_ROWS, S), BF16),
                 jax.ShapeDtypeStruct((FOX_HEADS, HEAD_DIM + FOX_PAD, S), BF16),
                 jax.ShapeDtypeStruct((FOX_HEADS, S, HEAD_DIM), BF16),
                 jax.ShapeDtypeStruct((FOX_HEADS, HEAD_DIM + STAT_ROWS, S), BF16),
                 jax.ShapeDtypeStruct((S, FOX_HEADS), F32),
                 jax.ShapeDtypeStruct((6, MLA_HEADS, S), F32)]
    out_specs = [heads_t(MLA_HEADS, MLA_QK_DIM), heads(MLA_HEADS, MLA_QK_DIM + STAT_ROWS),
                 heads_t(MLA_HEADS, MLA_V_DIM + STAT_ROWS),
                 heads_t(FOX_HEADS, HEAD_DIM + FOX_PAD), heads(FOX_HEADS, HEAD_DIM),
                 heads_t(FOX_HEADS, HEAD_DIM + STAT_ROWS),
                 rows(FOX_HEADS), heads_t(6, MLA_HEADS)]
    return pl.pallas_call(
        _even_prep_kernel,
        grid=(S // tm,),
        in_specs=[rows(D), full(g), full(w_in), full(wq), full(wqs), full(wk), full(e96), full(wv), full(eye),
                  full(qn), full(kvn), full(fb), cols(MLA_QK_DIM), cols(MLA_QK_DIM), rows(MLA_ROPE_DIM),
                  rows(MLA_ROPE_DIM)],
        out_specs=out_specs,
        out_shape=out_shape,
        compiler_params=_cparams(("parallel",)),
    )(h, g, w_in, wq, wqs, wk, e96, wv, eye, qn, kvn, fb, cq, sq, ck, sk)


def _cumsum_split_kernel(x_ref, o_ref):
    x = x_ref[...]
    n = x.shape[1]
    lane = lax.broadcasted_iota(jnp.int32, x.shape, 1)
    shift = 1
    while shift < n:
        x = x + jnp.where(lane >= shift, pltpu.roll(x, shift, axis=1), 0.0)
        shift *= 2
    r = -x * LOG2E
    for i in range(FOX_SPLIT):
        part = r.astype(BF16)
        o_ref[i] = part
        r = r - part.astype(F32)


def cumsum_split(x):
    return pl.pallas_call(
        _cumsum_split_kernel,
        out_shape=jax.ShapeDtypeStruct((FOX_SPLIT,) + x.shape, BF16),
        compiler_params=_cparams(None),
    )(x)


def _flash_kernel(qt_ref, kt_ref, fast_ref, q_ref, r_ref, k_ref, v_ref, bt_ref, o_ref, m_scr, acc_scr, p_scr, *,
                  hps, shared_kv, n_delta, nt, R):
    step = pl.program_id(1)
    qi = qt_ref[step]
    ki = kt_ref[step]
    fast = fast_ref[pl.program_id(0) * nt + qi] != 0
    near = qi - R * ki < n_delta - 1

    @pl.when(ki == 0)
    def _():
        m_scr[...] = jnp.full(m_scr.shape, -3e38, F32)
        acc_scr[...] = jnp.zeros(acc_scr.shape, F32)

    hb = bt_ref.shape[1]
    Tk = k_ref.shape[1]
    Dv = v_ref.shape[1] - STAT_ROWS

    def logits(h, with_table):
        q = jnp.concatenate([q_ref[0, h], r_ref[h]], axis=0)
        s = _dot(k_ref[0 if shared_kv else h], q)
        return s + bt_ref[0, h if hb > 1 else 0] if with_table else s

    def store_p(h, s, m):
        for c in range(Tk // P_CHUNK):
            rows = slice(c * P_CHUNK, (c + 1) * P_CHUNK)
            x = s[rows] if m is None else s[rows] - m
            p_scr[h, rows, :] = jnp.exp2(x).astype(BF16)

    def fast_update(with_table):
        acc_prev = [acc_scr[h] for h in range(hps)]
        for h in range(hps):
            store_p(h, logits(h, with_table), None)
        for h in range(hps):
            acc_scr[h] = acc_prev[h] + _dot(v_ref[0 if shared_kv else h], p_scr[h])

    def safe_update(with_table):
        m_prev = [m_scr[h] for h in range(hps)]
        acc_prev = [acc_scr[h] for h in range(hps)]
        ss = [logits(h, with_table) for h in range(hps)]
        ms = [jnp.maximum(m_prev[h], jnp.max(ss[h], axis=0, keepdims=True)) for h in range(hps)]
        for h in range(hps):
            store_p(h, ss[h], ms[h])
        for h in range(hps):
            alpha = jnp.exp2(m_prev[h] - ms[h])
            acc_scr[h] = alpha * acc_prev[h] + _dot(v_ref[0 if shared_kv else h], p_scr[h])
            m_scr[h] = ms[h]

    for take_fast, update in ((True, fast_update), (False, safe_update)):
        for with_table in (True, False):
            cond = jnp.logical_and(fast == take_fast, near == with_table)
            pl.when(cond)(functools.partial(update, with_table))

    @pl.when(ki == qi // R)
    def _():
        outs = [(acc_scr[h, :Dv, :] / acc_scr[h, Dv:Dv + 1, :]).T for h in range(hps)]
        o_ref[...] = jnp.concatenate(outs, axis=1).astype(o_ref.dtype)


def flash_causal(qT, r_rows, k, vT, btab, fast, T, hps, key_group_tokens=None, out_dtype=BF16):
    G, H, Dq, S = qT.shape
    Dk = Dq + STAT_ROWS
    Hk, Dv, _ = vT.shape
    Dv -= STAT_ROWS
    shared_kv = Hk == 1
    n_delta, Hb, Tk = btab.shape[0], btab.shape[1], btab.shape[2]
    R = Tk // T
    nt = S // T
    qt = np.array([i for i in range(nt) for _ in range(i // R + 1)], np.int32)
    kt = np.array([j for i in range(nt) for j in range(i // R + 1)], np.int32)
    kv_h = 1 if shared_kv else hps
    assert G == 1 or R == 1
    tiles_per_group = (key_group_tokens // T) if G > 1 else 1

    def q_map(g, s, qt, kt, fast):
        return (kt[s] // tiles_per_group if G > 1 else 0, g, 0, qt[s])

    def bt_map(g, s, qt, kt, fast):
        return (jnp.minimum(qt[s] - R * kt[s], n_delta - 1), g if Hb > hps else 0, 0, 0)

    in_specs = [pl.BlockSpec((1, hps, Dq, T), q_map),
                pl.BlockSpec((hps, STAT_ROWS, T), lambda g, s, qt, kt, fast: (g, 0, qt[s])),
                pl.BlockSpec((kv_h, Tk, Dk), lambda g, s, qt, kt, fast: (0 if shared_kv else g, kt[s], 0)),
                pl.BlockSpec((kv_h, Dv + STAT_ROWS, Tk), lambda g, s, qt, kt, fast: (0 if shared_kv else g, 0, kt[s])),
                pl.BlockSpec((1, min(Hb, hps), Tk, T), bt_map)]
    grid_spec = pltpu.PrefetchScalarGridSpec(
        num_scalar_prefetch=3,
        grid=(H // hps, len(qt)),
        in_specs=in_specs,
        out_specs=pl.BlockSpec((T, hps * Dv), lambda g, s, qt, kt, fast: (qt[s], g)),
        scratch_shapes=[pltpu.VMEM((hps, 1, T), F32), pltpu.VMEM((hps, Dv + STAT_ROWS, T), F32),
                        pltpu.VMEM((hps, Tk, T), BF16)],
    )
    return pl.pallas_call(
        functools.partial(_flash_kernel, hps=hps, shared_kv=shared_kv, n_delta=n_delta, nt=nt, R=R),
        grid_spec=grid_spec,
        out_shape=jax.ShapeDtypeStruct((S, H * Dv), out_dtype),
        compiler_params=_cparams(("parallel", "arbitrary")),
    )(jnp.asarray(qt), jnp.asarray(kt), fast, qT, r_rows, k, vT, btab)


def _reference_rows(qn2, kn2, l_self, extra, bias_max, bias_self, T, hps):
    H, S = qn2.shape
    kmax = jnp.sqrt(jnp.max(kn2, axis=1, keepdims=True))
    bound = jnp.sqrt(qn2) * kmax + bias_max + extra
    ok = bound - (l_self + bias_self + extra) <= FAST_MARGIN
    fast = jnp.all(ok.reshape(H // hps, hps, S // T, T), axis=(1, 3)).astype(jnp.int32).reshape(-1)
    r = FAST_OVER - bound
    parts = []
    for _ in range(3):
        part = r.astype(BF16)
        parts.append(part)
        r = r - part.astype(F32)
    rows = jnp.stack(parts + [jnp.zeros_like(parts[0])] * (STAT_ROWS - 3), axis=1)
    return rows, fast


def _stat_cols(Hk, S):
    return jnp.broadcast_to((jnp.arange(STAT_ROWS) < 3).astype(BF16), (Hk, S, STAT_ROWS))


def _band_kernel(*refs, n_prev, Ta, H, shared_kv, want_lse):
    q_ref = refs[0]
    k_refs = refs[1:2 + n_prev]
    v_refs = refs[2 + n_prev:3 + 2 * n_prev]
    bt_ref = refs[3 + 2 * n_prev]
    o_ref = refs[4 + 2 * n_prev]
    lse_ref = refs[5 + 2 * n_prev] if want_lse else None
    ai = pl.program_id(1)
    k = jnp.concatenate([r[...] for r in k_refs], axis=0).astype(BF16)
    v = jnp.concatenate([r[...] for r in v_refs], axis=0).astype(BF16)
    W = (n_prev + 1) * Ta
    col = lax.broadcasted_iota(jnp.int32, (1, W), 1)
    col_valid = (ai - n_prev) * Ta + col >= 0
    q = q_ref[...]
    outs, lses = [], []
    for h in range(H):
        hk = 0 if shared_kv else h
        qh = (q[:, h * HEAD_DIM:(h + 1) * HEAD_DIM] * (HEAD_DIM ** -0.5)).astype(BF16)
        s = _dot_nt(qh, k[:, hk * HEAD_DIM:(hk + 1) * HEAD_DIM]) + bt_ref[h]
        s = jnp.where(col_valid, s, NEG_INF)
        m = jnp.max(s, axis=-1, keepdims=True)
        p = jnp.exp(s - m)
        l = jnp.sum(p, axis=-1, keepdims=True)
        outs.append(_dot(p.astype(BF16), v[:, hk * HEAD_DIM:(hk + 1) * HEAD_DIM]) / l)
        if want_lse:
            lses.append(jnp.broadcast_to(m + jnp.log(l), (Ta, HEAD_DIM)))
    o_ref[...] = jnp.concatenate(outs, axis=1)
    if want_lse:
        lse_ref[...] = jnp.concatenate(lses, axis=1)


def band_attention(q, k, v, btab, Ta, n_prev, want_lse, widths=None, col_blocks=(0, 0, 0)):
    R, A = q.shape[:2]
    QW, KW = widths if widths else (q.shape[2], k.shape[2])
    H = QW // HEAD_DIM
    shared_kv = KW == HEAD_DIM
    nA = A // Ta
    qc, kc, vc = col_blocks

    def prev_map(p, c):
        return lambda r, a: (r, jnp.maximum(a - n_prev + p, 0), c)

    cur = lambda r, a: (r, a, 0)
    k_specs = ([pl.BlockSpec((None, Ta, KW), prev_map(p, kc)) for p in range(n_prev)]
               + [pl.BlockSpec((None, Ta, KW), lambda r, a: (r, a, kc))])
    v_specs = ([pl.BlockSpec((None, Ta, KW), prev_map(p, vc)) for p in range(n_prev)]
               + [pl.BlockSpec((None, Ta, KW), lambda r, a: (r, a, vc))])
    out_shape = [jax.ShapeDtypeStruct((R, A, QW), F32)]
    out_specs = [pl.BlockSpec((None, Ta, QW), cur)]
    if want_lse:
        out_shape.append(jax.ShapeDtypeStruct((R, A, QW), F32))
        out_specs.append(pl.BlockSpec((None, Ta, QW), cur))
    res = pl.pallas_call(
        functools.partial(_band_kernel, n_prev=n_prev, Ta=Ta, H=H, shared_kv=shared_kv, want_lse=want_lse),
        grid=(R, nA),
        in_specs=[pl.BlockSpec((None, Ta, QW), lambda r, a: (r, a, qc))] + k_specs + v_specs
                 + [pl.BlockSpec(btab.shape, lambda r, a: (0, 0, 0))],
        out_specs=out_specs,
        out_shape=out_shape,
        compiler_params=_cparams(("parallel", "arbitrary")),
    )(q, *([k] * (n_prev + 1)), *([v] * (n_prev + 1)), btab)
    return res if want_lse else res[0]


def _gelu_tanh(x):
    return 0.5 * x * (1.0 + jnp.tanh(math.sqrt(2.0 / math.pi) * (x + 0.044715 * (x * x * x))))


def _compress_kernel(ks_ref, vs_ref, pk_ref, pv_ref, w1k_ref, w2k_ref, w1v_ref, w2vt_ref, kc_ref, vct_ref):
    def hidden(src, pos, w1):
        x = src[...]
        n = x.shape[0]
        first = _dot((x + pos[0:1, :]).astype(BF16), w1[0])
        second = _dot((x + pos[1:2, :]).astype(BF16), w1[1])
        return _gelu_tanh(first + pltpu.roll(second, n - 1, axis=0)).astype(BF16)

    kc_ref[...] = _dot(hidden(ks_ref, pk_ref, w1k_ref), w2k_ref[...]).astype(BF16)
    vct_ref[...] = _dot_nt(w2vt_ref[...], hidden(vs_ref, pv_ref, w1v_ref)).astype(BF16)


def nsa_compress(k_chunks, v_chunks, pos_k, pos_v, w1k, w2k, w1v, w2v_t):
    n = k_chunks.shape[0]
    return pl.pallas_call(
        _compress_kernel,
        out_shape=[jax.ShapeDtypeStruct((n, HEAD_DIM), BF16), jax.ShapeDtypeStruct((HEAD_DIM, n), BF16)],
        compiler_params=_cparams(None),
    )(k_chunks, v_chunks, pos_k, pos_v, w1k, w2k, w1v, w2v_t)


def _cmp_topk_kernel(nq_ref, ks_ref, vs_ref, kc_ref, vct_ref, ovt_ref, eye_ref,
                     ocmp_ref, qaug_ref, st_ref, kaug_ref, vst_ref, *, Tq, n_cmp, n_sel, nbg):
    q0 = pl.program_id(0) * Tq
    t = q0 + lax.broadcasted_iota(jnp.int32, (n_cmp, Tq), 1)
    cmp_end = NSA_CMP_STRIDE * lax.broadcasted_iota(jnp.int32, (n_cmp, Tq), 0) + (NSA_CMP_LEN - 1)
    c_neg = jnp.where(cmp_end <= t, 0.0, NEG_INF)
    sees_any = q0 + lax.broadcasted_iota(jnp.int32, (1, Tq), 1) >= NSA_CMP_LEN - 1
    kc = kc_ref[...]
    vct = vct_ref[...]
    q = nq_ref[...]
    eye = eye_ref[...]
    ks = ks_ref[...]
    ks_t = _dot_nt(eye, ks)
    tok = q0 + lax.broadcasted_iota(jnp.int32, (Tq, nbg), 0)
    blk = jnp.bitwise_and(jnp.right_shift(tok, NSA_SEL_BLOCK.bit_length() - 1), nbg - 1)
    onehot = jnp.where(blk == lax.broadcasted_iota(jnp.int32, (Tq, nbg), 1), 1.0, 0.0).astype(BF16)
    stat_cols = jnp.where(lax.broadcasted_iota(jnp.int32, (Tq, STAT_ROWS), 1) < 3, 1.0, 0.0).astype(BF16)
    kaug_ref[...] = jnp.concatenate([ks, onehot, stat_cols], axis=1)
    ones_row = jnp.where(lax.broadcasted_iota(jnp.int32, (STAT_ROWS, Tq), 0) < 1, 1.0, 0.0).astype(BF16)
    vst_ref[...] = jnp.concatenate([_dot_nt(eye, vs_ref[...]).astype(BF16), ones_row], axis=0)
    psum = jnp.zeros((n_cmp, Tq), F32)
    outs, q_ts, qn2, l_self = [], [], [], []
    for h in range(NSA_HEADS):
        qh = (q[:, h * HEAD_DIM:(h + 1) * HEAD_DIM] * (HEAD_DIM ** -0.5 * LOG2E)).astype(BF16)
        q_t = _dot_nt(eye, qh)
        q_ts.append(q_t.astype(BF16))
        qn2.append(jnp.sum(q_t * q_t, axis=0, keepdims=True))
        l_self.append(jnp.sum(q_t * ks_t, axis=0, keepdims=True))
        s = _dot(kc, q_ts[h]) + c_neg
        p = jnp.exp2(s - jnp.max(s, axis=0, keepdims=True))
        p = p * jnp.where(sees_any, 1.0 / jnp.sum(p, axis=0, keepdims=True), 0.0)
        psum = psum + p
        outs.append(_dot(vct, p.astype(BF16)).T)
    ocmp_ref[...] = jnp.concatenate(outs, axis=1)
    st_ref[0] = jnp.concatenate(qn2, axis=0)
    st_ref[1] = jnp.broadcast_to(jnp.sum(ks_t * ks_t, axis=0, keepdims=True), (NSA_HEADS, Tq))
    st_ref[2] = jnp.concatenate(l_self, axis=0)
    hi = psum.astype(BF16)
    lo = (psum - hi.astype(F32)).astype(BF16)
    imp = _dot(ovt_ref[...], hi) + _dot(ovt_ref[...], lo)
    t = q0 + lax.broadcasted_iota(jnp.int32, (n_sel, Tq), 1)
    j = lax.broadcasted_iota(jnp.int32, (n_sel, Tq), 0)
    start = j * NSA_SEL_BLOCK
    cur_blk = (start <= t) & (t < start + NSA_SEL_BLOCK)
    prev_blk = (start + NSA_SEL_BLOCK <= t) & (t < start + 2 * NSA_SEL_BLOCK)
    forced = (j == 0) | cur_blk | prev_blk
    score = jnp.where(forced, NSA_FORCE_SCORE, jnp.where(start <= t, imp, -NSA_FORCE_SCORE))
    jf = j.astype(F32)
    taken = jnp.float32(-3e38)
    for _ in range(min(NSA_TOPK, n_sel)):
        mx = jnp.max(score, axis=0, keepdims=True)
        first = jnp.min(jnp.where(score == mx, jf, float(n_sel)), axis=0, keepdims=True)
        score = jnp.where(jf == first, taken, score)
    sel_bias_t = jnp.where(score == taken, 0.0, SEL_NEG).astype(BF16)
    for h in range(NSA_HEADS):
        for g in range(n_sel // nbg):
            qaug_ref[g, h] = jnp.concatenate([q_ts[h], sel_bias_t[g * nbg:(g + 1) * nbg]], axis=0)


def nsa_cmp_topk(y, nq_block, k_sel, v_sel, kc, vc_t, overlap_t, Tq, nbg):
    S = y.shape[0]
    n_cmp = kc.shape[0]
    n_sel = S // NSA_SEL_BLOCK
    G = n_sel // nbg
    assert nbg & (nbg - 1) == 0 and NSA_SEL_BLOCK & (NSA_SEL_BLOCK - 1) == 0
    const = lambda a: pl.BlockSpec(a.shape, lambda i: (0,) * a.ndim)
    rows = lambda w: pl.BlockSpec((Tq, w), lambda i: (i, 0))
    eye = jnp.eye(HEAD_DIM, dtype=BF16)
    kw = HEAD_DIM + nbg + STAT_ROWS
    return pl.pallas_call(
        functools.partial(_cmp_topk_kernel, Tq=Tq, n_cmp=n_cmp, n_sel=n_sel, nbg=nbg),
        grid=(S // Tq,),
        in_specs=[pl.BlockSpec((Tq, NSA_W), lambda i: (i, nq_block)), rows(HEAD_DIM), rows(HEAD_DIM),
                  const(kc), const(vc_t), const(overlap_t), const(eye)],
        out_specs=[rows(NSA_W),
                   pl.BlockSpec((G, NSA_HEADS, HEAD_DIM + nbg, Tq), lambda i: (0, 0, 0, i)),
                   pl.BlockSpec((3, NSA_HEADS, Tq), lambda i: (0, 0, i)),
                   rows(kw),
                   pl.BlockSpec((HEAD_DIM + STAT_ROWS, Tq), lambda i: (0, i))],
        out_shape=[jax.ShapeDtypeStruct((S, NSA_W), F32),
                   jax.ShapeDtypeStruct((G, NSA_HEADS, HEAD_DIM + nbg, S), BF16),
                   jax.ShapeDtypeStruct((3, NSA_HEADS, S), F32),
                   jax.ShapeDtypeStruct((S, kw), BF16),
                   jax.ShapeDtypeStruct((HEAD_DIM + STAT_ROWS, S), BF16)],
        compiler_params=_cparams(("parallel",)),
    )(y, k_sel, v_sel, kc, vc_t, overlap_t, eye)


def _outproj_even_kernel(oa_ref, ob_ref, wa_ref, wb_ref, g_ref, h_ref, out_ref):
    m = _dot(oa_ref[...], wa_ref[...]) + _dot(ob_ref[...], wb_ref[...])
    out_ref[...] = h_ref[...] + _rms(m, g_ref[...])


def outproj_even(oa, ob, wa, wb, g, h, tm):
    S, D = h.shape
    rows = lambda w: pl.BlockSpec((tm, w), lambda i: (i, 0))
    full = lambda a: pl.BlockSpec(a.shape, lambda i: (0,) * a.ndim)
    return pl.pallas_call(
        _outproj_even_kernel,
        grid=(S // tm,),
        in_specs=[rows(oa.shape[1]), rows(ob.shape[1]), full(wa), full(wb), full(g), rows(D)],
        out_specs=rows(D),
        out_shape=jax.ShapeDtypeStruct((S, D), F32),
        compiler_params=_cparams(("parallel",)),
    )(oa, ob, wa, wb, g, h)


def _outproj_odd_kernel(o0_ref, o1_ref, o2_ref, l0_ref, l1_ref, l2_ref, oc_ref, os_ref, ow_ref,
                        gc_ref, gs_ref, gw_ref, wd_ref, wn_ref, g_ref, h_ref, out_ref, scr):
    tm = h_ref.shape[0]

    def rows_of(ref):
        if len(ref.shape) == 2:
            return ref[...]
        d, _, width = ref.shape
        for r in range(d):
            v = ref[r]
            for c in range(width // LANES):
                scr[c, pl.ds(r, tm // d, stride=d), :] = v[:, c * LANES:(c + 1) * LANES]
        return jnp.concatenate([scr[c] for c in range(width // LANES)], axis=1)

    l0, l1, l2 = rows_of(l0_ref), rows_of(l1_ref), rows_of(l2_ref)
    mx = jnp.maximum(jnp.maximum(l0, l1), l2)
    e0, e1, e2 = jnp.exp(l0 - mx), jnp.exp(l1 - mx), jnp.exp(l2 - mx)
    o_dil = (e0 * rows_of(o0_ref) + e1 * rows_of(o1_ref) + e2 * rows_of(o2_ref)) / (e0 + e1 + e2)
    o_nsa = (jax.nn.sigmoid(gc_ref[...]) * oc_ref[...] + jax.nn.sigmoid(gs_ref[...]) * os_ref[...].astype(F32)
             + jax.nn.sigmoid(gw_ref[...]) * ow_ref[...])
    m = _dot(o_dil.astype(BF16), wd_ref[...]) + _dot(o_nsa.astype(BF16), wn_ref[...])
    out_ref[...] = h_ref[...] + _rms(m, g_ref[...])


def outproj_odd(o_dil, lse_dil, o_cmp, o_sel, o_win, y, gate_block, wd, wn, g, h, tm):
    S, D = h.shape
    rows = lambda w: pl.BlockSpec((tm, w), lambda i: (i, 0))
    full = lambda a: pl.BlockSpec(a.shape, lambda i: (0,) * a.ndim)
    gate = lambda b: pl.BlockSpec((tm, NSA_W), lambda i: (i, gate_block + b))

    def dil(a):
        if a.ndim == 2:
            return rows(DIL_GW)
        return pl.BlockSpec((a.shape[0], tm // a.shape[0], DIL_GW), lambda i: (0, i, 0))

    return pl.pallas_call(
        _outproj_odd_kernel,
        grid=(S // tm,),
        in_specs=[dil(a) for a in (*o_dil, *lse_dil)] + [rows(NSA_W)] * 3
                 + [gate(0), gate(1), gate(2), full(wd), full(wn), full(g), rows(D)],
        out_specs=rows(D),
        out_shape=jax.ShapeDtypeStruct((S, D), F32),
        scratch_shapes=[pltpu.VMEM((DIL_GW // LANES, tm, LANES), F32)],
        compiler_params=_cparams(("parallel",)),
    )(*o_dil, *lse_dil, o_cmp, o_sel, o_win, y, y, y, wd, wn, g, h)


def _mlp_ple_kernel(h_ref, g1_ref, wu_ref, wd_ref, g2_ref, g3_ref, wg_ref, p_ref, wp_ref, out_ref,
                    hn_ref, acc_ref):
    j = pl.program_id(1)

    @pl.when(j == 0)
    def _():
        hn_ref[...] = _rms(h_ref[...], g1_ref[...]).astype(BF16)
        acc_ref[...] = jnp.zeros(acc_ref.shape, F32)

    u = jnp.maximum(_dot(hn_ref[...], wu_ref[...]), 0.0)
    acc_ref[...] += _dot((u * u).astype(BF16), wd_ref[...])

    @pl.when(j == pl.num_programs(1) - 1)
    def _():
        h2 = h_ref[...] + _rms(acc_ref[...], g2_ref[...])
        gate = jax.nn.sigmoid(_dot(_rms(h2, g3_ref[...]).astype(BF16), wg_ref[...]))
        out_ref[...] = h2 + gate * _dot(p_ref[...].astype(BF16), wp_ref[...])


def mlp_ple(h, g1, wu, wd, g2, g3, wg, p, wp, tm, tf):
    S, D = h.shape
    FF = wu.shape[1]
    PD = p.shape[1]
    const = lambda a: pl.BlockSpec(a.shape, lambda i, j: (0,) * a.ndim)
    return pl.pallas_call(
        _mlp_ple_kernel,
        grid=(S // tm, FF // tf),
        in_specs=[pl.BlockSpec((tm, D), lambda i, j: (i, 0)), const(g1),
                  pl.BlockSpec((D, tf), lambda i, j: (0, j)),
                  pl.BlockSpec((tf, D), lambda i, j: (j, 0)),
                  const(g2), const(g3), const(wg),
                  pl.BlockSpec((tm, PD), lambda i, j: (i, 0)), const(wp)],
        out_specs=pl.BlockSpec((tm, D), lambda i, j: (i, 0)),
        out_shape=jax.ShapeDtypeStruct((S, D), F32),
        scratch_shapes=[pltpu.VMEM((tm, D), BF16), pltpu.VMEM((tm, D), F32)],
        compiler_params=_cparams(("parallel", "arbitrary")),
    )(h, g1, wu, wd, g2, g3, wg, p, wp)


def _t5_bucket_of(dist):
    n = jnp.maximum(dist, 0)
    max_exact = T5_BUCKETS // 2
    ratio = jnp.log(jnp.maximum(n, 1).astype(F32) / max_exact) / math.log(T5_MAX_DIST / max_exact)
    large = jnp.minimum(max_exact + (ratio * (T5_BUCKETS - max_exact)).astype(jnp.int32), T5_BUCKETS - 1)
    return jnp.where(n < max_exact, n, large)


def _bias_of_dist(bias, dist):
    bucket = _t5_bucket_of(dist)[None]
    out = jnp.zeros((bias.shape[1],) + dist.shape, F32)
    for b in range(T5_BUCKETS):
        out = jnp.where(bucket == b, bias[b].reshape((-1,) + (1,) * dist.ndim), out)
    return out


def _causal_tables(T, R):
    k = jnp.arange(R * T)[None, :, None]
    q = jnp.arange(T)[None, None, :]
    e = jnp.arange(R + 1)[:, None, None]
    return jnp.where(e * T + q >= k, 0.0, NEG_INF).astype(F32)[:, None]


def _t5_delta_tables(bias, T):
    n_delta = -(-(T5_MAX_DIST - 1) // T) + 2
    k = jnp.arange(T)[None, :, None]
    q = jnp.arange(T)[None, None, :]
    dist = jnp.arange(n_delta)[:, None, None] * T + q - k
    val = (_bias_of_dist(bias, dist) - bias[T5_BUCKETS - 1].reshape(-1, 1, 1, 1)) * LOG2E
    return jnp.transpose(jnp.where(dist[None] >= 0, val, NEG_INF), (1, 0, 2, 3))


def _band_table(bias, Ta, n_prev, max_rel, stride, inclusive):
    P = n_prev * Ta
    i = jnp.arange(Ta)[:, None]
    c = jnp.arange(P + Ta)[None, :]
    rel = i + P - c
    ok = (rel >= 0) & ((rel <= max_rel) if inclusive else (rel < max_rel))
    return jnp.where(ok[None], _bias_of_dist(bias, rel * stride), NEG_INF)


def _tile(S, pref):
    t = min(pref, S)
    assert S % t == 0
    return t


def even_mixer_core(h, g_pre, w_in, cos, sin, w_uq, q_norm, w_ukv, kv_norm, forget_bias, T):
    S = h.shape[0]
    scale = MLA_QK_DIM ** -0.5 * LOG2E
    half = MLA_ROPE_DIM // 2
    cos_t, sin_t = jnp.transpose(cos), jnp.transpose(sin)
    ones = jnp.ones((MLA_NOPE_DIM, S), F32)
    cq = jnp.concatenate([ones, cos_t, cos_t], axis=0) * scale
    sq = jnp.concatenate([0.0 * ones, -sin_t, sin_t], axis=0) * scale
    ck = jnp.concatenate([cos, cos], axis=1)
    sk = jnp.concatenate([-sin, sin], axis=1)
    wq = jnp.transpose(w_uq.reshape(MLA_Q_RANK, MLA_HEADS, MLA_QK_DIM), (1, 2, 0))
    swap = np.concatenate([np.arange(MLA_NOPE_DIM), MLA_NOPE_DIM + half + np.arange(half),
                           MLA_NOPE_DIM + np.arange(half)])
    wqs = wq[:, swap, :]
    wkv = jnp.transpose(w_ukv.reshape(MLA_KV_RANK, MLA_HEADS, MLA_NOPE_DIM + MLA_V_DIM), (1, 0, 2))
    wk = jnp.pad(wkv[:, :, :MLA_NOPE_DIM], ((0, 0), (0, 0), (0, MLA_ROPE_DIM)))
    wv = jnp.transpose(wkv[:, :, MLA_NOPE_DIM:], (0, 2, 1))
    e96 = jnp.pad(jnp.eye(MLA_ROPE_DIM, dtype=F32), ((0, 0), (MLA_NOPE_DIM, 0)))
    qm, km, vm, fq, fk, fv, logf, st = even_prep(
        h, g_pre, w_in, wq.astype(BF16), wqs.astype(BF16), wk.astype(BF16), e96.astype(BF16), wv.astype(BF16),
        jnp.eye(MLA_QK_DIM, dtype=BF16), q_norm[None, :], kv_norm[None, :], forget_bias[None, :],
        cq, sq, ck, sk, _tile(S, 512))
    neg_f = cumsum_split(jnp.transpose(logf))
    hps = MLA_HEADS
    causal = _causal_tables(T, 2 if S >= 2 * T else 1)
    r_mla, fast_mla = _reference_rows(st[0], st[1], st[2], 0.0, 0.0, 0.0, T, hps)
    o_mla = flash_causal(qm[None], r_mla, km, vm, causal, fast_mla, T, hps)
    r_fox, fast_fox = _reference_rows(st[3], st[4], st[5], jnp.sum(neg_f.astype(F32), axis=0), 0.0, 0.0, T, hps)
    fk_aug = jnp.concatenate([fk, jnp.transpose(neg_f, (1, 2, 0)),
                              jnp.zeros((FOX_HEADS, S, FOX_PAD - FOX_SPLIT), BF16), _stat_cols(FOX_HEADS, S)], axis=2)
    o_fox = flash_causal(fq[None], r_fox, fk_aug, fv, causal, fast_fox, T, hps)
    return o_mla, o_fox


def _odd_deint_plan():
    G = len(DIL_GROUPS)
    return tuple(((b * G + g) * DIL_GW, DIL_GW, d) for g, (_, d) in enumerate(DIL_GROUPS) if d > 1 for b in range(3))


def odd_mixer_core(y, deint, t5_bias, pos_k, pos_v, w1k, w2k, w1v, w2v, T):
    S = y.shape[0]
    G = len(DIL_GROUPS)
    Ta = 256
    o_dil, lse_dil = [], []
    for g, (w, d) in enumerate(DIL_GROUPS):
        ta = _tile(S // d, Ta)
        assert w // d <= ta
        btab = _band_table(t5_bias[:, g * DIL_HEADS:(g + 1) * DIL_HEADS], ta, 1, w // d, d, True)
        if d == 1:
            o, lse = band_attention(y[None], y[None], y[None], btab, ta, 1, True,
                                    widths=(DIL_GW, DIL_GW), col_blocks=(g, G + g, 2 * G + g))
            o, lse = o[0], lse[0]
        else:
            q, k, v = (deint[(b * G + g) * DIL_GW] for b in range(3))
            o, lse = band_attention(q, k, v, btab, ta, 1, True)
        o_dil.append(o)
        lse_dil.append(lse)
    nq_block = 3 * G
    gate_block = nq_block + 1
    base = (gate_block + 3) * NSA_W
    k_cmp, v_cmp, k_sel, v_sel, k_win, v_win = (y[:, base + i * HEAD_DIM:base + (i + 1) * HEAD_DIM] for i in range(6))
    bias_nsa = t5_bias[:, G * DIL_HEADS:]
    n_chunk = S // NSA_CMP_STRIDE
    cw = NSA_CMP_STRIDE * HEAD_DIM
    kc, vc = nsa_compress(k_cmp.reshape(n_chunk, cw), v_cmp.reshape(n_chunk, cw),
                          pos_k.reshape(2, cw), pos_v.reshape(2, cw),
                          w1k.reshape(2, cw, -1).astype(BF16), w2k.astype(BF16),
                          w1v.reshape(2, cw, -1).astype(BF16), jnp.transpose(w2v).astype(BF16))
    n_sel = S // NSA_SEL_BLOCK
    ci = np.arange(n_chunk)[None, :] * NSA_CMP_STRIDE
    sj = np.arange(n_sel)[:, None] * NSA_SEL_BLOCK
    overlap_t = jnp.asarray(((ci < sj + NSA_SEL_BLOCK) & (ci + NSA_CMP_LEN > sj)).astype(np.float32), BF16)
    nbg = min(n_sel, NSA_SEL_GROUP_BLOCKS)
    o_cmp, qaug, st, kaug, v_sel_t = nsa_cmp_topk(y, nq_block, k_sel.astype(BF16), v_sel.astype(BF16), kc, vc,
                                                  overlap_t, _tile(S, 256), nbg)
    delta = (bias_nsa - bias_nsa[T5_BUCKETS - 1]) * LOG2E
    r_sel, fast_sel = _reference_rows(st[0], st[1, :1], st[2], 0.0, jnp.max(delta, axis=0)[:, None],
                                      delta[0][:, None], T, NSA_HEADS)
    o_sel = flash_causal(qaug, r_sel, kaug[None], v_sel_t[None], _t5_delta_tables(bias_nsa, T), fast_sel, T,
                         hps=NSA_HEADS, key_group_tokens=nbg * NSA_SEL_BLOCK)
    Tw = _tile(S, NSA_WINDOW)
    n_prev = NSA_WINDOW // Tw
    wtab = _band_table(bias_nsa, Tw, n_prev, NSA_WINDOW, 1, False)
    o_win = band_attention(y[None], k_win[None], v_win[None], wtab, Tw, n_prev, False,
                           widths=(NSA_W, HEAD_DIM), col_blocks=(nq_block, 0, 0))[0]
    return o_dil, lse_dil, o_cmp, o_sel, o_win, gate_block


def _pad_cols(w, n):
    return jnp.pad(w, ((0, 0), (0, n - w.shape[1])))


def _even_w_in(w):
    cq, ckv, kr, fq, fk, fv, fl = jnp.split(w, np.cumsum([256, 128, 32, 512, 512, 512])[:], axis=1)
    half = MLA_ROPE_DIM // 2
    kr_sw = jnp.concatenate([kr[:, half:], kr[:, :half]], axis=1)
    head = _pad_cols(jnp.concatenate([cq, ckv, kr, kr_sw, fl], axis=1), 512)
    return jnp.concatenate([head, fq, fk, fv], axis=1).astype(BF16)


def _odd_w_in(w):
    n_main = 3 * 3 * DIL_GW + NSA_W
    main, six = w[:, :n_main], w[:, n_main:n_main + 6 * HEAD_DIM]
    gl = w[:, n_main + 6 * HEAD_DIM:]
    gl = jnp.transpose(gl.reshape(-1, NSA_HEADS, 3), (0, 2, 1))
    gl = jnp.repeat(gl[..., None], HEAD_DIM, axis=-1).reshape(w.shape[0], 3 * NSA_W)
    return _pad_cols(jnp.concatenate([main, gl, six], axis=1), ODD_IN_PAD).astype(BF16)


def _trunk(x, p, positions, t5_bias, ev_w_in, ev_q_norm, ev_w_uq, ev_kv_norm, ev_w_ukv, ev_forget_bias,
           ev_w_out, od_w_in, od_cmp_pos_k, od_cmp_pos_v, od_cmp_w1_k, od_cmp_w2_k, od_cmp_w1_v, od_cmp_w2_v,
           od_w_out, norm_mix_pre, norm_mix_post, norm_mlp_pre, norm_mlp_post, w_mlp_up, w_mlp_down,
           ple_norm, w_ple_gate, w_ple_proj):
    S, D = x.shape
    depth = p.shape[0]
    T = _tile(S, 512)
    tm = _tile(S, 512)
    inv_freq = ROPE_THETA ** (-jnp.arange(0, MLA_ROPE_DIM, 2, dtype=F32) / MLA_ROPE_DIM)
    angles = positions.astype(F32)[:, None] * inv_freq
    cos, sin = jnp.cos(angles), jnp.sin(angles)
    h = x
    for i in range(depth):
        j = i // 2
        if i % 2 == 0:
            o_mla, o_fox = even_mixer_core(h, norm_mix_pre[i][None], _even_w_in(ev_w_in[j]), cos, sin, ev_w_uq[j],
                                           ev_q_norm[j], ev_w_ukv[j], ev_kv_norm[j], ev_forget_bias[j], T)
            wo = ev_w_out[j].astype(BF16)
            na = MLA_HEADS * MLA_V_DIM
            h = outproj_even(o_mla, o_fox, wo[:na], wo[na:], norm_mix_post[i][None], h, tm)
        else:
            plan = _odd_deint_plan()
            y, *parts = odd_inproj(h, norm_mix_pre[i][None], _odd_w_in(od_w_in[j]), _tile(S, 256), plan)
            o_dil, lse_dil, o_cmp, o_sel, o_win, gate_block = odd_mixer_core(
                y, {col: a for (col, _, _), a in zip(plan, parts)}, t5_bias, od_cmp_pos_k[j], od_cmp_pos_v[j], od_cmp_w1_k[j], od_cmp_w2_k[j],
                od_cmp_w1_v[j], od_cmp_w2_v[j], T)
            wo = od_w_out[j].astype(BF16)
            h = outproj_odd(o_dil, lse_dil, o_cmp, o_sel, o_win, y, gate_block, wo[:DIL_GW], wo[DIL_GW:],
                            norm_mix_post[i][None], h, tm)
        h = mlp_ple(h, norm_mlp_pre[i][None], w_mlp_up[i].astype(BF16), w_mlp_down[i].astype(BF16),
                    norm_mlp_post[i][None], ple_norm[i][None], w_ple_gate[i].astype(BF16), p[i],
                    w_ple_proj[i].astype(BF16), _tile(S, 1024), 512)
    return h


def kernel(x, p, positions, t5_bias, ev_w_in, ev_q_norm, ev_w_uq, ev_kv_norm, ev_w_ukv, ev_forget_bias, ev_w_out, od_w_in, od_cmp_pos_k, od_cmp_pos_v, od_cmp_w1_k, od_cmp_w2_k, od_cmp_w1_v, od_cmp_w2_v, od_w_out, norm_mix_pre, norm_mix_post, norm_mlp_pre, norm_mlp_post, w_mlp_up, w_mlp_down, ple_norm, w_ple_gate, w_ple_proj):
    params = (t5_bias, ev_w_in, ev_q_norm, ev_w_uq, ev_kv_norm, ev_w_ukv, ev_forget_bias, ev_w_out, od_w_in,
              od_cmp_pos_k, od_cmp_pos_v, od_cmp_w1_k, od_cmp_w2_k, od_cmp_w1_v, od_cmp_w2_v, od_w_out,
              norm_mix_pre, norm_mix_post, norm_mlp_pre, norm_mlp_post, w_mlp_up, w_mlp_down, ple_norm,
              w_ple_gate, w_ple_proj)
    outs = [_trunk(x[b], p[:, b], positions[b], *params) for b in range(x.shape[0])]
    return jnp.stack(outs).astype(x.dtype)
```

```python
import functools
import math

import numpy as np
import jax
import jax.numpy as jnp
from jax import lax
from jax.experimental import pallas as pl
from jax.experimental.pallas import tpu as pltpu

F32 = jnp.float32
BF16 = jnp.bfloat16

HEAD_DIM = 64
RMS_EPS = 1e-6
NEG_INF = -1e30
SEL_NEG = -(2.0 ** 99)

MLA_HEADS = 8
MLA_NOPE_DIM = 64
MLA_ROPE_DIM = 32
MLA_V_DIM = 64
MLA_Q_RANK = 256
MLA_KV_RANK = 128
MLA_QK_DIM = MLA_NOPE_DIM + MLA_ROPE_DIM
ROPE_THETA = 10000.0
FOX_HEADS = 8
FOX_SPLIT = 3
FOX_PAD = 16
LOG2E = math.log2(math.e)
STAT_ROWS = 16
FAST_OVER = 64.0
FAST_MARGIN = 150.0
P_CHUNK = 32
BAND_SUB = 128
DIL_GROUPS = ((128, 1), (512, 4), (2048, 16))
DIL_HEADS = 4
DIL_GW = DIL_HEADS * HEAD_DIM
NSA_HEADS = 4
NSA_W = NSA_HEADS * HEAD_DIM
NSA_CMP_LEN = 32
NSA_CMP_STRIDE = 16
NSA_SEL_BLOCK = 64
NSA_TOPK = 16
NSA_WINDOW = 512
NSA_FORCE_SCORE = 1e9
NSA_SEL_GROUP_BLOCKS = 128
T5_BUCKETS = 32
T5_MAX_DIST = 2048

EVEN_IN_PAD = 2048
ODD_IN_PAD = 3840

VMEM_LIMIT_BYTES = 56 * 1024 * 1024


def _cparams(sem):
    return pltpu.CompilerParams(dimension_semantics=sem, vmem_limit_bytes=VMEM_LIMIT_BYTES)


def _rms(x, g):
    return x * lax.rsqrt(jnp.mean(x * x, axis=-1, keepdims=True) + RMS_EPS) * g


def _dot(a, b):
    return jnp.dot(a, b, preferred_element_type=F32)


def _dot_nt(a, b):
    return lax.dot_general(a, b, (((1,), (1,)), ((), ())), preferred_element_type=F32)


LANES = 128


def _odd_inproj_kernel(h_ref, g_ref, w_ref, y_ref, *rest, deint):
    out_refs, scr = rest[:-1], rest[-1]
    y = _dot(_rms(h_ref[...], g_ref[...]).astype(BF16), w_ref[...])
    y_ref[...] = y
    tm = y.shape[0]
    for out, (col, width, d) in zip(out_refs, deint):
        n = width // LANES
        for c in range(n):
            scr[c] = y[:, col + c * LANES:col + (c + 1) * LANES]
        for r in range(d):
            out[r] = jnp.concatenate([scr[c, pl.ds(r, tm // d, stride=d), :] for c in range(n)], axis=1)


def odd_inproj(h, g, w, tm, deint):
    S, D = h.shape
    N = w.shape[1]
    out_shape = [jax.ShapeDtypeStruct((S, N), F32)]
    out_specs = [pl.BlockSpec((tm, N), lambda i: (i, 0))]
    for _, width, d in deint:
        out_shape.append(jax.ShapeDtypeStruct((d, S // d, width), F32))
        out_specs.append(pl.BlockSpec((d, tm // d, width), lambda i: (0, i, 0)))
    return pl.pallas_call(
        functools.partial(_odd_inproj_kernel, deint=deint),
        grid=(S // tm,),
        in_specs=[pl.BlockSpec((tm, D), lambda i: (i, 0)),
                  pl.BlockSpec((1, D), lambda i: (0, 0)),
                  pl.BlockSpec((D, N), lambda i: (0, 0))],
        out_specs=out_specs,
        out_shape=out_shape,
        scratch_shapes=[pltpu.VMEM((max(wd for _, wd, _ in deint) // LANES, tm, LANES), F32)],
        compiler_params=_cparams(("parallel",)),
    )(h, g, w)


def _even_prep_kernel(h_ref, g_ref, win_ref, wq_ref, wqs_ref, wk_ref, e_ref, wv_ref, eye_ref, qn_ref, kvn_ref,
                      fb_ref, cq_ref, sq_ref, ck_ref, sk_ref,
                      qm_ref, km_ref, vm_ref, fq_ref, fk_ref, fv_ref, lf_ref, st_ref):
    y = _dot(_rms(h_ref[...], g_ref[...]).astype(BF16), win_ref[...])
    tm = y.shape[0]
    c_q = y[:, 0:MLA_Q_RANK]
    c_kv = y[:, MLA_Q_RANK:MLA_Q_RANK + MLA_KV_RANK]
    o = MLA_Q_RANK + MLA_KV_RANK
    k_r = y[:, o:o + MLA_ROPE_DIM]
    k_rs = y[:, o + MLA_ROPE_DIM:o + 2 * MLA_ROPE_DIM]
    f_logit = y[:, o + 2 * MLA_ROPE_DIM:o + 2 * MLA_ROPE_DIM + FOX_HEADS]
    nq = _rms(c_q, qn_ref[...]).astype(BF16)
    nkv = _rms(c_kv, kvn_ref[...]).astype(BF16)
    k_rot = (k_r * ck_ref[...] + k_rs * sk_ref[...]).astype(BF16)
    k_rot96 = _dot(k_rot, e_ref[...])
    cq = cq_ref[...]
    sq = sq_ref[...]
    eye = eye_ref[...]
    row = lax.broadcasted_iota(jnp.int32, (STAT_ROWS, tm), 0)
    ones_row = jnp.where(row < 1, 1.0, 0.0).astype(BF16)
    col = lax.broadcasted_iota(jnp.int32, (tm, STAT_ROWS), 1)
    stat_cols = jnp.where(col < 3, 1.0, 0.0).astype(BF16)
    stats = [[] for _ in range(6)]

    def record(slot, q_t, k):
        q_t = q_t.astype(F32)
        k_t = _dot_nt(eye[:k.shape[1], :k.shape[1]], k)
        stats[slot].append(jnp.sum(q_t * q_t, axis=0, keepdims=True))
        stats[slot + 1].append(jnp.sum(k_t * k_t, axis=0, keepdims=True))
        stats[slot + 2].append(jnp.sum(q_t * k_t, axis=0, keepdims=True))

    for h in range(MLA_HEADS):
        q = (_dot_nt(wq_ref[h], nq) * cq + _dot_nt(wqs_ref[h], nq) * sq).astype(BF16)
        k = (_dot(nkv, wk_ref[h]) + k_rot96).astype(BF16)
        qm_ref[h] = q
        km_ref[h] = jnp.concatenate([k, stat_cols], axis=1)
        vm_ref[h] = jnp.concatenate([_dot_nt(wv_ref[h], nkv).astype(BF16), ones_row], axis=0)
        record(0, q, k)
    base = 512
    fw = FOX_HEADS * HEAD_DIM
    eye64 = eye[:HEAD_DIM, :HEAD_DIM]
    ones_rows = jnp.where(row < FOX_SPLIT, 1.0, 0.0).astype(BF16)
    for h in range(FOX_HEADS):
        lo = base + h * HEAD_DIM
        fq = _dot_nt(eye64, (y[:, lo:lo + HEAD_DIM] * (HEAD_DIM ** -0.5 * LOG2E)).astype(BF16)).astype(BF16)
        fk = y[:, lo + fw:lo + fw + HEAD_DIM].astype(BF16)
        fq_ref[h] = jnp.concatenate([fq, ones_rows], axis=0)
        fk_ref[h] = fk
        fv = y[:, lo + 2 * fw:lo + 2 * fw + HEAD_DIM].astype(BF16)
        fv_ref[h] = jnp.concatenate([_dot_nt(eye64, fv).astype(BF16), ones_row], axis=0)
        record(3, fq, fk)
    for i in range(6):
        st_ref[i] = jnp.concatenate(stats[i], axis=0)
    z = f_logit + fb_ref[...]
    lf_ref[...] = jnp.minimum(z, 0.0) - jnp.log1p(jnp.exp(-jnp.abs(z)))


def even_prep(h, g, w_in, wq, wqs, wk, e96, wv, eye, qn, kvn, fb, cq, sq, ck, sk, tm):
    S, D = h.shape
    full = lambda a: pl.BlockSpec(a.shape, lambda i: (0,) * a.ndim)
    rows = lambda w: pl.BlockSpec((tm, w), lambda i: (i, 0))
    cols = lambda w: pl.BlockSpec((w, tm), lambda i: (0, i))
    heads = lambda n, w: pl.BlockSpec((n, tm, w), lambda i: (0, i, 0))
    heads_t = lambda n, w: pl.BlockSpec((n, w, tm), lambda i: (0, 0, i))
    out_shape = [jax.ShapeDtypeStruct((MLA_HEADS, MLA_QK_DIM, S), BF16),
                 jax.ShapeDtypeStruct((MLA_HEADS, S, MLA_QK_DIM + STAT_ROWS), BF16),
                 jax.ShapeDtypeStruct((MLA_HEADS, MLA_V_DIM + STAT_ROWS, S), BF16),
                 jax.ShapeDtypeStruct((FOX_HEADS, HEAD_DIM + FOX_PAD, S), BF16),
                 jax.ShapeDtypeStruct((FOX_HEADS, S, HEAD_DIM), BF16),
                 jax.ShapeDtypeStruct((FOX_HEADS, HEAD_DIM + STAT_ROWS, S), BF16),
                 jax.ShapeDtypeStruct((S, FOX_HEADS), F32),
                 jax.ShapeDtypeStruct((6, MLA_HEADS, S), F32)]
    out_specs = [heads_t(MLA_HEADS, MLA_QK_DIM), heads(MLA_HEADS, MLA_QK_DIM + STAT_ROWS),
                 heads_t(MLA_HEADS, MLA_V_DIM + STAT_ROWS),
                 heads_t(FOX_HEADS, HEAD_DIM + FOX_PAD), heads(FOX_HEADS, HEAD_DIM),
                 heads_t(FOX_HEADS, HEAD_DIM + STAT_ROWS),
                 rows(FOX_HEADS), heads_t(6, MLA_HEADS)]
    return pl.pallas_call(
        _even_prep_kernel,
        grid=(S // tm,),
        in_specs=[rows(D), full(g), full(w_in), full(wq), full(wqs), full(wk), full(e96), full(wv), full(eye),
                  full(qn), full(kvn), full(fb), cols(MLA_QK_DIM), cols(MLA_QK_DIM), rows(MLA_ROPE_DIM),
                  rows(MLA_ROPE_DIM)],
        out_specs=out_specs,
        out_shape=out_shape,
        compiler_params=_cparams(("parallel",)),
    )(h, g, w_in, wq, wqs, wk, e96, wv, eye, qn, kvn, fb, cq, sq, ck, sk)


def _cumsum_split_kernel(x_ref, o_ref):
    x = x_ref[...]
    n = x.shape[1]
    lane = lax.broadcasted_iota(jnp.int32, x.shape, 1)
    shift = 1
    while shift < n:
        x = x + jnp.where(lane >= shift, pltpu.roll(x, shift, axis=1), 0.0)
        shift *= 2
    r = -x * LOG2E
    for i in range(FOX_SPLIT):
        part = r.astype(BF16)
        o_ref[i] = part
        r = r - part.astype(F32)


def cumsum_split(x):
    return pl.pallas_call(
        _cumsum_split_kernel,
        out_shape=jax.ShapeDtypeStruct((FOX_SPLIT,) + x.shape, BF16),
        compiler_params=_cparams(None),
    )(x)


def _flash_kernel(qt_ref, kt_ref, fast_ref, q_ref, r_ref, k_ref, v_ref, bt_ref, o_ref, m_scr, acc_scr, p_scr, *,
                  hps, shared_kv, n_delta, nt, R):
    step = pl.program_id(1)
    qi = qt_ref[step]
    ki = kt_ref[step]
    fast = fast_ref[pl.program_id(0) * nt + qi] != 0
    near = qi - R * ki < n_delta - 1

    @pl.when(ki == 0)
    def _():
        m_scr[...] = jnp.full(m_scr.shape, -3e38, F32)
        acc_scr[...] = jnp.zeros(acc_scr.shape, F32)

    hb = bt_ref.shape[1]
    Tk = k_ref.shape[1]
    Dv = v_ref.shape[1] - STAT_ROWS

    def logits(h, with_table):
        q = jnp.concatenate([q_ref[0, h], r_ref[h]], axis=0)
        s = _dot(k_ref[0 if shared_kv else h], q)
        return s + bt_ref[0, h if hb > 1 else 0] if with_table else s

    def store_p(h, s, m):
        for c in range(Tk // P_CHUNK):
            rows = slice(c * P_CHUNK, (c + 1) * P_CHUNK)
            x = s[rows] if m is None else s[rows] - m
            p_scr[h, rows, :] = jnp.exp2(x).astype(BF16)

    def fast_update(with_table):
        acc_prev = [acc_scr[h] for h in range(hps)]
        for h in range(hps):
            store_p(h, logits(h, with_table), None)
        for h in range(hps):
            acc_scr[h] = acc_prev[h] + _dot(v_ref[0 if shared_kv else h], p_scr[h])

    def safe_update(with_table):
        m_prev = [m_scr[h] for h in range(hps)]
        acc_prev = [acc_scr[h] for h in range(hps)]
        ss = [logits(h, with_table) for h in range(hps)]
        ms = [jnp.maximum(m_prev[h], jnp.max(ss[h], axis=0, keepdims=True)) for h in range(hps)]
        for h in range(hps):
            store_p(h, ss[h], ms[h])
        for h in range(hps):
            alpha = jnp.exp2(m_prev[h] - ms[h])
            acc_scr[h] = alpha * acc_prev[h] + _dot(v_ref[0 if shared_kv else h], p_scr[h])
            m_scr[h] = ms[h]

    for take_fast, update in ((True, fast_update), (False, safe_update)):
        for with_table in (True, False):
            cond = jnp.logical_and(fast == take_fast, near == with_table)
            pl.when(cond)(functools.partial(update, with_table))

    @pl.when(ki == qi // R)
    def _():
        outs = [(acc_scr[h, :Dv, :] / acc_scr[h, Dv:Dv + 1, :]).T for h in range(hps)]
        o_ref[...] = jnp.concatenate(outs, axis=1).astype(o_ref.dtype)


def flash_causal(qT, r_rows, k, vT, btab, fast, T, hps, key_group_tokens=None, out_dtype=BF16):
    G, H, Dq, S = qT.shape
    Dk = Dq + STAT_ROWS
    Hk, Dv, _ = vT.shape
    Dv -= STAT_ROWS
    shared_kv = Hk == 1
    n_delta, Hb, Tk = btab.shape[0], btab.shape[1], btab.shape[2]
    R = Tk // T
    nt = S // T
    qt = np.array([i for i in range(nt) for _ in range(i // R + 1)], np.int32)
    kt = np.array([j for i in range(nt) for j in range(i // R + 1)], np.int32)
    kv_h = 1 if shared_kv else hps
    assert G == 1 or R == 1
    tiles_per_group = (key_group_tokens // T) if G > 1 else 1

    def q_map(g, s, qt, kt, fast):
        return (kt[s] // tiles_per_group if G > 1 else 0, g, 0, qt[s])

    def bt_map(g, s, qt, kt, fast):
        return (jnp.minimum(qt[s] - R * kt[s], n_delta - 1), g if Hb > hps else 0, 0, 0)

    in_specs = [pl.BlockSpec((1, hps, Dq, T), q_map),
                pl.BlockSpec((hps, STAT_ROWS, T), lambda g, s, qt, kt, fast: (g, 0, qt[s])),
                pl.BlockSpec((kv_h, Tk, Dk), lambda g, s, qt, kt, fast: (0 if shared_kv else g, kt[s], 0)),
                pl.BlockSpec((kv_h, Dv + STAT_ROWS, Tk), lambda g, s, qt, kt, fast: (0 if shared_kv else g, 0, kt[s])),
                pl.BlockSpec((1, min(Hb, hps), Tk, T), bt_map)]
    grid_spec = pltpu.PrefetchScalarGridSpec(
        num_scalar_prefetch=3,
        grid=(H // hps, len(qt)),
        in_specs=in_specs,
        out_specs=pl.BlockSpec((T, hps * Dv), lambda g, s, qt, kt, fast: (qt[s], g)),
        scratch_shapes=[pltpu.VMEM((hps, 1, T), F32), pltpu.VMEM((hps, Dv + STAT_ROWS, T), F32),
                        pltpu.VMEM((hps, Tk, T), BF16)],
    )
    return pl.pallas_call(
        functools.partial(_flash_kernel, hps=hps, shared_kv=shared_kv, n_delta=n_delta, nt=nt, R=R),
        grid_spec=grid_spec,
        out_shape=jax.ShapeDtypeStruct((S, H * Dv), out_dtype),
        compiler_params=_cparams(("parallel", "arbitrary")),
    )(jnp.asarray(qt), jnp.asarray(kt), fast, qT, r_rows, k, vT, btab)


def _reference_rows(qn2, kn2, l_self, extra, bias_max, bias_self, T, hps):
    H, S = qn2.shape
    kmax = jnp.sqrt(jnp.max(kn2, axis=1, keepdims=True))
    bound = jnp.sqrt(qn2) * kmax + bias_max + extra
    ok = bound - (l_self + bias_self + extra) <= FAST_MARGIN
    fast = jnp.all(ok.reshape(H // hps, hps, S // T, T), axis=(1, 3)).astype(jnp.int32).reshape(-1)
    r = FAST_OVER - bound
    parts = []
    for _ in range(3):
        part = r.astype(BF16)
        parts.append(part)
        r = r - part.astype(F32)
    rows = jnp.stack(parts + [jnp.zeros_like(parts[0])] * (STAT_ROWS - 3), axis=1)
    return rows, fast


def _stat_cols(Hk, S):
    return jnp.broadcast_to((jnp.arange(STAT_ROWS) < 3).astype(BF16), (Hk, S, STAT_ROWS))


def _band_kernel(*refs, n_prev, Ta, H, shared_kv, want_lse):
    q_ref = refs[0]
    k_refs = refs[1:2 + n_prev]
    v_refs = refs[2 + n_prev:3 + 2 * n_prev]
    bt_ref = refs[3 + 2 * n_prev]
    o_ref = refs[4 + 2 * n_prev]
    lse_ref = refs[5 + 2 * n_prev] if want_lse else None
    ai = pl.program_id(1)
    k = jnp.concatenate([r[...] for r in k_refs], axis=0).astype(BF16)
    v = jnp.concatenate([r[...] for r in v_refs], axis=0).astype(BF16)
    sb = bt_ref.shape[1]
    P = bt_ref.shape[2] - sb
    q = q_ref[...]
    for s_i in range(Ta // sb):
        first = n_prev * Ta + s_i * sb - P
        col = lax.broadcasted_iota(jnp.int32, (1, P + sb), 1)
        col_valid = (ai - n_prev) * Ta + first + col >= 0
        outs, lses = [], []
        for h in range(H):
            hk = 0 if shared_kv else h
            qh = (q[s_i * sb:(s_i + 1) * sb, h * HEAD_DIM:(h + 1) * HEAD_DIM] * (HEAD_DIM ** -0.5)).astype(BF16)
            s = _dot_nt(qh, k[first:first + P + sb, hk * HEAD_DIM:(hk + 1) * HEAD_DIM]) + bt_ref[h]
            s = jnp.where(col_valid, s, NEG_INF)
            m = jnp.max(s, axis=-1, keepdims=True)
            p = jnp.exp(s - m)
            l = jnp.sum(p, axis=-1, keepdims=True)
            outs.append(_dot(p.astype(BF16), v[first:first + P + sb, hk * HEAD_DIM:(hk + 1) * HEAD_DIM]) / l)
            if want_lse:
                lses.append(jnp.broadcast_to(m + jnp.log(l), (sb, HEAD_DIM)))
        o_ref[s_i * sb:(s_i + 1) * sb, :] = jnp.concatenate(outs, axis=1)
        if want_lse:
            lse_ref[s_i * sb:(s_i + 1) * sb, :] = jnp.concatenate(lses, axis=1)


def band_attention(q, k, v, btab, Ta, n_prev, want_lse, widths=None, col_blocks=(0, 0, 0)):
    R, A = q.shape[:2]
    QW, KW = widths if widths else (q.shape[2], k.shape[2])
    H = QW // HEAD_DIM
    shared_kv = KW == HEAD_DIM
    nA = A // Ta
    qc, kc, vc = col_blocks

    def prev_map(p, c):
        return lambda r, a: (r, jnp.maximum(a - n_prev + p, 0), c)

    cur = lambda r, a: (r, a, 0)
    k_specs = ([pl.BlockSpec((None, Ta, KW), prev_map(p, kc)) for p in range(n_prev)]
               + [pl.BlockSpec((None, Ta, KW), lambda r, a: (r, a, kc))])
    v_specs = ([pl.BlockSpec((None, Ta, KW), prev_map(p, vc)) for p in range(n_prev)]
               + [pl.BlockSpec((None, Ta, KW), lambda r, a: (r, a, vc))])
    out_shape = [jax.ShapeDtypeStruct((R, A, QW), F32)]
    out_specs = [pl.BlockSpec((None, Ta, QW), cur)]
    if want_lse:
        out_shape.append(jax.ShapeDtypeStruct((R, A, QW), F32))
        out_specs.append(pl.BlockSpec((None, Ta, QW), cur))
    res = pl.pallas_call(
        functools.partial(_band_kernel, n_prev=n_prev, Ta=Ta, H=H, shared_kv=shared_kv, want_lse=want_lse),
        grid=(R, nA),
        in_specs=[pl.BlockSpec((None, Ta, QW), lambda r, a: (r, a, qc))] + k_specs + v_specs
                 + [pl.BlockSpec(btab.shape, lambda r, a: (0, 0, 0))],
        out_specs=out_specs,
        out_shape=out_shape,
        compiler_params=_cparams(("parallel", "arbitrary")),
    )(q, *([k] * (n_prev + 1)), *([v] * (n_prev + 1)), btab)
    return res if want_lse else res[0]


def _gelu_tanh(x):
    return 0.5 * x * (1.0 + jnp.tanh(math.sqrt(2.0 / math.pi) * (x + 0.044715 * (x * x * x))))


def _compress_kernel(ks_ref, vs_ref, pk_ref, pv_ref, w1k_ref, w2k_ref, w1v_ref, w2vt_ref, kc_ref, vct_ref):
    def hidden(src, pos, w1):
        x = src[...]
        n = x.shape[0]
        first = _dot((x + pos[0:1, :]).astype(BF16), w1[0])
        second = _dot((x + pos[1:2, :]).astype(BF16), w1[1])
        return _gelu_tanh(first + pltpu.roll(second, n - 1, axis=0)).astype(BF16)

    kc_ref[...] = _dot(hidden(ks_ref, pk_ref, w1k_ref), w2k_ref[...]).astype(BF16)
    vct_ref[...] = _dot_nt(w2vt_ref[...], hidden(vs_ref, pv_ref, w1v_ref)).astype(BF16)


def nsa_compress(k_chunks, v_chunks, pos_k, pos_v, w1k, w2k, w1v, w2v_t):
    n = k_chunks.shape[0]
    return pl.pallas_call(
        _compress_kernel,
        out_shape=[jax.ShapeDtypeStruct((n, HEAD_DIM), BF16), jax.ShapeDtypeStruct((HEAD_DIM, n), BF16)],
        compiler_params=_cparams(None),
    )(k_chunks, v_chunks, pos_k, pos_v, w1k, w2k, w1v, w2v_t)


def _cmp_topk_kernel(nq_ref, ks_ref, vs_ref, kc_ref, vct_ref, ovt_ref, eye_ref,
                     ocmp_ref, qaug_ref, st_ref, kaug_ref, vst_ref, *, Tq, n_cmp, n_sel, nbg):
    q0 = pl.program_id(0) * Tq
    t = q0 + lax.broadcasted_iota(jnp.int32, (n_cmp, Tq), 1)
    cmp_end = NSA_CMP_STRIDE * lax.broadcasted_iota(jnp.int32, (n_cmp, Tq), 0) + (NSA_CMP_LEN - 1)
    c_neg = jnp.where(cmp_end <= t, 0.0, NEG_INF)
    sees_any = q0 + lax.broadcasted_iota(jnp.int32, (1, Tq), 1) >= NSA_CMP_LEN - 1
    kc = kc_ref[...]
    vct = vct_ref[...]
    q = nq_ref[...]
    eye = eye_ref[...]
    ks = ks_ref[...]
    ks_t = _dot_nt(eye, ks)
    tok = q0 + lax.broadcasted_iota(jnp.int32, (Tq, nbg), 0)
    blk = jnp.bitwise_and(jnp.right_shift(tok, NSA_SEL_BLOCK.bit_length() - 1), nbg - 1)
    onehot = jnp.where(blk == lax.broadcasted_iota(jnp.int32, (Tq, nbg), 1), 1.0, 0.0).astype(BF16)
    stat_cols = jnp.where(lax.broadcasted_iota(jnp.int32, (Tq, STAT_ROWS), 1) < 3, 1.0, 0.0).astype(BF16)
    kaug_ref[...] = jnp.concatenate([ks, onehot, stat_cols], axis=1)
    ones_row = jnp.where(lax.broadcasted_iota(jnp.int32, (STAT_ROWS, Tq), 0) < 1, 1.0, 0.0).astype(BF16)
    vst_ref[...] = jnp.concatenate([_dot_nt(eye, vs_ref[...]).astype(BF16), ones_row], axis=0)
    psum = jnp.zeros((n_cmp, Tq), F32)
    outs, q_ts, qn2, l_self = [], [], [], []
    for h in range(NSA_HEADS):
        qh = (q[:, h * HEAD_DIM:(h + 1) * HEAD_DIM] * (HEAD_DIM ** -0.5 * LOG2E)).astype(BF16)
        q_t = _dot_nt(eye, qh)
        q_ts.append(q_t.astype(BF16))
        qn2.append(jnp.sum(q_t * q_t, axis=0, keepdims=True))
        l_self.append(jnp.sum(q_t * ks_t, axis=0, keepdims=True))
        s = _dot(kc, q_ts[h]) + c_neg
        p = jnp.exp2(s - jnp.max(s, axis=0, keepdims=True))
        p = p * jnp.where(sees_any, 1.0 / jnp.sum(p, axis=0, keepdims=True), 0.0)
        psum = psum + p
        outs.append(_dot(vct, p.astype(BF16)).T)
    ocmp_ref[...] = jnp.concatenate(outs, axis=1)
    st_ref[0] = jnp.concatenate(qn2, axis=0)
    st_ref[1] = jnp.broadcast_to(jnp.sum(ks_t * ks_t, axis=0, keepdims=True), (NSA_HEADS, Tq))
    st_ref[2] = jnp.concatenate(l_self, axis=0)
    hi = psum.astype(BF16)
    lo = (psum - hi.astype(F32)).astype(BF16)
    imp = _dot(ovt_ref[...], hi) + _dot(ovt_ref[...], lo)
    t = q0 + lax.broadcasted_iota(jnp.int32, (n_sel, Tq), 1)
    j = lax.broadcasted_iota(jnp.int32, (n_sel, Tq), 0)
    start = j * NSA_SEL_BLOCK
    cur_blk = (start <= t) & (t < start + NSA_SEL_BLOCK)
    prev_blk = (start + NSA_SEL_BLOCK <= t) & (t < start + 2 * NSA_SEL_BLOCK)
    forced = (j == 0) | cur_blk | prev_blk
    score = jnp.where(forced, NSA_FORCE_SCORE, jnp.where(start <= t, imp, -NSA_FORCE_SCORE))
    jf = j.astype(F32)
    taken = jnp.float32(-3e38)
    for _ in range(min(NSA_TOPK, n_sel)):
        mx = jnp.max(score, axis=0, keepdims=True)
        first = jnp.min(jnp.where(score == mx, jf, float(n_sel)), axis=0, keepdims=True)
        score = jnp.where(jf == first, taken, score)
    sel_bias_t = jnp.where(score == taken, 0.0, SEL_NEG).astype(BF16)
    for h in range(NSA_HEADS):
        for g in range(n_sel // nbg):
            qaug_ref[g, h] = jnp.concatenate([q_ts[h], sel_bias_t[g * nbg:(g + 1) * nbg]], axis=0)


def nsa_cmp_topk(y, nq_block, k_sel, v_sel, kc, vc_t, overlap_t, Tq, nbg):
    S = y.shape[0]
    n_cmp = kc.shape[0]
    n_sel = S // NSA_SEL_BLOCK
    G = n_sel // nbg
    assert nbg & (nbg - 1) == 0 and NSA_SEL_BLOCK & (NSA_SEL_BLOCK - 1) == 0
    const = lambda a: pl.BlockSpec(a.shape, lambda i: (0,) * a.ndim)
    rows = lambda w: pl.BlockSpec((Tq, w), lambda i: (i, 0))
    eye = jnp.eye(HEAD_DIM, dtype=BF16)
    kw = HEAD_DIM + nbg + STAT_ROWS
    return pl.pallas_call(
        functools.partial(_cmp_topk_kernel, Tq=Tq, n_cmp=n_cmp, n_sel=n_sel, nbg=nbg),
        grid=(S // Tq,),
        in_specs=[pl.BlockSpec((Tq, NSA_W), lambda i: (i, nq_block)), rows(HEAD_DIM), rows(HEAD_DIM),
                  const(kc), const(vc_t), const(overlap_t), const(eye)],
        out_specs=[rows(NSA_W),
                   pl.BlockSpec((G, NSA_HEADS, HEAD_DIM + nbg, Tq), lambda i: (0, 0, 0, i)),
                   pl.BlockSpec((3, NSA_HEADS, Tq), lambda i: (0, 0, i)),
                   rows(kw),
                   pl.BlockSpec((HEAD_DIM + STAT_ROWS, Tq), lambda i: (0, i))],
        out_shape=[jax.ShapeDtypeStruct((S, NSA_W), F32),
                   jax.ShapeDtypeStruct((G, NSA_HEADS, HEAD_DIM + nbg, S), BF16),
                   jax.ShapeDtypeStruct((3, NSA_HEADS, S), F32),
                   jax.ShapeDtypeStruct((S, kw), BF16),
                   jax.ShapeDtypeStruct((HEAD_DIM + STAT_ROWS, S), BF16)],
        compiler_params=_cparams(("parallel",)),
    )(y, k_sel, v_sel, kc, vc_t, overlap_t, eye)


def _outproj_even_kernel(oa_ref, ob_ref, wa_ref, wb_ref, g_ref, h_ref, out_ref):
    m = _dot(oa_ref[...], wa_ref[...]) + _dot(ob_ref[...], wb_ref[...])
    out_ref[...] = h_ref[...] + _rms(m, g_ref[...])


def outproj_even(oa, ob, wa, wb, g, h, tm):
    S, D = h.shape
    rows = lambda w: pl.BlockSpec((tm, w), lambda i: (i, 0))
    full = lambda a: pl.BlockSpec(a.shape, lambda i: (0,) * a.ndim)
    return pl.pallas_call(
        _outproj_even_kernel,
        grid=(S // tm,),
        in_specs=[rows(oa.shape[1]), rows(ob.shape[1]), full(wa), full(wb), full(g), rows(D)],
        out_specs=rows(D),
        out_shape=jax.ShapeDtypeStruct((S, D), F32),
        compiler_params=_cparams(("parallel",)),
    )(oa, ob, wa, wb, g, h)


def _outproj_odd_kernel(o0_ref, o1_ref, o2_ref, l0_ref, l1_ref, l2_ref, oc_ref, os_ref, ow_ref,
                        gc_ref, gs_ref, gw_ref, wd_ref, wn_ref, g_ref, h_ref, out_ref, scr):
    tm = h_ref.shape[0]

    def rows_of(ref):
        if len(ref.shape) == 2:
            return ref[...]
        d, _, width = ref.shape
        for r in range(d):
            v = ref[r]
            for c in range(width // LANES):
                scr[c, pl.ds(r, tm // d, stride=d), :] = v[:, c * LANES:(c + 1) * LANES]
        return jnp.concatenate([scr[c] for c in range(width // LANES)], axis=1)

    l0, l1, l2 = rows_of(l0_ref), rows_of(l1_ref), rows_of(l2_ref)
    mx = jnp.maximum(jnp.maximum(l0, l1), l2)
    e0, e1, e2 = jnp.exp(l0 - mx), jnp.exp(l1 - mx), jnp.exp(l2 - mx)
    o_dil = (e0 * rows_of(o0_ref) + e1 * rows_of(o1_ref) + e2 * rows_of(o2_ref)) / (e0 + e1 + e2)
    o_nsa = (jax.nn.sigmoid(gc_ref[...]) * oc_ref[...] + jax.nn.sigmoid(gs_ref[...]) * os_ref[...].astype(F32)
             + jax.nn.sigmoid(gw_ref[...]) * ow_ref[...])
    m = _dot(o_dil.astype(BF16), wd_ref[...]) + _dot(o_nsa.astype(BF16), wn_ref[...])
    out_ref[...] = h_ref[...] + _rms(m, g_ref[...])


def outproj_odd(o_dil, lse_dil, o_cmp, o_sel, o_win, y, gate_block, wd, wn, g, h, tm):
    S, D = h.shape
    rows = lambda w: pl.BlockSpec((tm, w), lambda i: (i, 0))
    full = lambda a: pl.BlockSpec(a.shape, lambda i: (0,) * a.ndim)
    gate = lambda b: pl.BlockSpec((tm, NSA_W), lambda i: (i, gate_block + b))

    def dil(a):
        if a.ndim == 2:
            return rows(DIL_GW)
        return pl.BlockSpec((a.shape[0], tm // a.shape[0], DIL_GW), lambda i: (0, i, 0))

    return pl.pallas_call(
        _outproj_odd_kernel,
        grid=(S // tm,),
        in_specs=[dil(a) for a in (*o_dil, *lse_dil)] + [rows(NSA_W)] * 3
                 + [gate(0), gate(1), gate(2), full(wd), full(wn), full(g), rows(D)],
        out_specs=rows(D),
        out_shape=jax.ShapeDtypeStruct((S, D), F32),
        scratch_shapes=[pltpu.VMEM((DIL_GW // LANES, tm, LANES), F32)],
        compiler_params=_cparams(("parallel",)),
    )(*o_dil, *lse_dil, o_cmp, o_sel, o_win, y, y, y, wd, wn, g, h)


def _mlp_ple_kernel(h_ref, g1_ref, wu_ref, wd_ref, g2_ref, g3_ref, wg_ref, p_ref, wp_ref, out_ref,
                    hn_ref, acc_ref):
    j = pl.program_id(1)

    @pl.when(j == 0)
    def _():
        hn_ref[...] = _rms(h_ref[...], g1_ref[...]).astype(BF16)
        acc_ref[...] = jnp.zeros(acc_ref.shape, F32)

    u = jnp.maximum(_dot(hn_ref[...], wu_ref[...]), 0.0)
    acc_ref[...] += _dot((u * u).astype(BF16), wd_ref[...])

    @pl.when(j == pl.num_programs(1) - 1)
    def _():
        h2 = h_ref[...] + _rms(acc_ref[...], g2_ref[...])
        gate = jax.nn.sigmoid(_dot(_rms(h2, g3_ref[...]).astype(BF16), wg_ref[...]))
        out_ref[...] = h2 + gate * _dot(p_ref[...].astype(BF16), wp_ref[...])


def mlp_ple(h, g1, wu, wd, g2, g3, wg, p, wp, tm, tf):
    S, D = h.shape
    FF = wu.shape[1]
    PD = p.shape[1]
    const = lambda a: pl.BlockSpec(a.shape, lambda i, j: (0,) * a.ndim)
    return pl.pallas_call(
        _mlp_ple_kernel,
        grid=(S // tm, FF // tf),
        in_specs=[pl.BlockSpec((tm, D), lambda i, j: (i, 0)), const(g1),
                  pl.BlockSpec((D, tf), lambda i, j: (0, j)),
                  pl.BlockSpec((tf, D), lambda i, j: (j, 0)),
                  const(g2), const(g3), const(wg),
                  pl.BlockSpec((tm, PD), lambda i, j: (i, 0)), const(wp)],
        out_specs=pl.BlockSpec((tm, D), lambda i, j: (i, 0)),
        out_shape=jax.ShapeDtypeStruct((S, D), F32),
        scratch_shapes=[pltpu.VMEM((tm, D), BF16), pltpu.VMEM((tm, D), F32)],
        compiler_params=_cparams(("parallel", "arbitrary")),
    )(h, g1, wu, wd, g2, g3, wg, p, wp)


def _t5_bucket_of(dist):
    n = jnp.maximum(dist, 0)
    max_exact = T5_BUCKETS // 2
    ratio = jnp.log(jnp.maximum(n, 1).astype(F32) / max_exact) / math.log(T5_MAX_DIST / max_exact)
    large = jnp.minimum(max_exact + (ratio * (T5_BUCKETS - max_exact)).astype(jnp.int32), T5_BUCKETS - 1)
    return jnp.where(n < max_exact, n, large)


def _bias_of_dist(bias, dist):
    bucket = _t5_bucket_of(dist)[None]
    out = jnp.zeros((bias.shape[1],) + dist.shape, F32)
    for b in range(T5_BUCKETS):
        out = jnp.where(bucket == b, bias[b].reshape((-1,) + (1,) * dist.ndim), out)
    return out


def _causal_tables(T, R):
    k = jnp.arange(R * T)[None, :, None]
    q = jnp.arange(T)[None, None, :]
    e = jnp.arange(R + 1)[:, None, None]
    return jnp.where(e * T + q >= k, 0.0, NEG_INF).astype(F32)[:, None]


def _t5_delta_tables(bias, T):
    n_delta = -(-(T5_MAX_DIST - 1) // T) + 2
    k = jnp.arange(T)[None, :, None]
    q = jnp.arange(T)[None, None, :]
    dist = jnp.arange(n_delta)[:, None, None] * T + q - k
    val = (_bias_of_dist(bias, dist) - bias[T5_BUCKETS - 1].reshape(-1, 1, 1, 1)) * LOG2E
    return jnp.transpose(jnp.where(dist[None] >= 0, val, NEG_INF), (1, 0, 2, 3))


def _band_table(bias, Ta, n_prev, max_rel, stride, inclusive):
    P = n_prev * Ta
    i = jnp.arange(Ta)[:, None]
    c = jnp.arange(P + Ta)[None, :]
    rel = i + P - c
    ok = (rel >= 0) & ((rel <= max_rel) if inclusive else (rel < max_rel))
    return jnp.where(ok[None], _bias_of_dist(bias, rel * stride), NEG_INF)


def _tile(S, pref):
    t = min(pref, S)
    assert S % t == 0
    return t


def even_mixer_core(h, g_pre, w_in, cos, sin, w_uq, q_norm, w_ukv, kv_norm, forget_bias, T):
    S = h.shape[0]
    scale = MLA_QK_DIM ** -0.5 * LOG2E
    half = MLA_ROPE_DIM // 2
    cos_t, sin_t = jnp.transpose(cos), jnp.transpose(sin)
    ones = jnp.ones((MLA_NOPE_DIM, S), F32)
    cq = jnp.concatenate([ones, cos_t, cos_t], axis=0) * scale
    sq = jnp.concatenate([0.0 * ones, -sin_t, sin_t], axis=0) * scale
    ck = jnp.concatenate([cos, cos], axis=1)
    sk = jnp.concatenate([-sin, sin], axis=1)
    wq = jnp.transpose(w_uq.reshape(MLA_Q_RANK, MLA_HEADS, MLA_QK_DIM), (1, 2, 0))
    swap = np.concatenate([np.arange(MLA_NOPE_DIM), MLA_NOPE_DIM + half + np.arange(half),
                           MLA_NOPE_DIM + np.arange(half)])
    wqs = wq[:, swap, :]
    wkv = jnp.transpose(w_ukv.reshape(MLA_KV_RANK, MLA_HEADS, MLA_NOPE_DIM + MLA_V_DIM), (1, 0, 2))
    wk = jnp.pad(wkv[:, :, :MLA_NOPE_DIM], ((0, 0), (0, 0), (0, MLA_ROPE_DIM)))
    wv = jnp.transpose(wkv[:, :, MLA_NOPE_DIM:], (0, 2, 1))
    e96 = jnp.pad(jnp.eye(MLA_ROPE_DIM, dtype=F32), ((0, 0), (MLA_NOPE_DIM, 0)))
    qm, km, vm, fq, fk, fv, logf, st = even_prep(
        h, g_pre, w_in, wq.astype(BF16), wqs.astype(BF16), wk.astype(BF16), e96.astype(BF16), wv.astype(BF16),
        jnp.eye(MLA_QK_DIM, dtype=BF16), q_norm[None, :], kv_norm[None, :], forget_bias[None, :],
        cq, sq, ck, sk, _tile(S, 512))
    neg_f = cumsum_split(jnp.transpose(logf))
    hps = MLA_HEADS
    causal = _causal_tables(T, 2 if S >= 2 * T else 1)
    r_mla, fast_mla = _reference_rows(st[0], st[1], st[2], 0.0, 0.0, 0.0, T, hps)
    o_mla = flash_causal(qm[None], r_mla, km, vm, causal, fast_mla, T, hps)
    r_fox, fast_fox = _reference_rows(st[3], st[4], st[5], jnp.sum(neg_f.astype(F32), axis=0), 0.0, 0.0, T, hps)
    fk_aug = jnp.concatenate([fk, jnp.transpose(neg_f, (1, 2, 0)),
                              jnp.zeros((FOX_HEADS, S, FOX_PAD - FOX_SPLIT), BF16), _stat_cols(FOX_HEADS, S)], axis=2)
    o_fox = flash_causal(fq[None], r_fox, fk_aug, fv, causal, fast_fox, T, hps)
    return o_mla, o_fox


def _odd_deint_plan():
    G = len(DIL_GROUPS)
    return tuple(((b * G + g) * DIL_GW, DIL_GW, d) for g, (_, d) in enumerate(DIL_GROUPS) if d > 1 for b in range(3))


def odd_mixer_core(y, deint, t5_bias, pos_k, pos_v, w1k, w2k, w1v, w2v, T):
    S = y.shape[0]
    G = len(DIL_GROUPS)
    Ta = 256
    o_dil, lse_dil = [], []
    for g, (w, d) in enumerate(DIL_GROUPS):
        ta = _tile(S // d, Ta)
        sb = min(BAND_SUB, ta)
        assert w // d <= sb <= ta
        btab = _band_table(t5_bias[:, g * DIL_HEADS:(g + 1) * DIL_HEADS], sb, 1, w // d, d, True)
        if d == 1:
            o, lse = band_attention(y[None], y[None], y[None], btab, ta, 1, True,
                                    widths=(DIL_GW, DIL_GW), col_blocks=(g, G + g, 2 * G + g))
            o, lse = o[0], lse[0]
        else:
            q, k, v = (deint[(b * G + g) * DIL_GW] for b in range(3))
            o, lse = band_attention(q, k, v, btab, ta, 1, True)
        o_dil.append(o)
        lse_dil.append(lse)
    nq_block = 3 * G
    gate_block = nq_block + 1
    base = (gate_block + 3) * NSA_W
    k_cmp, v_cmp, k_sel, v_sel, k_win, v_win = (y[:, base + i * HEAD_DIM:base + (i + 1) * HEAD_DIM] for i in range(6))
    bias_nsa = t5_bias[:, G * DIL_HEADS:]
    n_chunk = S // NSA_CMP_STRIDE
    cw = NSA_CMP_STRIDE * HEAD_DIM
    kc, vc = nsa_compress(k_cmp.reshape(n_chunk, cw), v_cmp.reshape(n_chunk, cw),
                          pos_k.reshape(2, cw), pos_v.reshape(2, cw),
                          w1k.reshape(2, cw, -1).astype(BF16), w2k.astype(BF16),
                          w1v.reshape(2, cw, -1).astype(BF16), jnp.transpose(w2v).astype(BF16))
    n_sel = S // NSA_SEL_BLOCK
    ci = np.arange(n_chunk)[None, :] * NSA_CMP_STRIDE
    sj = np.arange(n_sel)[:, None] * NSA_SEL_BLOCK
    overlap_t = jnp.asarray(((ci < sj + NSA_SEL_BLOCK) & (ci + NSA_CMP_LEN > sj)).astype(np.float32), BF16)
    nbg = min(n_sel, NSA_SEL_GROUP_BLOCKS)
    o_cmp, qaug, st, kaug, v_sel_t = nsa_cmp_topk(y, nq_block, k_sel.astype(BF16), v_sel.astype(BF16), kc, vc,
                                                  overlap_t, _tile(S, 256), nbg)
    delta = (bias_nsa - bias_nsa[T5_BUCKETS - 1]) * LOG2E
    r_sel, fast_sel = _reference_rows(st[0], st[1, :1], st[2], 0.0, jnp.max(delta, axis=0)[:, None],
                                      delta[0][:, None], T, NSA_HEADS)
    o_sel = flash_causal(qaug, r_sel, kaug[None], v_sel_t[None], _t5_delta_tables(bias_nsa, T), fast_sel, T,
                         hps=NSA_HEADS, key_group_tokens=nbg * NSA_SEL_BLOCK)
    Tw = _tile(S, NSA_WINDOW)
    n_prev = NSA_WINDOW // Tw
    wtab = _band_table(bias_nsa, BAND_SUB, NSA_WINDOW // BAND_SUB, NSA_WINDOW, 1, False)
    o_win = band_attention(y[None], k_win[None], v_win[None], wtab, Tw, n_prev, False,
                           widths=(NSA_W, HEAD_DIM), col_blocks=(nq_block, 0, 0))[0]
    return o_dil, lse_dil, o_cmp, o_sel, o_win, gate_block


def _pad_cols(w, n):
    return jnp.pad(w, ((0, 0), (0, n - w.shape[1])))


def _even_w_in(w):
    cq, ckv, kr, fq, fk, fv, fl = jnp.split(w, np.cumsum([256, 128, 32, 512, 512, 512])[:], axis=1)
    half = MLA_ROPE_DIM // 2
    kr_sw = jnp.concatenate([kr[:, half:], kr[:, :half]], axis=1)
    head = _pad_cols(jnp.concatenate([cq, ckv, kr, kr_sw, fl], axis=1), 512)
    return jnp.concatenate([head, fq, fk, fv], axis=1).astype(BF16)


def _odd_w_in(w):
    n_main = 3 * 3 * DIL_GW + NSA_W
    main, six = w[:, :n_main], w[:, n_main:n_main + 6 * HEAD_DIM]
    gl = w[:, n_main + 6 * HEAD_DIM:]
    gl = jnp.transpose(gl.reshape(-1, NSA_HEADS, 3), (0, 2, 1))
    gl = jnp.repeat(gl[..., None], HEAD_DIM, axis=-1).reshape(w.shape[0], 3 * NSA_W)
    return _pad_cols(jnp.concatenate([main, gl, six], axis=1), ODD_IN_PAD).astype(BF16)


def _trunk(x, p, positions, t5_bias, ev_w_in, ev_q_norm, ev_w_uq, ev_kv_norm, ev_w_ukv, ev_forget_bias,
           ev_w_out, od_w_in, od_cmp_pos_k, od_cmp_pos_v, od_cmp_w1_k, od_cmp_w2_k, od_cmp_w1_v, od_cmp_w2_v,
           od_w_out, norm_mix_pre, norm_mix_post, norm_mlp_pre, norm_mlp_post, w_mlp_up, w_mlp_down,
           ple_norm, w_ple_gate, w_ple_proj):
    S, D = x.shape
    depth = p.shape[0]
    T = _tile(S, 512)
    tm = _tile(S, 512)
    inv_freq = ROPE_THETA ** (-jnp.arange(0, MLA_ROPE_DIM, 2, dtype=F32) / MLA_ROPE_DIM)
    angles = positions.astype(F32)[:, None] * inv_freq
    cos, sin = jnp.cos(angles), jnp.sin(angles)
    h = x
    for i in range(depth):
        j = i // 2
        if i % 2 == 0:
            o_mla, o_fox = even_mixer_core(h, norm_mix_pre[i][None], _even_w_in(ev_w_in[j]), cos, sin, ev_w_uq[j],
                                           ev_q_norm[j], ev_w_ukv[j], ev_kv_norm[j], ev_forget_bias[j], T)
            wo = ev_w_out[j].astype(BF16)
            na = MLA_HEADS * MLA_V_DIM
            h = outproj_even(o_mla, o_fox, wo[:na], wo[na:], norm_mix_post[i][None], h, tm)
        else:
            plan = _odd_deint_plan()
            y, *parts = odd_inproj(h, norm_mix_pre[i][None], _odd_w_in(od_w_in[j]), _tile(S, 256), plan)
            o_dil, lse_dil, o_cmp, o_sel, o_win, gate_block = odd_mixer_core(
                y, {col: a for (col, _, _), a in zip(plan, parts)}, t5_bias, od_cmp_pos_k[j], od_cmp_pos_v[j], od_cmp_w1_k[j], od_cmp_w2_k[j],
                od_cmp_w1_v[j], od_cmp_w2_v[j], T)
            wo = od_w_out[j].astype(BF16)
            h = outproj_odd(o_dil, lse_dil, o_cmp, o_sel, o_win, y, gate_block, wo[:DIL_GW], wo[DIL_GW:],
                            norm_mix_post[i][None], h, tm)
        h = mlp_ple(h, norm_mlp_pre[i][None], w_mlp_up[i].astype(BF16), w_mlp_down[i].astype(BF16),
                    norm_mlp_post[i][None], ple_norm[i][None], w_ple_gate[i].astype(BF16), p[i],
                    w_ple_proj[i].astype(BF16), _tile(S, 1024), 512)
    return h


def kernel(x, p, positions, t5_bias, ev_w_in, ev_q_norm, ev_w_uq, ev_kv_norm, ev_w_ukv, ev_forget_bias, ev_w_out, od_w_in, od_cmp_pos_k, od_cmp_pos_v, od_cmp_w1_k, od_cmp_w2_k, od_cmp_w1_v, od_cmp_w2_v, od_w_out, norm_mix_pre, norm_mix_post, norm_mlp_pre, norm_mlp_post, w_mlp_up, w_mlp_down, ple_norm, w_ple_gate, w_ple_proj):
    params = (t5_bias, ev_w_in, ev_q_norm, ev_w_uq, ev_kv_norm, ev_w_ukv, ev_forget_bias, ev_w_out, od_w_in,
              od_cmp_pos_k, od_cmp_pos_v, od_cmp_w1_k, od_cmp_w2_k, od_cmp_w1_v, od_cmp_w2_v, od_w_out,
              norm_mix_pre, norm_mix_post, norm_mlp_pre, norm_mlp_post, w_mlp_up, w_mlp_down, ple_norm,
              w_ple_gate, w_ple_proj)
    outs = [_trunk(x[b], p[:, b], positions[b], *params) for b in range(x.shape[0])]
    return jnp.stack(outs).astype(x.dtype)
```

```python
import functools
import math

import numpy as np
import jax
import jax.numpy as jnp
from jax import lax
from jax.experimental import pallas as pl
from jax.experimental.pallas import tpu as pltpu

F32 = jnp.float32
BF16 = jnp.bfloat16

HEAD_DIM = 64
RMS_EPS = 1e-6
NEG_INF = -1e30
SEL_NEG = -(2.0 ** 99)

MLA_HEADS = 8
MLA_NOPE_DIM = 64
MLA_ROPE_DIM = 32
MLA_V_DIM = 64
MLA_Q_RANK = 256
MLA_KV_RANK = 128
MLA_QK_DIM = MLA_NOPE_DIM + MLA_ROPE_DIM
ROPE_THETA = 10000.0
FOX_HEADS = 8
FOX_SPLIT = 3
FOX_PAD = 16
LOG2E = math.log2(math.e)
STAT_ROWS = 16
FAST_OVER = 64.0
FAST_MARGIN = 150.0
P_CHUNK = 32
BAND_SUB = 128
DIL_GROUPS = ((128, 1), (512, 4), (2048, 16))
DIL_HEADS = 4
DIL_GW = DIL_HEADS * HEAD_DIM
NSA_HEADS = 4
NSA_W = NSA_HEADS * HEAD_DIM
NSA_CMP_LEN = 32
NSA_CMP_STRIDE = 16
NSA_SEL_BLOCK = 64
NSA_TOPK = 16
NSA_WINDOW = 512
NSA_FORCE_SCORE = 1e9
NSA_SEL_GROUP_BLOCKS = 128
T5_BUCKETS = 32
T5_MAX_DIST = 2048
T5_FAR_DIST = 1536
assert 16 * math.log(T5_FAR_DIST / 16) / math.log(T5_MAX_DIST / 16) > 15.02

EVEN_IN_PAD = 2048
ODD_IN_PAD = 3840

VMEM_LIMIT_BYTES = 56 * 1024 * 1024


def _cparams(sem):
    return pltpu.CompilerParams(dimension_semantics=sem, vmem_limit_bytes=VMEM_LIMIT_BYTES)


def _rms(x, g):
    return x * lax.rsqrt(jnp.mean(x * x, axis=-1, keepdims=True) + RMS_EPS) * g


def _dot(a, b):
    return jnp.dot(a, b, preferred_element_type=F32)


def _dot_nt(a, b):
    return lax.dot_general(a, b, (((1,), (1,)), ((), ())), preferred_element_type=F32)


LANES = 128


def _odd_inproj_kernel(h_ref, g_ref, w_ref, y_ref, *rest, deint):
    out_refs, scr = rest[:-1], rest[-1]
    y = _dot(_rms(h_ref[...], g_ref[...]).astype(BF16), w_ref[...])
    y_ref[...] = y
    tm = y.shape[0]
    for out, (col, width, d) in zip(out_refs, deint):
        n = width // LANES
        for c in range(n):
            scr[c] = y[:, col + c * LANES:col + (c + 1) * LANES]
        for r in range(d):
            out[r] = jnp.concatenate([scr[c, pl.ds(r, tm // d, stride=d), :] for c in range(n)], axis=1)


def odd_inproj(h, g, w, tm, deint):
    S, D = h.shape
    N = w.shape[1]
    out_shape = [jax.ShapeDtypeStruct((S, N), F32)]
    out_specs = [pl.BlockSpec((tm, N), lambda i: (i, 0))]
    for _, width, d in deint:
        out_shape.append(jax.ShapeDtypeStruct((d, S // d, width), F32))
        out_specs.append(pl.BlockSpec((d, tm // d, width), lambda i: (0, i, 0)))
    return pl.pallas_call(
        functools.partial(_odd_inproj_kernel, deint=deint),
        grid=(S // tm,),
        in_specs=[pl.BlockSpec((tm, D), lambda i: (i, 0)),
                  pl.BlockSpec((1, D), lambda i: (0, 0)),
                  pl.BlockSpec((D, N), lambda i: (0, 0))],
        out_specs=out_specs,
        out_shape=out_shape,
        scratch_shapes=[pltpu.VMEM((max(wd for _, wd, _ in deint) // LANES, tm, LANES), F32)],
        compiler_params=_cparams(("parallel",)),
    )(h, g, w)


def _even_prep_kernel(h_ref, g_ref, win_ref, wq_ref, wqs_ref, wk_ref, e_ref, wv_ref, eye_ref, qn_ref, kvn_ref,
                      fb_ref, cq_ref, sq_ref, ck_ref, sk_ref,
                      qm_ref, km_ref, vm_ref, fq_ref, fk_ref, fv_ref, lf_ref, st_ref):
    y = _dot(_rms(h_ref[...], g_ref[...]).astype(BF16), win_ref[...])
    tm = y.shape[0]
    c_q = y[:, 0:MLA_Q_RANK]
    c_kv = y[:, MLA_Q_RANK:MLA_Q_RANK + MLA_KV_RANK]
    o = MLA_Q_RANK + MLA_KV_RANK
    k_r = y[:, o:o + MLA_ROPE_DIM]
    k_rs = y[:, o + MLA_ROPE_DIM:o + 2 * MLA_ROPE_DIM]
    f_logit = y[:, o + 2 * MLA_ROPE_DIM:o + 2 * MLA_ROPE_DIM + FOX_HEADS]
    nq = _rms(c_q, qn_ref[...]).astype(BF16)
    nkv = _rms(c_kv, kvn_ref[...]).astype(BF16)
    k_rot = (k_r * ck_ref[...] + k_rs * sk_ref[...]).astype(BF16)
    k_rot96 = _dot(k_rot, e_ref[...])
    cq = cq_ref[...]
    sq = sq_ref[...]
    eye = eye_ref[...]
    row = lax.broadcasted_iota(jnp.int32, (STAT_ROWS, tm), 0)
    ones_row = jnp.where(row < 1, 1.0, 0.0).astype(BF16)
    col = lax.broadcasted_iota(jnp.int32, (tm, STAT_ROWS), 1)
    stat_cols = jnp.where(col < 3, 1.0, 0.0).astype(BF16)
    stats = [[] for _ in range(6)]

    def record(slot, q_t, k):
        q_t = q_t.astype(F32)
        k_t = _dot_nt(eye[:k.shape[1], :k.shape[1]], k)
        stats[slot].append(jnp.sum(q_t * q_t, axis=0, keepdims=True))
        stats[slot + 1].append(jnp.sum(k_t * k_t, axis=0, keepdims=True))
        stats[slot + 2].append(jnp.sum(q_t * k_t, axis=0, keepdims=True))

    for h in range(MLA_HEADS):
        q = (_dot_nt(wq_ref[h], nq) * cq + _dot_nt(wqs_ref[h], nq) * sq).astype(BF16)
        k = (_dot(nkv, wk_ref[h]) + k_rot96).astype(BF16)
        qm_ref[h] = q
        km_ref[h] = jnp.concatenate([k, stat_cols], axis=1)
        vm_ref[h] = jnp.concatenate([_dot_nt(wv_ref[h], nkv).astype(BF16), ones_row], axis=0)
        record(0, q, k)
    base = 512
    fw = FOX_HEADS * HEAD_DIM
    eye64 = eye[:HEAD_DIM, :HEAD_DIM]
    ones_rows = jnp.where(row < FOX_SPLIT, 1.0, 0.0).astype(BF16)
    for h in range(FOX_HEADS):
        lo = base + h * HEAD_DIM
        fq = _dot_nt(eye64, (y[:, lo:lo + HEAD_DIM] * (HEAD_DIM ** -0.5 * LOG2E)).astype(BF16)).astype(BF16)
        fk = y[:, lo + fw:lo + fw + HEAD_DIM].astype(BF16)
        fq_ref[h] = jnp.concatenate([fq, ones_rows], axis=0)
        fk_ref[h] = fk
        fv = y[:, lo + 2 * fw:lo + 2 * fw + HEAD_DIM].astype(BF16)
        fv_ref[h] = jnp.concatenate([_dot_nt(eye64, fv).astype(BF16), ones_row], axis=0)
        record(3, fq, fk)
    for i in range(6):
        st_ref[i] = jnp.concatenate(stats[i], axis=0)
    z = f_logit + fb_ref[...]
    lf_ref[...] = jnp.minimum(z, 0.0) - jnp.log1p(jnp.exp(-jnp.abs(z)))


def even_prep(h, g, w_in, wq, wqs, wk, e96, wv, eye, qn, kvn, fb, cq, sq, ck, sk, tm):
    S, D = h.shape
    full = lambda a: pl.BlockSpec(a.shape, lambda i: (0,) * a.ndim)
    rows = lambda w: pl.BlockSpec((tm, w), lambda i: (i, 0))
    cols = lambda w: pl.BlockSpec((w, tm), lambda i: (0, i))
    heads = lambda n, w: pl.BlockSpec((n, tm, w), lambda i: (0, i, 0))
    heads_t = lambda n, w: pl.BlockSpec((n, w, tm), lambda i: (0, 0, i))
    out_shape = [jax.ShapeDtypeStruct((MLA_HEADS, MLA_QK_DIM, S), BF16),
                 jax.ShapeDtypeStruct((MLA_HEADS, S, MLA_QK_DIM + STAT_ROWS), BF16),
                 jax.ShapeDtypeStruct((MLA_HEADS, MLA_V_DIM + STAT_ROWS, S), BF16),
                 jax.ShapeDtypeStruct((FOX_HEADS, HEAD_DIM + FOX_PAD, S), BF16),
                 jax.ShapeDtypeStruct((FOX_HEADS, S, HEAD_DIM), BF16),
                 jax.ShapeDtypeStruct((FOX_HEADS, HEAD_DIM + STAT_ROWS, S), BF16),
                 jax.ShapeDtypeStruct((S, FOX_HEADS), F32),
                 jax.ShapeDtypeStruct((6, MLA_HEADS, S), F32)]
    out_specs = [heads_t(MLA_HEADS, MLA_QK_DIM), heads(MLA_HEADS, MLA_QK_DIM + STAT_ROWS),
                 heads_t(MLA_HEADS, MLA_V_DIM + STAT_ROWS),
                 heads_t(FOX_HEADS, HEAD_DIM + FOX_PAD), heads(FOX_HEADS, HEAD_DIM),
                 heads_t(FOX_HEADS, HEAD_DIM + STAT_ROWS),
                 rows(FOX_HEADS), heads_t(6, MLA_HEADS)]
    return pl.pallas_call(
        _even_prep_kernel,
        grid=(S // tm,),
        in_specs=[rows(D), full(g), full(w_in), full(wq), full(wqs), full(wk), full(e96), full(wv), full(eye),
                  full(qn), full(kvn), full(fb), cols(MLA_QK_DIM), cols(MLA_QK_DIM), rows(MLA_ROPE_DIM),
                  rows(MLA_ROPE_DIM)],
        out_specs=out_specs,
        out_shape=out_shape,
        compiler_params=_cparams(("parallel",)),
    )(h, g, w_in, wq, wqs, wk, e96, wv, eye, qn, kvn, fb, cq, sq, ck, sk)


def _cumsum_split_kernel(x_ref, o_ref):
    x = x_ref[...]
    n = x.shape[1]
    lane = lax.broadcasted_iota(jnp.int32, x.shape, 1)
    shift = 1
    while shift < n:
        x = x + jnp.where(lane >= shift, pltpu.roll(x, shift, axis=1), 0.0)
        shift *= 2
    r = -x * LOG2E
    for i in range(FOX_SPLIT):
        part = r.astype(BF16)
        o_ref[i] = part
        r = r - part.astype(F32)


def cumsum_split(x):
    return pl.pallas_call(
        _cumsum_split_kernel,
        out_shape=jax.ShapeDtypeStruct((FOX_SPLIT,) + x.shape, BF16),
        compiler_params=_cparams(None),
    )(x)


def _flash_kernel(qt_ref, kt_ref, fast_ref, q_ref, r_ref, k_ref, v_ref, bt_ref, o_ref, m_scr, acc_scr, p_scr, *,
                  hps, shared_kv, n_delta, nt, R):
    step = pl.program_id(1)
    qi = qt_ref[step]
    ki = kt_ref[step]
    fast = fast_ref[pl.program_id(0) * nt + qi] != 0
    near = qi - R * ki < n_delta - 1

    @pl.when(ki == 0)
    def _():
        m_scr[...] = jnp.full(m_scr.shape, -3e38, F32)
        acc_scr[...] = jnp.zeros(acc_scr.shape, F32)

    hb = bt_ref.shape[1]
    Tk = k_ref.shape[1]
    Dv = v_ref.shape[1] - STAT_ROWS

    def logits(h, with_table):
        q = jnp.concatenate([q_ref[0, h], r_ref[h]], axis=0)
        s = _dot(k_ref[0 if shared_kv else h], q)
        return s + bt_ref[0, h if hb > 1 else 0] if with_table else s

    def store_p(h, s, m):
        for c in range(Tk // P_CHUNK):
            rows = slice(c * P_CHUNK, (c + 1) * P_CHUNK)
            x = s[rows] if m is None else s[rows] - m
            p_scr[h, rows, :] = jnp.exp2(x).astype(BF16)

    def fast_update(with_table):
        acc_prev = [acc_scr[h] for h in range(hps)]
        for h in range(hps):
            store_p(h, logits(h, with_table), None)
        for h in range(hps):
            acc_scr[h] = acc_prev[h] + _dot(v_ref[0 if shared_kv else h], p_scr[h])

    def safe_update(with_table):
        m_prev = [m_scr[h] for h in range(hps)]
        acc_prev = [acc_scr[h] for h in range(hps)]
        ss = [logits(h, with_table) for h in range(hps)]
        ms = [jnp.maximum(m_prev[h], jnp.max(ss[h], axis=0, keepdims=True)) for h in range(hps)]
        for h in range(hps):
            store_p(h, ss[h], ms[h])
        for h in range(hps):
            alpha = jnp.exp2(m_prev[h] - ms[h])
            acc_scr[h] = alpha * acc_prev[h] + _dot(v_ref[0 if shared_kv else h], p_scr[h])
            m_scr[h] = ms[h]

    for take_fast, update in ((True, fast_update), (False, safe_update)):
        for with_table in (True, False):
            cond = jnp.logical_and(fast == take_fast, near == with_table)
            pl.when(cond)(functools.partial(update, with_table))

    @pl.when(ki == qi // R)
    def _():
        outs = [(acc_scr[h, :Dv, :] / acc_scr[h, Dv:Dv + 1, :]).T for h in range(hps)]
        o_ref[...] = jnp.concatenate(outs, axis=1).astype(o_ref.dtype)


def flash_causal(qT, r_rows, k, vT, btab, fast, T, hps, key_group_tokens=None, out_dtype=BF16):
    G, H, Dq, S = qT.shape
    Dk = Dq + STAT_ROWS
    Hk, Dv, _ = vT.shape
    Dv -= STAT_ROWS
    shared_kv = Hk == 1
    n_delta, Hb, Tk = btab.shape[0], btab.shape[1], btab.shape[2]
    R = Tk // T
    nt = S // T
    qt = np.array([i for i in range(nt) for _ in range(i // R + 1)], np.int32)
    kt = np.array([j for i in range(nt) for j in range(i // R + 1)], np.int32)
    kv_h = 1 if shared_kv else hps
    assert G == 1 or R == 1
    tiles_per_group = (key_group_tokens // T) if G > 1 else 1

    def q_map(g, s, qt, kt, fast):
        return (kt[s] // tiles_per_group if G > 1 else 0, g, 0, qt[s])

    def bt_map(g, s, qt, kt, fast):
        return (jnp.minimum(qt[s] - R * kt[s], n_delta - 1), g if Hb > hps else 0, 0, 0)

    in_specs = [pl.BlockSpec((1, hps, Dq, T), q_map),
                pl.BlockSpec((hps, STAT_ROWS, T), lambda g, s, qt, kt, fast: (g, 0, qt[s])),
                pl.BlockSpec((kv_h, Tk, Dk), lambda g, s, qt, kt, fast: (0 if shared_kv else g, kt[s], 0)),
                pl.BlockSpec((kv_h, Dv + STAT_ROWS, Tk), lambda g, s, qt, kt, fast: (0 if shared_kv else g, 0, kt[s])),
                pl.BlockSpec((1, min(Hb, hps), Tk, T), bt_map)]
    grid_spec = pltpu.PrefetchScalarGridSpec(
        num_scalar_prefetch=3,
        grid=(H // hps, len(qt)),
        in_specs=in_specs,
        out_specs=pl.BlockSpec((T, hps * Dv), lambda g, s, qt, kt, fast: (qt[s], g)),
        scratch_shapes=[pltpu.VMEM((hps, 1, T), F32), pltpu.VMEM((hps, Dv + STAT_ROWS, T), F32),
                        pltpu.VMEM((hps, Tk, T), BF16)],
    )
    return pl.pallas_call(
        functools.partial(_flash_kernel, hps=hps, shared_kv=shared_kv, n_delta=n_delta, nt=nt, R=R),
        grid_spec=grid_spec,
        out_shape=jax.ShapeDtypeStruct((S, H * Dv), out_dtype),
        compiler_params=_cparams(("parallel", "arbitrary")),
    )(jnp.asarray(qt), jnp.asarray(kt), fast, qT, r_rows, k, vT, btab)


def _reference_rows(qn2, kn2, l_self, extra, bias_max, bias_self, T, hps):
    H, S = qn2.shape
    kmax = jnp.sqrt(jnp.max(kn2, axis=1, keepdims=True))
    bound = jnp.sqrt(qn2) * kmax + bias_max + extra
    ok = bound - (l_self + bias_self + extra) <= FAST_MARGIN
    fast = jnp.all(ok.reshape(H // hps, hps, S // T, T), axis=(1, 3)).astype(jnp.int32).reshape(-1)
    r = FAST_OVER - bound
    parts = []
    for _ in range(3):
        part = r.astype(BF16)
        parts.append(part)
        r = r - part.astype(F32)
    rows = jnp.stack(parts + [jnp.zeros_like(parts[0])] * (STAT_ROWS - 3), axis=1)
    return rows, fast


def _stat_cols(Hk, S):
    return jnp.broadcast_to((jnp.arange(STAT_ROWS) < 3).astype(BF16), (Hk, S, STAT_ROWS))


def _band_kernel(*refs, n_prev, Ta, H, shared_kv, want_lse):
    q_ref = refs[0]
    k_refs = refs[1:2 + n_prev]
    v_refs = refs[2 + n_prev:3 + 2 * n_prev]
    bt_ref = refs[3 + 2 * n_prev]
    o_ref = refs[4 + 2 * n_prev]
    lse_ref = refs[5 + 2 * n_prev] if want_lse else None
    ai = pl.program_id(1)
    k = jnp.concatenate([r[...] for r in k_refs], axis=0).astype(BF16)
    v = jnp.concatenate([r[...] for r in v_refs], axis=0).astype(BF16)
    sb = bt_ref.shape[1]
    P = bt_ref.shape[2] - sb
    q = q_ref[...]
    for s_i in range(Ta // sb):
        first = n_prev * Ta + s_i * sb - P
        col = lax.broadcasted_iota(jnp.int32, (1, P + sb), 1)
        col_valid = (ai - n_prev) * Ta + first + col >= 0
        outs, lses = [], []
        for h in range(H):
            hk = 0 if shared_kv else h
            qh = (q[s_i * sb:(s_i + 1) * sb, h * HEAD_DIM:(h + 1) * HEAD_DIM] * (HEAD_DIM ** -0.5)).astype(BF16)
            s = _dot_nt(qh, k[first:first + P + sb, hk * HEAD_DIM:(hk + 1) * HEAD_DIM]) + bt_ref[h]
            s = jnp.where(col_valid, s, NEG_INF)
            m = jnp.max(s, axis=-1, keepdims=True)
            p = jnp.exp(s - m)
            l = jnp.sum(p, axis=-1, keepdims=True)
            outs.append(_dot(p.astype(BF16), v[first:first + P + sb, hk * HEAD_DIM:(hk + 1) * HEAD_DIM]) / l)
            if want_lse:
                lses.append(jnp.broadcast_to(m + jnp.log(l), (sb, HEAD_DIM)))
        o_ref[s_i * sb:(s_i + 1) * sb, :] = jnp.concatenate(outs, axis=1)
        if want_lse:
            lse_ref[s_i * sb:(s_i + 1) * sb, :] = jnp.concatenate(lses, axis=1)


def band_attention(q, k, v, btab, Ta, n_prev, want_lse, widths=None, col_blocks=(0, 0, 0)):
    R, A = q.shape[:2]
    QW, KW = widths if widths else (q.shape[2], k.shape[2])
    H = QW // HEAD_DIM
    shared_kv = KW == HEAD_DIM
    nA = A // Ta
    qc, kc, vc = col_blocks

    def prev_map(p, c):
        return lambda r, a: (r, jnp.maximum(a - n_prev + p, 0), c)

    cur = lambda r, a: (r, a, 0)
    k_specs = ([pl.BlockSpec((None, Ta, KW), prev_map(p, kc)) for p in range(n_prev)]
               + [pl.BlockSpec((None, Ta, KW), lambda r, a: (r, a, kc))])
    v_specs = ([pl.BlockSpec((None, Ta, KW), prev_map(p, vc)) for p in range(n_prev)]
               + [pl.BlockSpec((None, Ta, KW), lambda r, a: (r, a, vc))])
    out_shape = [jax.ShapeDtypeStruct((R, A, QW), F32)]
    out_specs = [pl.BlockSpec((None, Ta, QW), cur)]
    if want_lse:
        out_shape.append(jax.ShapeDtypeStruct((R, A, QW), F32))
        out_specs.append(pl.BlockSpec((None, Ta, QW), cur))
    res = pl.pallas_call(
        functools.partial(_band_kernel, n_prev=n_prev, Ta=Ta, H=H, shared_kv=shared_kv, want_lse=want_lse),
        grid=(R, nA),
        in_specs=[pl.BlockSpec((None, Ta, QW), lambda r, a: (r, a, qc))] + k_specs + v_specs
                 + [pl.BlockSpec(btab.shape, lambda r, a: (0, 0, 0))],
        out_specs=out_specs,
        out_shape=out_shape,
        compiler_params=_cparams(("parallel", "arbitrary")),
    )(q, *([k] * (n_prev + 1)), *([v] * (n_prev + 1)), btab)
    return res if want_lse else res[0]


def _gelu_tanh(x):
    return 0.5 * x * (1.0 + jnp.tanh(math.sqrt(2.0 / math.pi) * (x + 0.044715 * (x * x * x))))


def _compress_kernel(ks_ref, vs_ref, pk_ref, pv_ref, w1k_ref, w2k_ref, w1v_ref, w2vt_ref, kc_ref, vct_ref):
    def hidden(src, pos, w1):
        x = src[...]
        n = x.shape[0]
        first = _dot((x + pos[0:1, :]).astype(BF16), w1[0])
        second = _dot((x + pos[1:2, :]).astype(BF16), w1[1])
        return _gelu_tanh(first + pltpu.roll(second, n - 1, axis=0)).astype(BF16)

    kc_ref[...] = _dot(hidden(ks_ref, pk_ref, w1k_ref), w2k_ref[...]).astype(BF16)
    vct_ref[...] = _dot_nt(w2vt_ref[...], hidden(vs_ref, pv_ref, w1v_ref)).astype(BF16)


def nsa_compress(k_chunks, v_chunks, pos_k, pos_v, w1k, w2k, w1v, w2v_t):
    n = k_chunks.shape[0]
    return pl.pallas_call(
        _compress_kernel,
        out_shape=[jax.ShapeDtypeStruct((n, HEAD_DIM), BF16), jax.ShapeDtypeStruct((HEAD_DIM, n), BF16)],
        compiler_params=_cparams(None),
    )(k_chunks, v_chunks, pos_k, pos_v, w1k, w2k, w1v, w2v_t)


def _cmp_topk_kernel(nq_ref, ks_ref, vs_ref, kc_ref, vct_ref, ovt_ref, eye_ref,
                     ocmp_ref, qaug_ref, st_ref, kaug_ref, vst_ref, *, Tq, n_cmp, n_sel, nbg):
    q0 = pl.program_id(0) * Tq
    t = q0 + lax.broadcasted_iota(jnp.int32, (n_cmp, Tq), 1)
    cmp_end = NSA_CMP_STRIDE * lax.broadcasted_iota(jnp.int32, (n_cmp, Tq), 0) + (NSA_CMP_LEN - 1)
    c_neg = jnp.where(cmp_end <= t, 0.0, NEG_INF)
    sees_any = q0 + lax.broadcasted_iota(jnp.int32, (1, Tq), 1) >= NSA_CMP_LEN - 1
    kc = kc_ref[...]
    vct = vct_ref[...]
    q = nq_ref[...]
    eye = eye_ref[...]
    ks = ks_ref[...]
    ks_t = _dot_nt(eye, ks)
    tok = q0 + lax.broadcasted_iota(jnp.int32, (Tq, nbg), 0)
    blk = jnp.bitwise_and(jnp.right_shift(tok, NSA_SEL_BLOCK.bit_length() - 1), nbg - 1)
    onehot = jnp.where(blk == lax.broadcasted_iota(jnp.int32, (Tq, nbg), 1), 1.0, 0.0).astype(BF16)
    stat_cols = jnp.where(lax.broadcasted_iota(jnp.int32, (Tq, STAT_ROWS), 1) < 3, 1.0, 0.0).astype(BF16)
    kaug_ref[...] = jnp.concatenate([ks, onehot, stat_cols], axis=1)
    ones_row = jnp.where(lax.broadcasted_iota(jnp.int32, (STAT_ROWS, Tq), 0) < 1, 1.0, 0.0).astype(BF16)
    vst_ref[...] = jnp.concatenate([_dot_nt(eye, vs_ref[...]).astype(BF16), ones_row], axis=0)
    psum = jnp.zeros((n_cmp, Tq), F32)
    outs, q_ts, qn2, l_self = [], [], [], []
    for h in range(NSA_HEADS):
        qh = (q[:, h * HEAD_DIM:(h + 1) * HEAD_DIM] * (HEAD_DIM ** -0.5 * LOG2E)).astype(BF16)
        q_t = _dot_nt(eye, qh)
        q_ts.append(q_t.astype(BF16))
        qn2.append(jnp.sum(q_t * q_t, axis=0, keepdims=True))
        l_self.append(jnp.sum(q_t * ks_t, axis=0, keepdims=True))
        s = _dot(kc, q_ts[h]) + c_neg
        p = jnp.exp2(s - jnp.max(s, axis=0, keepdims=True))
        p = p * jnp.where(sees_any, 1.0 / jnp.sum(p, axis=0, keepdims=True), 0.0)
        psum = psum + p
        outs.append(_dot(vct, p.astype(BF16)).T)
    ocmp_ref[...] = jnp.concatenate(outs, axis=1)
    st_ref[0] = jnp.concatenate(qn2, axis=0)
    st_ref[1] = jnp.broadcast_to(jnp.sum(ks_t * ks_t, axis=0, keepdims=True), (NSA_HEADS, Tq))
    st_ref[2] = jnp.concatenate(l_self, axis=0)
    hi = psum.astype(BF16)
    lo = (psum - hi.astype(F32)).astype(BF16)
    imp = _dot(ovt_ref[...], hi) + _dot(ovt_ref[...], lo)
    t = q0 + lax.broadcasted_iota(jnp.int32, (n_sel, Tq), 1)
    j = lax.broadcasted_iota(jnp.int32, (n_sel, Tq), 0)
    start = j * NSA_SEL_BLOCK
    cur_blk = (start <= t) & (t < start + NSA_SEL_BLOCK)
    prev_blk = (start + NSA_SEL_BLOCK <= t) & (t < start + 2 * NSA_SEL_BLOCK)
    forced = (j == 0) | cur_blk | prev_blk
    score = jnp.where(forced, NSA_FORCE_SCORE, jnp.where(start <= t, imp, -NSA_FORCE_SCORE))
    jf = j.astype(F32)
    taken = jnp.float32(-3e38)
    for _ in range(min(NSA_TOPK, n_sel)):
        mx = jnp.max(score, axis=0, keepdims=True)
        first = jnp.min(jnp.where(score == mx, jf, float(n_sel)), axis=0, keepdims=True)
        score = jnp.where(jf == first, taken, score)
    sel_bias_t = jnp.where(score == taken, 0.0, SEL_NEG).astype(BF16)
    for h in range(NSA_HEADS):
        for g in range(n_sel // nbg):
            qaug_ref[g, h] = jnp.concatenate([q_ts[h], sel_bias_t[g * nbg:(g + 1) * nbg]], axis=0)


def nsa_cmp_topk(y, nq_block, k_sel, v_sel, kc, vc_t, overlap_t, Tq, nbg):
    S = y.shape[0]
    n_cmp = kc.shape[0]
    n_sel = S // NSA_SEL_BLOCK
    G = n_sel // nbg
    assert nbg & (nbg - 1) == 0 and NSA_SEL_BLOCK & (NSA_SEL_BLOCK - 1) == 0
    const = lambda a: pl.BlockSpec(a.shape, lambda i: (0,) * a.ndim)
    rows = lambda w: pl.BlockSpec((Tq, w), lambda i: (i, 0))
    eye = jnp.eye(HEAD_DIM, dtype=BF16)
    kw = HEAD_DIM + nbg + STAT_ROWS
    return pl.pallas_call(
        functools.partial(_cmp_topk_kernel, Tq=Tq, n_cmp=n_cmp, n_sel=n_sel, nbg=nbg),
        grid=(S // Tq,),
        in_specs=[pl.BlockSpec((Tq, NSA_W), lambda i: (i, nq_block)), rows(HEAD_DIM), rows(HEAD_DIM),
                  const(kc), const(vc_t), const(overlap_t), const(eye)],
        out_specs=[rows(NSA_W),
                   pl.BlockSpec((G, NSA_HEADS, HEAD_DIM + nbg, Tq), lambda i: (0, 0, 0, i)),
                   pl.BlockSpec((3, NSA_HEADS, Tq), lambda i: (0, 0, i)),
                   rows(kw),
                   pl.BlockSpec((HEAD_DIM + STAT_ROWS, Tq), lambda i: (0, i))],
        out_shape=[jax.ShapeDtypeStruct((S, NSA_W), F32),
                   jax.ShapeDtypeStruct((G, NSA_HEADS, HEAD_DIM + nbg, S), BF16),
                   jax.ShapeDtypeStruct((3, NSA_HEADS, S), F32),
                   jax.ShapeDtypeStruct((S, kw), BF16),
                   jax.ShapeDtypeStruct((HEAD_DIM + STAT_ROWS, S), BF16)],
        compiler_params=_cparams(("parallel",)),
    )(y, k_sel, v_sel, kc, vc_t, overlap_t, eye)


def _outproj_even_kernel(oa_ref, ob_ref, wa_ref, wb_ref, g_ref, h_ref, out_ref):
    m = _dot(oa_ref[...], wa_ref[...]) + _dot(ob_ref[...], wb_ref[...])
    out_ref[...] = h_ref[...] + _rms(m, g_ref[...])


def outproj_even(oa, ob, wa, wb, g, h, tm):
    S, D = h.shape
    rows = lambda w: pl.BlockSpec((tm, w), lambda i: (i, 0))
    full = lambda a: pl.BlockSpec(a.shape, lambda i: (0,) * a.ndim)
    return pl.pallas_call(
        _outproj_even_kernel,
        grid=(S // tm,),
        in_specs=[rows(oa.shape[1]), rows(ob.shape[1]), full(wa), full(wb), full(g), rows(D)],
        out_specs=rows(D),
        out_shape=jax.ShapeDtypeStruct((S, D), F32),
        compiler_params=_cparams(("parallel",)),
    )(oa, ob, wa, wb, g, h)


def _outproj_odd_kernel(o0_ref, o1_ref, o2_ref, l0_ref, l1_ref, l2_ref, oc_ref, os_ref, ow_ref,
                        gc_ref, gs_ref, gw_ref, wd_ref, wn_ref, g_ref, h_ref, out_ref, scr):
    tm = h_ref.shape[0]

    def rows_of(ref):
        if len(ref.shape) == 2:
            return ref[...]
        d, _, width = ref.shape
        for r in range(d):
            v = ref[r]
            for c in range(width // LANES):
                scr[c, pl.ds(r, tm // d, stride=d), :] = v[:, c * LANES:(c + 1) * LANES]
        return jnp.concatenate([scr[c] for c in range(width // LANES)], axis=1)

    l0, l1, l2 = rows_of(l0_ref), rows_of(l1_ref), rows_of(l2_ref)
    mx = jnp.maximum(jnp.maximum(l0, l1), l2)
    e0, e1, e2 = jnp.exp(l0 - mx), jnp.exp(l1 - mx), jnp.exp(l2 - mx)
    o_dil = (e0 * rows_of(o0_ref) + e1 * rows_of(o1_ref) + e2 * rows_of(o2_ref)) / (e0 + e1 + e2)
    o_nsa = (jax.nn.sigmoid(gc_ref[...]) * oc_ref[...] + jax.nn.sigmoid(gs_ref[...]) * os_ref[...].astype(F32)
             + jax.nn.sigmoid(gw_ref[...]) * ow_ref[...])
    m = _dot(o_dil.astype(BF16), wd_ref[...]) + _dot(o_nsa.astype(BF16), wn_ref[...])
    out_ref[...] = h_ref[...] + _rms(m, g_ref[...])


def outproj_odd(o_dil, lse_dil, o_cmp, o_sel, o_win, y, gate_block, wd, wn, g, h, tm):
    S, D = h.shape
    rows = lambda w: pl.BlockSpec((tm, w), lambda i: (i, 0))
    full = lambda a: pl.BlockSpec(a.shape, lambda i: (0,) * a.ndim)
    gate = lambda b: pl.BlockSpec((tm, NSA_W), lambda i: (i, gate_block + b))

    def dil(a):
        if a.ndim == 2:
            return rows(DIL_GW)
        return pl.BlockSpec((a.shape[0], tm // a.shape[0], DIL_GW), lambda i: (0, i, 0))

    return pl.pallas_call(
        _outproj_odd_kernel,
        grid=(S // tm,),
        in_specs=[dil(a) for a in (*o_dil, *lse_dil)] + [rows(NSA_W)] * 3
                 + [gate(0), gate(1), gate(2), full(wd), full(wn), full(g), rows(D)],
        out_specs=rows(D),
        out_shape=jax.ShapeDtypeStruct((S, D), F32),
        scratch_shapes=[pltpu.VMEM((DIL_GW // LANES, tm, LANES), F32)],
        compiler_params=_cparams(("parallel",)),
    )(*o_dil, *lse_dil, o_cmp, o_sel, o_win, y, y, y, wd, wn, g, h)


def _mlp_ple_kernel(h_ref, g1_ref, wu_ref, wd_ref, g2_ref, g3_ref, wg_ref, p_ref, wp_ref, out_ref,
                    hn_ref, acc_ref):
    j = pl.program_id(1)

    @pl.when(j == 0)
    def _():
        hn_ref[...] = _rms(h_ref[...], g1_ref[...]).astype(BF16)
        acc_ref[...] = jnp.zeros(acc_ref.shape, F32)

    u = jnp.maximum(_dot(hn_ref[...], wu_ref[...]), 0.0)
    acc_ref[...] += _dot((u * u).astype(BF16), wd_ref[...])

    @pl.when(j == pl.num_programs(1) - 1)
    def _():
        h2 = h_ref[...] + _rms(acc_ref[...], g2_ref[...])
        gate = jax.nn.sigmoid(_dot(_rms(h2, g3_ref[...]).astype(BF16), wg_ref[...]))
        out_ref[...] = h2 + gate * _dot(p_ref[...].astype(BF16), wp_ref[...])


def mlp_ple(h, g1, wu, wd, g2, g3, wg, p, wp, tm, tf):
    S, D = h.shape
    FF = wu.shape[1]
    PD = p.shape[1]
    const = lambda a: pl.BlockSpec(a.shape, lambda i, j: (0,) * a.ndim)
    return pl.pallas_call(
        _mlp_ple_kernel,
        grid=(S // tm, FF // tf),
        in_specs=[pl.BlockSpec((tm, D), lambda i, j: (i, 0)), const(g1),
                  pl.BlockSpec((D, tf), lambda i, j: (0, j)),
                  pl.BlockSpec((tf, D), lambda i, j: (j, 0)),
                  const(g2), const(g3), const(wg),
                  pl.BlockSpec((tm, PD), lambda i, j: (i, 0)), const(wp)],
        out_specs=pl.BlockSpec((tm, D), lambda i, j: (i, 0)),
        out_shape=jax.ShapeDtypeStruct((S, D), F32),
        scratch_shapes=[pltpu.VMEM((tm, D), BF16), pltpu.VMEM((tm, D), F32)],
        compiler_params=_cparams(("parallel", "arbitrary")),
    )(h, g1, wu, wd, g2, g3, wg, p, wp)


def _t5_bucket_of(dist):
    n = jnp.maximum(dist, 0)
    max_exact = T5_BUCKETS // 2
    ratio = jnp.log(jnp.maximum(n, 1).astype(F32) / max_exact) / math.log(T5_MAX_DIST / max_exact)
    large = jnp.minimum(max_exact + (ratio * (T5_BUCKETS - max_exact)).astype(jnp.int32), T5_BUCKETS - 1)
    return jnp.where(n < max_exact, n, large)


def _bias_of_dist(bias, dist):
    bucket = _t5_bucket_of(dist)[None]
    out = jnp.zeros((bias.shape[1],) + dist.shape, F32)
    for b in range(T5_BUCKETS):
        out = jnp.where(bucket == b, bias[b].reshape((-1,) + (1,) * dist.ndim), out)
    return out


def _causal_tables(T, R):
    k = jnp.arange(R * T)[None, :, None]
    q = jnp.arange(T)[None, None, :]
    e = jnp.arange(R + 1)[:, None, None]
    return jnp.where(e * T + q >= k, 0.0, NEG_INF).astype(F32)[:, None]


def _t5_delta_tables(bias, T):
    n_delta = -(-(T5_FAR_DIST - 1) // T) + 2
    k = jnp.arange(T)[None, :, None]
    q = jnp.arange(T)[None, None, :]
    dist = jnp.arange(n_delta)[:, None, None] * T + q - k
    val = (_bias_of_dist(bias, dist) - bias[T5_BUCKETS - 1].reshape(-1, 1, 1, 1)) * LOG2E
    return jnp.transpose(jnp.where(dist[None] >= 0, val, NEG_INF), (1, 0, 2, 3))


def _band_table(bias, Ta, n_prev, max_rel, stride, inclusive):
    P = n_prev * Ta
    i = jnp.arange(Ta)[:, None]
    c = jnp.arange(P + Ta)[None, :]
    rel = i + P - c
    ok = (rel >= 0) & ((rel <= max_rel) if inclusive else (rel < max_rel))
    return jnp.where(ok[None], _bias_of_dist(bias, rel * stride), NEG_INF)


def _tile(S, pref):
    t = min(pref, S)
    assert S % t == 0
    return t


def even_mixer_core(h, g_pre, w_in, cos, sin, w_uq, q_norm, w_ukv, kv_norm, forget_bias, T):
    S = h.shape[0]
    scale = MLA_QK_DIM ** -0.5 * LOG2E
    half = MLA_ROPE_DIM // 2
    cos_t, sin_t = jnp.transpose(cos), jnp.transpose(sin)
    ones = jnp.ones((MLA_NOPE_DIM, S), F32)
    cq = jnp.concatenate([ones, cos_t, cos_t], axis=0) * scale
    sq = jnp.concatenate([0.0 * ones, -sin_t, sin_t], axis=0) * scale
    ck = jnp.concatenate([cos, cos], axis=1)
    sk = jnp.concatenate([-sin, sin], axis=1)
    wq = jnp.transpose(w_uq.reshape(MLA_Q_RANK, MLA_HEADS, MLA_QK_DIM), (1, 2, 0))
    swap = np.concatenate([np.arange(MLA_NOPE_DIM), MLA_NOPE_DIM + half + np.arange(half),
                           MLA_NOPE_DIM + np.arange(half)])
    wqs = wq[:, swap, :]
    wkv = jnp.transpose(w_ukv.reshape(MLA_KV_RANK, MLA_HEADS, MLA_NOPE_DIM + MLA_V_DIM), (1, 0, 2))
    wk = jnp.pad(wkv[:, :, :MLA_NOPE_DIM], ((0, 0), (0, 0), (0, MLA_ROPE_DIM)))
    wv = jnp.transpose(wkv[:, :, MLA_NOPE_DIM:], (0, 2, 1))
    e96 = jnp.pad(jnp.eye(MLA_ROPE_DIM, dtype=F32), ((0, 0), (MLA_NOPE_DIM, 0)))
    qm, km, vm, fq, fk, fv, logf, st = even_prep(
        h, g_pre, w_in, wq.astype(BF16), wqs.astype(BF16), wk.astype(BF16), e96.astype(BF16), wv.astype(BF16),
        jnp.eye(MLA_QK_DIM, dtype=BF16), q_norm[None, :], kv_norm[None, :], forget_bias[None, :],
        cq, sq, ck, sk, _tile(S, 512))
    neg_f = cumsum_split(jnp.transpose(logf))
    hps = MLA_HEADS
    causal = _causal_tables(T, 2 if S >= 2 * T else 1)
    r_mla, fast_mla = _reference_rows(st[0], st[1], st[2], 0.0, 0.0, 0.0, T, hps)
    o_mla = flash_causal(qm[None], r_mla, km, vm, causal, fast_mla, T, hps)
    r_fox, fast_fox = _reference_rows(st[3], st[4], st[5], jnp.sum(neg_f.astype(F32), axis=0), 0.0, 0.0, T, hps)
    fk_aug = jnp.concatenate([fk, jnp.transpose(neg_f, (1, 2, 0)),
                              jnp.zeros((FOX_HEADS, S, FOX_PAD - FOX_SPLIT), BF16), _stat_cols(FOX_HEADS, S)], axis=2)
    o_fox = flash_causal(fq[None], r_fox, fk_aug, fv, causal, fast_fox, T, hps)
    return o_mla, o_fox


def _odd_deint_plan():
    G = len(DIL_GROUPS)
    return tuple(((b * G + g) * DIL_GW, DIL_GW, d) for g, (_, d) in enumerate(DIL_GROUPS) if d > 1 for b in range(3))


def odd_mixer_core(y, deint, t5_bias, pos_k, pos_v, w1k, w2k, w1v, w2v, T):
    S = y.shape[0]
    G = len(DIL_GROUPS)
    Ta = 256
    o_dil, lse_dil = [], []
    for g, (w, d) in enumerate(DIL_GROUPS):
        ta = _tile(S // d, Ta)
        sb = min(BAND_SUB, ta)
        assert w // d <= sb <= ta
        btab = _band_table(t5_bias[:, g * DIL_HEADS:(g + 1) * DIL_HEADS], sb, 1, w // d, d, True)
        if d == 1:
            o, lse = band_attention(y[None], y[None], y[None], btab, ta, 1, True,
                                    widths=(DIL_GW, DIL_GW), col_blocks=(g, G + g, 2 * G + g))
            o, lse = o[0], lse[0]
        else:
            q, k, v = (deint[(b * G + g) * DIL_GW] for b in range(3))
            o, lse = band_attention(q, k, v, btab, ta, 1, True)
        o_dil.append(o)
        lse_dil.append(lse)
    nq_block = 3 * G
    gate_block = nq_block + 1
    base = (gate_block + 3) * NSA_W
    k_cmp, v_cmp, k_sel, v_sel, k_win, v_win = (y[:, base + i * HEAD_DIM:base + (i + 1) * HEAD_DIM] for i in range(6))
    bias_nsa = t5_bias[:, G * DIL_HEADS:]
    n_chunk = S // NSA_CMP_STRIDE
    cw = NSA_CMP_STRIDE * HEAD_DIM
    kc, vc = nsa_compress(k_cmp.reshape(n_chunk, cw), v_cmp.reshape(n_chunk, cw),
                          pos_k.reshape(2, cw), pos_v.reshape(2, cw),
                          w1k.reshape(2, cw, -1).astype(BF16), w2k.astype(BF16),
                          w1v.reshape(2, cw, -1).astype(BF16), jnp.transpose(w2v).astype(BF16))
    n_sel = S // NSA_SEL_BLOCK
    ci = np.arange(n_chunk)[None, :] * NSA_CMP_STRIDE
    sj = np.arange(n_sel)[:, None] * NSA_SEL_BLOCK
    overlap_t = jnp.asarray(((ci < sj + NSA_SEL_BLOCK) & (ci + NSA_CMP_LEN > sj)).astype(np.float32), BF16)
    nbg = min(n_sel, NSA_SEL_GROUP_BLOCKS)
    o_cmp, qaug, st, kaug, v_sel_t = nsa_cmp_topk(y, nq_block, k_sel.astype(BF16), v_sel.astype(BF16), kc, vc,
                                                  overlap_t, _tile(S, 256), nbg)
    delta = (bias_nsa - bias_nsa[T5_BUCKETS - 1]) * LOG2E
    r_sel, fast_sel = _reference_rows(st[0], st[1, :1], st[2], 0.0, jnp.max(delta, axis=0)[:, None],
                                      delta[0][:, None], T, NSA_HEADS)
    o_sel = flash_causal(qaug, r_sel, kaug[None], v_sel_t[None], _t5_delta_tables(bias_nsa, T), fast_sel, T,
                         hps=NSA_HEADS, key_group_tokens=nbg * NSA_SEL_BLOCK)
    Tw = _tile(S, NSA_WINDOW)
    n_prev = NSA_WINDOW // Tw
    wtab = _band_table(bias_nsa, Tw, n_prev, NSA_WINDOW, 1, False)
    o_win = band_attention(y[None], k_win[None], v_win[None], wtab, Tw, n_prev, False,
                           widths=(NSA_W, HEAD_DIM), col_blocks=(nq_block, 0, 0))[0]
    return o_dil, lse_dil, o_cmp, o_sel, o_win, gate_block


def _pad_cols(w, n):
    return jnp.pad(w, ((0, 0), (0, n - w.shape[1])))


def _even_w_in(w):
    cq, ckv, kr, fq, fk, fv, fl = jnp.split(w, np.cumsum([256, 128, 32, 512, 512, 512])[:], axis=1)
    half = MLA_ROPE_DIM // 2
    kr_sw = jnp.concatenate([kr[:, half:], kr[:, :half]], axis=1)
    head = _pad_cols(jnp.concatenate([cq, ckv, kr, kr_sw, fl], axis=1), 512)
    return jnp.concatenate([head, fq, fk, fv], axis=1).astype(BF16)


def _odd_w_in(w):
    n_main = 3 * 3 * DIL_GW + NSA_W
    main, six = w[:, :n_main], w[:, n_main:n_main + 6 * HEAD_DIM]
    gl = w[:, n_main + 6 * HEAD_DIM:]
    gl = jnp.transpose(gl.reshape(-1, NSA_HEADS, 3), (0, 2, 1))
    gl = jnp.repeat(gl[..., None], HEAD_DIM, axis=-1).reshape(w.shape[0], 3 * NSA_W)
    return _pad_cols(jnp.concatenate([main, gl, six], axis=1), ODD_IN_PAD).astype(BF16)


def _trunk(x, p, positions, t5_bias, ev_w_in, ev_q_norm, ev_w_uq, ev_kv_norm, ev_w_ukv, ev_forget_bias,
           ev_w_out, od_w_in, od_cmp_pos_k, od_cmp_pos_v, od_cmp_w1_k, od_cmp_w2_k, od_cmp_w1_v, od_cmp_w2_v,
           od_w_out, norm_mix_pre, norm_mix_post, norm_mlp_pre, norm_mlp_post, w_mlp_up, w_mlp_down,
           ple_norm, w_ple_gate, w_ple_proj):
    S, D = x.shape
    depth = p.shape[0]
    T = _tile(S, 512)
    tm = _tile(S, 512)
    inv_freq = ROPE_THETA ** (-jnp.arange(0, MLA_ROPE_DIM, 2, dtype=F32) / MLA_ROPE_DIM)
    angles = positions.astype(F32)[:, None] * inv_freq
    cos, sin = jnp.cos(angles), jnp.sin(angles)
    h = x
    for i in range(depth):
        j = i // 2
        if i % 2 == 0:
            o_mla, o_fox = even_mixer_core(h, norm_mix_pre[i][None], _even_w_in(ev_w_in[j]), cos, sin, ev_w_uq[j],
                                           ev_q_norm[j], ev_w_ukv[j], ev_kv_norm[j], ev_forget_bias[j], T)
            wo = ev_w_out[j].astype(BF16)
            na = MLA_HEADS * MLA_V_DIM
            h = outproj_even(o_mla, o_fox, wo[:na], wo[na:], norm_mix_post[i][None], h, tm)
        else:
            plan = _odd_deint_plan()
            y, *parts = odd_inproj(h, norm_mix_pre[i][None], _odd_w_in(od_w_in[j]), _tile(S, 256), plan)
            o_dil, lse_dil, o_cmp, o_sel, o_win, gate_block = odd_mixer_core(
                y, {col: a for (col, _, _), a in zip(plan, parts)}, t5_bias, od_cmp_pos_k[j], od_cmp_pos_v[j], od_cmp_w1_k[j], od_cmp_w2_k[j],
                od_cmp_w1_v[j], od_cmp_w2_v[j], T)
            wo = od_w_out[j].astype(BF16)
            h = outproj_odd(o_dil, lse_dil, o_cmp, o_sel, o_win, y, gate_block, wo[:DIL_GW], wo[DIL_GW:],
                            norm_mix_post[i][None], h, tm)
        h = mlp_ple(h, norm_mlp_pre[i][None], w_mlp_up[i].astype(BF16), w_mlp_down[i].astype(BF16),
                    norm_mlp_post[i][None], ple_norm[i][None], w_ple_gate[i].astype(BF16), p[i],
                    w_ple_proj[i].astype(BF16), _tile(S, 1024), 512)
    return h


def kernel(x, p, positions, t5_bias, ev_w_in, ev_q_norm, ev_w_uq, ev_kv_norm, ev_w_ukv, ev_forget_bias, ev_w_out, od_w_in, od_cmp_pos_k, od_cmp_pos_v, od_cmp_w1_k, od_cmp_w2_k, od_cmp_w1_v, od_cmp_w2_v, od_w_out, norm_mix_pre, norm_mix_post, norm_mlp_pre, norm_mlp_post, w_mlp_up, w_mlp_down, ple_norm, w_ple_gate, w_ple_proj):
    params = (t5_bias, ev_w_in, ev_q_norm, ev_w_uq, ev_kv_norm, ev_w_ukv, ev_forget_bias, ev_w_out, od_w_in,
              od_cmp_pos_k, od_cmp_pos_v, od_cmp_w1_k, od_cmp_w2_k, od_cmp_w1_v, od_cmp_w2_v, od_w_out,
              norm_mix_pre, norm_mix_post, norm_mlp_pre, norm_mlp_post, w_mlp_up, w_mlp_down, ple_norm,
              w_ple_gate, w_ple_proj)
    outs = [_trunk(x[b], p[:, b], positions[b], *params) for b in range(x.shape[0])]
    return jnp.stack(outs).astype(x.dtype)
```

```python
import functools
import math

import numpy as np
import jax
import jax.numpy as jnp
from jax import lax
from jax.experimental import pallas as pl
from jax.experimental.pallas import tpu as pltpu

F32 = jnp.float32
BF16 = jnp.bfloat16

HEAD_DIM = 64
RMS_EPS = 1e-6
NEG_INF = -1e30
SEL_NEG = -(2.0 ** 99)

MLA_HEADS = 8
MLA_NOPE_DIM = 64
MLA_ROPE_DIM = 32
MLA_V_DIM = 64
MLA_Q_RANK = 256
MLA_KV_RANK = 128
MLA_QK_DIM = MLA_NOPE_DIM + MLA_ROPE_DIM
ROPE_THETA = 10000.0
FOX_HEADS = 8
FOX_SPLIT = 3
FOX_PAD = 16
LOG2E = math.log2(math.e)
STAT_ROWS = 16
FAST_OVER = 64.0
FAST_MARGIN = 150.0
P_CHUNK = 32
BAND_SUB = 128
DIL_GROUPS = ((128, 1), (512, 4), (2048, 16))
DIL_HEADS = 4
DIL_GW = DIL_HEADS * HEAD_DIM
NSA_HEADS = 4
NSA_W = NSA_HEADS * HEAD_DIM
NSA_CMP_LEN = 32
NSA_CMP_STRIDE = 16
NSA_SEL_BLOCK = 64
NSA_TOPK = 16
NSA_WINDOW = 512
NSA_FORCE_SCORE = 1e9
NSA_SEL_GROUP_BLOCKS = 128
T5_BUCKETS = 32
T5_MAX_DIST = 2048
T5_FAR_DIST = 1536
assert 16 * math.log(T5_FAR_DIST / 16) / math.log(T5_MAX_DIST / 16) > 15.02

EVEN_IN_PAD = 2048
ODD_IN_PAD = 3840

VMEM_LIMIT_BYTES = 56 * 1024 * 1024


def _cparams(sem):
    return pltpu.CompilerParams(dimension_semantics=sem, vmem_limit_bytes=VMEM_LIMIT_BYTES)


def _rms(x, g):
    return x * lax.rsqrt(jnp.mean(x * x, axis=-1, keepdims=True) + RMS_EPS) * g


def _dot(a, b):
    return jnp.dot(a, b, preferred_element_type=F32)


def _dot_nt(a, b):
    return lax.dot_general(a, b, (((1,), (1,)), ((), ())), preferred_element_type=F32)


LANES = 128


def _odd_inproj_kernel(h_ref, g_ref, w_ref, y_ref, *rest, deint):
    out_refs, scr = rest[:-1], rest[-1]
    y = _dot(_rms(h_ref[...], g_ref[...]).astype(BF16), w_ref[...])
    y_ref[...] = y
    tm = y.shape[0]
    for out, (col, width, d) in zip(out_refs, deint):
        n = width // LANES
        for c in range(n):
            scr[c] = y[:, col + c * LANES:col + (c + 1) * LANES]
        for r in range(d):
            out[r] = jnp.concatenate([scr[c, pl.ds(r, tm // d, stride=d), :] for c in range(n)], axis=1)


def odd_inproj(h, g, w, tm, deint):
    S, D = h.shape
    N = w.shape[1]
    out_shape = [jax.ShapeDtypeStruct((S, N), F32)]
    out_specs = [pl.BlockSpec((tm, N), lambda i: (i, 0))]
    for _, width, d in deint:
        out_shape.append(jax.ShapeDtypeStruct((d, S // d, width), F32))
        out_specs.append(pl.BlockSpec((d, tm // d, width), lambda i: (0, i, 0)))
    return pl.pallas_call(
        functools.partial(_odd_inproj_kernel, deint=deint),
        grid=(S // tm,),
        in_specs=[pl.BlockSpec((tm, D), lambda i: (i, 0)),
                  pl.BlockSpec((1, D), lambda i: (0, 0)),
                  pl.BlockSpec((D, N), lambda i: (0, 0))],
        out_specs=out_specs,
        out_shape=out_shape,
        scratch_shapes=[pltpu.VMEM((max(wd for _, wd, _ in deint) // LANES, tm, LANES), F32)],
        compiler_params=_cparams(("parallel",)),
    )(h, g, w)


def _even_prep_kernel(h_ref, g_ref, win_ref, wq_ref, wqs_ref, wk_ref, e_ref, wv_ref, eye_ref, qn_ref, kvn_ref,
                      fb_ref, cq_ref, sq_ref, ck_ref, sk_ref,
                      qm_ref, km_ref, vm_ref, fq_ref, fk_ref, fv_ref, lf_ref, st_ref):
    y = _dot(_rms(h_ref[...], g_ref[...]).astype(BF16), win_ref[...])
    tm = y.shape[0]
    c_q = y[:, 0:MLA_Q_RANK]
    c_kv = y[:, MLA_Q_RANK:MLA_Q_RANK + MLA_KV_RANK]
    o = MLA_Q_RANK + MLA_KV_RANK
    k_r = y[:, o:o + MLA_ROPE_DIM]
    k_rs = y[:, o + MLA_ROPE_DIM:o + 2 * MLA_ROPE_DIM]
    f_logit = y[:, o + 2 * MLA_ROPE_DIM:o + 2 * MLA_ROPE_DIM + FOX_HEADS]
    nq = _rms(c_q, qn_ref[...]).astype(BF16)
    nkv = _rms(c_kv, kvn_ref[...]).astype(BF16)
    k_rot = (k_r * ck_ref[...] + k_rs * sk_ref[...]).astype(BF16)
    k_rot96 = _dot(k_rot, e_ref[...])
    cq = cq_ref[...]
    sq = sq_ref[...]
    eye = eye_ref[...]
    row = lax.broadcasted_iota(jnp.int32, (STAT_ROWS, tm), 0)
    ones_row = jnp.where(row < 1, 1.0, 0.0).astype(BF16)
    col = lax.broadcasted_iota(jnp.int32, (tm, STAT_ROWS), 1)
    stat_cols = jnp.where(col < 3, 1.0, 0.0).astype(BF16)
    stats = [[] for _ in range(6)]

    def record(slot, q_t, k):
        q_t = q_t.astype(F32)
        k_t = _dot_nt(eye[:k.shape[1], :k.shape[1]], k)
        stats[slot].append(jnp.sum(q_t * q_t, axis=0, keepdims=True))
        stats[slot + 1].append(jnp.sum(k_t * k_t, axis=0, keepdims=True))
        stats[slot + 2].append(jnp.sum(q_t * k_t, axis=0, keepdims=True))

    for h in range(MLA_HEADS):
        q = (_dot_nt(wq_ref[h], nq) * cq + _dot_nt(wqs_ref[h], nq) * sq).astype(BF16)
        k = (_dot(nkv, wk_ref[h]) + k_rot96).astype(BF16)
        qm_ref[h] = q
        km_ref[h] = jnp.concatenate([k, stat_cols], axis=1)
        vm_ref[h] = jnp.concatenate([_dot_nt(wv_ref[h], nkv).astype(BF16), ones_row], axis=0)
        record(0, q, k)
    base = 512
    fw = FOX_HEADS * HEAD_DIM
    eye64 = eye[:HEAD_DIM, :HEAD_DIM]
    ones_rows = jnp.where(row < FOX_SPLIT, 1.0, 0.0).astype(BF16)
    for h in range(FOX_HEADS):
        lo = base + h * HEAD_DIM
        fq = _dot_nt(eye64, (y[:, lo:lo + HEAD_DIM] * (HEAD_DIM ** -0.5 * LOG2E)).astype(BF16)).astype(BF16)
        fk = y[:, lo + fw:lo + fw + HEAD_DIM].astype(BF16)
        fq_ref[h] = jnp.concatenate([fq, ones_rows], axis=0)
        fk_ref[h] = fk
        fv = y[:, lo + 2 * fw:lo + 2 * fw + HEAD_DIM].astype(BF16)
        fv_ref[h] = jnp.concatenate([_dot_nt(eye64, fv).astype(BF16), ones_row], axis=0)
        record(3, fq, fk)
    for i in range(6):
        st_ref[i] = jnp.concatenate(stats[i], axis=0)
    z = f_logit + fb_ref[...]
    lf_ref[...] = jnp.minimum(z, 0.0) - jnp.log1p(jnp.exp(-jnp.abs(z)))


def even_prep(h, g, w_in, wq, wqs, wk, e96, wv, eye, qn, kvn, fb, cq, sq, ck, sk, tm):
    S, D = h.shape
    full = lambda a: pl.BlockSpec(a.shape, lambda i: (0,) * a.ndim)
    rows = lambda w: pl.BlockSpec((tm, w), lambda i: (i, 0))
    cols = lambda w: pl.BlockSpec((w, tm), lambda i: (0, i))
    heads = lambda n, w: pl.BlockSpec((n, tm, w), lambda i: (0, i, 0))
    heads_t = lambda n, w: pl.BlockSpec((n, w, tm), lambda i: (0, 0, i))
    out_shape = [jax.ShapeDtypeStruct((MLA_HEADS, MLA_QK_DIM, S), BF16),
                 jax.ShapeDtypeStruct((MLA_HEADS, S, MLA_QK_DIM + STAT_ROWS), BF16),
                 jax.ShapeDtypeStruct((MLA_HEADS, MLA_V_DIM + STAT_ROWS, S), BF16),
                 jax.ShapeDtypeStruct((FOX_HEADS, HEAD_DIM + FOX_PAD, S), BF16),
                 jax.ShapeDtypeStruct((FOX_HEADS, S, HEAD_DIM), BF16),
                 jax.ShapeDtypeStruct((FOX_HEADS, HEAD_DIM + STAT_ROWS, S), BF16),
                 jax.ShapeDtypeStruct((S, FOX_HEADS), F32),
                 jax.ShapeDtypeStruct((6, MLA_HEADS, S), F32)]
    out_specs = [heads_t(MLA_HEADS, MLA_QK_DIM), heads(MLA_HEADS, MLA_QK_DIM + STAT_ROWS),
                 heads_t(MLA_HEADS, MLA_V_DIM + STAT_ROWS),
                 heads_t(FOX_HEADS, HEAD_DIM + FOX_PAD), heads(FOX_HEADS, HEAD_DIM),
                 heads_t(FOX_HEADS, HEAD_DIM + STAT_ROWS),
                 rows(FOX_HEADS), heads_t(6, MLA_HEADS)]
    return pl.pallas_call(
        _even_prep_kernel,
        grid=(S // tm,),
        in_specs=[rows(D), full(g), full(w_in), full(wq), full(wqs), full(wk), full(e96), full(wv), full(eye),
                  full(qn), full(kvn), full(fb), cols(MLA_QK_DIM), cols(MLA_QK_DIM), rows(MLA_ROPE_DIM),
                  rows(MLA_ROPE_DIM)],
        out_specs=out_specs,
        out_shape=out_shape,
        compiler_params=_cparams(("parallel",)),
    )(h, g, w_in, wq, wqs, wk, e96, wv, eye, qn, kvn, fb, cq, sq, ck, sk)


def _cumsum_split_kernel(x_ref, o_ref):
    x = x_ref[...]
    n = x.shape[1]
    lane = lax.broadcasted_iota(jnp.int32, x.shape, 1)
    shift = 1
    while shift < n:
        x = x + jnp.where(lane >= shift, pltpu.roll(x, shift, axis=1), 0.0)
        shift *= 2
    r = -x * LOG2E
    for i in range(FOX_SPLIT):
        part = r.astype(BF16)
        o_ref[i] = part
        r = r - part.astype(F32)


def cumsum_split(x):
    return pl.pallas_call(
        _cumsum_split_kernel,
        out_shape=jax.ShapeDtypeStruct((FOX_SPLIT,) + x.shape, BF16),
        compiler_params=_cparams(None),
    )(x)


def _flash_kernel(qt_ref, kt_ref, fast_ref, q_ref, r_ref, k_ref, v_ref, bt_ref, o_ref, m_scr, acc_scr, p_scr, *,
                  hps, shared_kv, n_delta, nt, R):
    step = pl.program_id(1)
    qi = qt_ref[step]
    ki = kt_ref[step]
    fast = fast_ref[pl.program_id(0) * nt + qi] != 0
    near = qi - R * ki < n_delta - 1

    @pl.when(ki == 0)
    def _():
        m_scr[...] = jnp.full(m_scr.shape, -3e38, F32)
        acc_scr[...] = jnp.zeros(acc_scr.shape, F32)

    hb = bt_ref.shape[1]
    Tk = k_ref.shape[1]
    Dv = v_ref.shape[1] - STAT_ROWS

    def logits(h, with_table):
        q = jnp.concatenate([q_ref[0, h], r_ref[h]], axis=0)
        s = _dot(k_ref[0 if shared_kv else h], q)
        return s + bt_ref[0, h if hb > 1 else 0] if with_table else s

    def store_p(h, s, m):
        for c in range(Tk // P_CHUNK):
            rows = slice(c * P_CHUNK, (c + 1) * P_CHUNK)
            x = s[rows] if m is None else s[rows] - m
            p_scr[h, rows, :] = jnp.exp2(x).astype(BF16)

    def fast_update(with_table):
        acc_prev = [acc_scr[h] for h in range(hps)]
        for h in range(hps):
            store_p(h, logits(h, with_table), None)
        for h in range(hps):
            acc_scr[h] = acc_prev[h] + _dot(v_ref[0 if shared_kv else h], p_scr[h])

    def safe_update(with_table):
        m_prev = [m_scr[h] for h in range(hps)]
        acc_prev = [acc_scr[h] for h in range(hps)]
        ss = [logits(h, with_table) for h in range(hps)]
        ms = [jnp.maximum(m_prev[h], jnp.max(ss[h], axis=0, keepdims=True)) for h in range(hps)]
        for h in range(hps):
            store_p(h, ss[h], ms[h])
        for h in range(hps):
            alpha = jnp.exp2(m_prev[h] - ms[h])
            acc_scr[h] = alpha * acc_prev[h] + _dot(v_ref[0 if shared_kv else h], p_scr[h])
            m_scr[h] = ms[h]

    for take_fast, update in ((True, fast_update), (False, safe_update)):
        for with_table in (True, False):
            cond = jnp.logical_and(fast == take_fast, near == with_table)
            pl.when(cond)(functools.partial(update, with_table))

    @pl.when(ki == qi // R)
    def _():
        outs = [(acc_scr[h, :Dv, :] / acc_scr[h, Dv:Dv + 1, :]).T for h in range(hps)]
        o_ref[...] = jnp.concatenate(outs, axis=1).astype(o_ref.dtype)


def flash_causal(qT, r_rows, k, vT, btab, fast, T, hps, key_group_tokens=None, out_dtype=BF16):
    G, H, Dq, S = qT.shape
    Dk = Dq + STAT_ROWS
    Hk, Dv, _ = vT.shape
    Dv -= STAT_ROWS
    shared_kv = Hk == 1
    n_delta, Hb, Tk = btab.shape[0], btab.shape[1], btab.shape[2]
    R = Tk // T
    nt = S // T
    qt = np.array([i for i in range(nt) for _ in range(i // R + 1)], np.int32)
    kt = np.array([j for i in range(nt) for j in range(i // R + 1)], np.int32)
    kv_h = 1 if shared_kv else hps
    assert G == 1 or R == 1
    tiles_per_group = (key_group_tokens // T) if G > 1 else 1

    def q_map(g, s, qt, kt, fast):
        return (kt[s] // tiles_per_group if G > 1 else 0, g, 0, qt[s])

    def bt_map(g, s, qt, kt, fast):
        return (jnp.minimum(qt[s] - R * kt[s], n_delta - 1), g if Hb > hps else 0, 0, 0)

    in_specs = [pl.BlockSpec((1, hps, Dq, T), q_map),
                pl.BlockSpec((hps, STAT_ROWS, T), lambda g, s, qt, kt, fast: (g, 0, qt[s])),
                pl.BlockSpec((kv_h, Tk, Dk), lambda g, s, qt, kt, fast: (0 if shared_kv else g, kt[s], 0)),
                pl.BlockSpec((kv_h, Dv + STAT_ROWS, Tk), lambda g, s, qt, kt, fast: (0 if shared_kv else g, 0, kt[s])),
                pl.BlockSpec((1, min(Hb, hps), Tk, T), bt_map)]
    grid_spec = pltpu.PrefetchScalarGridSpec(
        num_scalar_prefetch=3,
        grid=(H // hps, len(qt)),
        in_specs=in_specs,
        out_specs=pl.BlockSpec((T, hps * Dv), lambda g, s, qt, kt, fast: (qt[s], g)),
        scratch_shapes=[pltpu.VMEM((hps, 1, T), F32), pltpu.VMEM((hps, Dv + STAT_ROWS, T), F32),
                        pltpu.VMEM((hps, Tk, T), BF16)],
    )
    return pl.pallas_call(
        functools.partial(_flash_kernel, hps=hps, shared_kv=shared_kv, n_delta=n_delta, nt=nt, R=R),
        grid_spec=grid_spec,
        out_shape=jax.ShapeDtypeStruct((S, H * Dv), out_dtype),
        compiler_params=_cparams(("parallel", "arbitrary")),
    )(jnp.asarray(qt), jnp.asarray(kt), fast, qT, r_rows, k, vT, btab)


def _reference_rows(qn2, kn2, l_self, extra, bias_max, bias_self, T, hps):
    H, S = qn2.shape
    kmax = jnp.sqrt(jnp.max(kn2, axis=1, keepdims=True))
    bound = jnp.sqrt(qn2) * kmax + bias_max + extra
    ok = bound - (l_self + bias_self + extra) <= FAST_MARGIN
    fast = jnp.all(ok.reshape(H // hps, hps, S // T, T), axis=(1, 3)).astype(jnp.int32).reshape(-1)
    r = FAST_OVER - bound
    parts = []
    for _ in range(3):
        part = r.astype(BF16)
        parts.append(part)
        r = r - part.astype(F32)
    rows = jnp.stack(parts + [jnp.zeros_like(parts[0])] * (STAT_ROWS - 3), axis=1)
    return rows, fast


def _stat_cols(Hk, S):
    return jnp.broadcast_to((jnp.arange(STAT_ROWS) < 3).astype(BF16), (Hk, S, STAT_ROWS))


def _band_kernel(*refs, n_prev, Ta, H, shared_kv, want_lse):
    q_ref = refs[0]
    k_refs = refs[1:2 + n_prev]
    v_refs = refs[2 + n_prev:3 + 2 * n_prev]
    bt_ref = refs[3 + 2 * n_prev]
    o_ref = refs[4 + 2 * n_prev]
    lse_ref = refs[5 + 2 * n_prev] if want_lse else None
    ai = pl.program_id(1)
    k = jnp.concatenate([r[...] for r in k_refs], axis=0).astype(BF16)
    v = jnp.concatenate([r[...] for r in v_refs], axis=0).astype(BF16)
    sb = bt_ref.shape[1]
    P = bt_ref.shape[2] - sb
    q = q_ref[...]
    for s_i in range(Ta // sb):
        first = n_prev * Ta + s_i * sb - P
        col = lax.broadcasted_iota(jnp.int32, (1, P + sb), 1)
        col_valid = (ai - n_prev) * Ta + first + col >= 0
        outs, lses = [], []
        for h in range(H):
            hk = 0 if shared_kv else h
            qh = (q[s_i * sb:(s_i + 1) * sb, h * HEAD_DIM:(h + 1) * HEAD_DIM] * (HEAD_DIM ** -0.5)).astype(BF16)
            s = _dot_nt(qh, k[first:first + P + sb, hk * HEAD_DIM:(hk + 1) * HEAD_DIM]) + bt_ref[h]
            s = jnp.where(col_valid, s, NEG_INF)
            m = jnp.max(s, axis=-1, keepdims=True)
            p = jnp.exp(s - m)
            l = jnp.sum(p, axis=-1, keepdims=True)
            outs.append(_dot(p.astype(BF16), v[first:first + P + sb, hk * HEAD_DIM:(hk + 1) * HEAD_DIM]) / l)
            if want_lse:
                lses.append(jnp.broadcast_to(m + jnp.log(l), (sb, HEAD_DIM)))
        o_ref[s_i * sb:(s_i + 1) * sb, :] = jnp.concatenate(outs, axis=1)
        if want_lse:
            lse_ref[s_i * sb:(s_i + 1) * sb, :] = jnp.concatenate(lses, axis=1)


def band_attention(q, k, v, btab, Ta, n_prev, want_lse, widths=None, col_blocks=(0, 0, 0)):
    R, A = q.shape[:2]
    QW, KW = widths if widths else (q.shape[2], k.shape[2])
    H = QW // HEAD_DIM
    shared_kv = KW == HEAD_DIM
    nA = A // Ta
    qc, kc, vc = col_blocks

    def prev_map(p, c):
        return lambda r, a: (r, jnp.maximum(a - n_prev + p, 0), c)

    cur = lambda r, a: (r, a, 0)
    k_specs = ([pl.BlockSpec((None, Ta, KW), prev_map(p, kc)) for p in range(n_prev)]
               + [pl.BlockSpec((None, Ta, KW), lambda r, a: (r, a, kc))])
    v_specs = ([pl.BlockSpec((None, Ta, KW), prev_map(p, vc)) for p in range(n_prev)]
               + [pl.BlockSpec((None, Ta, KW), lambda r, a: (r, a, vc))])
    out_shape = [jax.ShapeDtypeStruct((R, A, QW), F32)]
    out_specs = [pl.BlockSpec((None, Ta, QW), cur)]
    if want_lse:
        out_shape.append(jax.ShapeDtypeStruct((R, A, QW), F32))
        out_specs.append(pl.BlockSpec((None, Ta, QW), cur))
    res = pl.pallas_call(
        functools.partial(_band_kernel, n_prev=n_prev, Ta=Ta, H=H, shared_kv=shared_kv, want_lse=want_lse),
        grid=(R, nA),
        in_specs=[pl.BlockSpec((None, Ta, QW), lambda r, a: (r, a, qc))] + k_specs + v_specs
                 + [pl.BlockSpec(btab.shape, lambda r, a: (0, 0, 0))],
        out_specs=out_specs,
        out_shape=out_shape,
        compiler_params=_cparams(("parallel", "arbitrary")),
    )(q, *([k] * (n_prev + 1)), *([v] * (n_prev + 1)), btab)
    return res if want_lse else res[0]


def _gelu_tanh(x):
    return 0.5 * x * (1.0 + jnp.tanh(math.sqrt(2.0 / math.pi) * (x + 0.044715 * (x * x * x))))


def _compress_kernel(ks_ref, vs_ref, pk_ref, pv_ref, w1k_ref, w2k_ref, w1v_ref, w2vt_ref, kc_ref, vct_ref):
    def hidden(src, pos, w1):
        x = src[...]
        n = x.shape[0]
        first = _dot((x + pos[0:1, :]).astype(BF16), w1[0])
        second = _dot((x + pos[1:2, :]).astype(BF16), w1[1])
        return _gelu_tanh(first + pltpu.roll(second, n - 1, axis=0)).astype(BF16)

    kc_ref[...] = _dot(hidden(ks_ref, pk_ref, w1k_ref), w2k_ref[...]).astype(BF16)
    vct_ref[...] = _dot_nt(w2vt_ref[...], hidden(vs_ref, pv_ref, w1v_ref)).astype(BF16)


def nsa_compress(k_chunks, v_chunks, pos_k, pos_v, w1k, w2k, w1v, w2v_t):
    n = k_chunks.shape[0]
    return pl.pallas_call(
        _compress_kernel,
        out_shape=[jax.ShapeDtypeStruct((n, HEAD_DIM), BF16), jax.ShapeDtypeStruct((HEAD_DIM, n), BF16)],
        compiler_params=_cparams(None),
    )(k_chunks, v_chunks, pos_k, pos_v, w1k, w2k, w1v, w2v_t)


def _cmp_topk_kernel(nq_ref, ks_ref, vs_ref, kc_ref, vct_ref, ovt_ref, eye_ref,
                     ocmp_ref, qaug_ref, st_ref, kaug_ref, vst_ref, *, Tq, n_cmp, n_sel, nbg):
    q0 = pl.program_id(0) * Tq
    t = q0 + lax.broadcasted_iota(jnp.int32, (n_cmp, Tq), 1)
    cmp_end = NSA_CMP_STRIDE * lax.broadcasted_iota(jnp.int32, (n_cmp, Tq), 0) + (NSA_CMP_LEN - 1)
    c_neg = jnp.where(cmp_end <= t, 0.0, NEG_INF)
    sees_any = q0 + lax.broadcasted_iota(jnp.int32, (1, Tq), 1) >= NSA_CMP_LEN - 1
    kc = kc_ref[...]
    vct = vct_ref[...]
    q = nq_ref[...]
    eye = eye_ref[...]
    ks = ks_ref[...]
    ks_t = _dot_nt(eye, ks)
    tok = q0 + lax.broadcasted_iota(jnp.int32, (Tq, nbg), 0)
    blk = jnp.bitwise_and(jnp.right_shift(tok, NSA_SEL_BLOCK.bit_length() - 1), nbg - 1)
    onehot = jnp.where(blk == lax.broadcasted_iota(jnp.int32, (Tq, nbg), 1), 1.0, 0.0).astype(BF16)
    stat_cols = jnp.where(lax.broadcasted_iota(jnp.int32, (Tq, STAT_ROWS), 1) < 3, 1.0, 0.0).astype(BF16)
    kaug_ref[...] = jnp.concatenate([ks, onehot, stat_cols], axis=1)
    ones_row = jnp.where(lax.broadcasted_iota(jnp.int32, (STAT_ROWS, Tq), 0) < 1, 1.0, 0.0).astype(BF16)
    vst_ref[...] = jnp.concatenate([_dot_nt(eye, vs_ref[...]).astype(BF16), ones_row], axis=0)
    psum = jnp.zeros((n_cmp, Tq), F32)
    outs, q_ts, qn2, l_self = [], [], [], []
    for h in range(NSA_HEADS):
        qh = (q[:, h * HEAD_DIM:(h + 1) * HEAD_DIM] * (HEAD_DIM ** -0.5 * LOG2E)).astype(BF16)
        q_t = _dot_nt(eye, qh)
        q_ts.append(q_t.astype(BF16))
        qn2.append(jnp.sum(q_t * q_t, axis=0, keepdims=True))
        l_self.append(jnp.sum(q_t * ks_t, axis=0, keepdims=True))
        s = _dot(kc, q_ts[h]) + c_neg
        p = jnp.exp2(s - jnp.max(s, axis=0, keepdims=True))
        p = p * jnp.where(sees_any, 1.0 / jnp.sum(p, axis=0, keepdims=True), 0.0)
        psum = psum + p
        outs.append(_dot(vct, p.astype(BF16)).T)
    ocmp_ref[...] = jnp.concatenate(outs, axis=1)
    st_ref[0] = jnp.concatenate(qn2, axis=0)
    st_ref[1] = jnp.broadcast_to(jnp.sum(ks_t * ks_t, axis=0, keepdims=True), (NSA_HEADS, Tq))
    st_ref[2] = jnp.concatenate(l_self, axis=0)
    hi = psum.astype(BF16)
    lo = (psum - hi.astype(F32)).astype(BF16)
    imp = _dot(ovt_ref[...], hi) + _dot(ovt_ref[...], lo)
    t = q0 + lax.broadcasted_iota(jnp.int32, (n_sel, Tq), 1)
    j = lax.broadcasted_iota(jnp.int32, (n_sel, Tq), 0)
    start = j * NSA_SEL_BLOCK
    cur_blk = (start <= t) & (t < start + NSA_SEL_BLOCK)
    prev_blk = (start + NSA_SEL_BLOCK <= t) & (t < start + 2 * NSA_SEL_BLOCK)
    forced = (j == 0) | cur_blk | prev_blk
    score = jnp.where(forced, NSA_FORCE_SCORE, jnp.where(start <= t, imp, -NSA_FORCE_SCORE))
    jf = j.astype(F32)
    taken = jnp.float32(-3e38)
    for _ in range(min(NSA_TOPK, n_sel)):
        mx = jnp.max(score, axis=0, keepdims=True)
        first = jnp.min(jnp.where(score == mx, jf, float(n_sel)), axis=0, keepdims=True)
        score = jnp.where(jf == first, taken, score)
    sel_bias_t = jnp.where(score == taken, 0.0, SEL_NEG).astype(BF16)
    for h in range(NSA_HEADS):
        for g in range(n_sel // nbg):
            qaug_ref[g, h] = jnp.concatenate([q_ts[h], sel_bias_t[g * nbg:(g + 1) * nbg]], axis=0)


def nsa_cmp_topk(y, nq_block, k_sel, v_sel, kc, vc_t, overlap_t, Tq, nbg):
    S = y.shape[0]
    n_cmp = kc.shape[0]
    n_sel = S // NSA_SEL_BLOCK
    G = n_sel // nbg
    assert nbg & (nbg - 1) == 0 and NSA_SEL_BLOCK & (NSA_SEL_BLOCK - 1) == 0
    const = lambda a: pl.BlockSpec(a.shape, lambda i: (0,) * a.ndim)
    rows = lambda w: pl.BlockSpec((Tq, w), lambda i: (i, 0))
    eye = jnp.eye(HEAD_DIM, dtype=BF16)
    kw = HEAD_DIM + nbg + STAT_ROWS
    return pl.pallas_call(
        functools.partial(_cmp_topk_kernel, Tq=Tq, n_cmp=n_cmp, n_sel=n_sel, nbg=nbg),
        grid=(S // Tq,),
        in_specs=[pl.BlockSpec((Tq, NSA_W), lambda i: (i, nq_block)), rows(HEAD_DIM), rows(HEAD_DIM),
                  const(kc), const(vc_t), const(overlap_t), const(eye)],
        out_specs=[rows(NSA_W),
                   pl.BlockSpec((G, NSA_HEADS, HEAD_DIM + nbg, Tq), lambda i: (0, 0, 0, i)),
                   pl.BlockSpec((3, NSA_HEADS, Tq), lambda i: (0, 0, i)),
                   rows(kw),
                   pl.BlockSpec((HEAD_DIM + STAT_ROWS, Tq), lambda i: (0, i))],
        out_shape=[jax.ShapeDtypeStruct((S, NSA_W), F32),
                   jax.ShapeDtypeStruct((G, NSA_HEADS, HEAD_DIM + nbg, S), BF16),
                   jax.ShapeDtypeStruct((3, NSA_HEADS, S), F32),
                   jax.ShapeDtypeStruct((S, kw), BF16),
                   jax.ShapeDtypeStruct((HEAD_DIM + STAT_ROWS, S), BF16)],
        compiler_params=_cparams(("parallel",)),
    )(y, k_sel, v_sel, kc, vc_t, overlap_t, eye)


def _outproj_even_kernel(oa_ref, ob_ref, wa_ref, wb_ref, g_ref, h_ref, out_ref):
    m = _dot(oa_ref[...], wa_ref[...]) + _dot(ob_ref[...], wb_ref[...])
    out_ref[...] = h_ref[...] + _rms(m, g_ref[...])


def outproj_even(oa, ob, wa, wb, g, h, tm):
    S, D = h.shape
    rows = lambda w: pl.BlockSpec((tm, w), lambda i: (i, 0))
    full = lambda a: pl.BlockSpec(a.shape, lambda i: (0,) * a.ndim)
    return pl.pallas_call(
        _outproj_even_kernel,
        grid=(S // tm,),
        in_specs=[rows(oa.shape[1]), rows(ob.shape[1]), full(wa), full(wb), full(g), rows(D)],
        out_specs=rows(D),
        out_shape=jax.ShapeDtypeStruct((S, D), F32),
        compiler_params=_cparams(("parallel",)),
    )(oa, ob, wa, wb, g, h)


def _outproj_odd_kernel(o0_ref, o1_ref, o2_ref, l0_ref, l1_ref, l2_ref, oc_ref, os_ref, ow_ref,
                        gc_ref, gs_ref, gw_ref, wd_ref, wn_ref, g_ref, h_ref, out_ref, scr):
    tm = h_ref.shape[0]

    def rows_of(ref):
        if len(ref.shape) == 2:
            return ref[...]
        d, _, width = ref.shape
        for r in range(d):
            v = ref[r]
            for c in range(width // LANES):
                scr[c, pl.ds(r, tm // d, stride=d), :] = v[:, c * LANES:(c + 1) * LANES]
        return jnp.concatenate([scr[c] for c in range(width // LANES)], axis=1)

    l0, l1, l2 = rows_of(l0_ref), rows_of(l1_ref), rows_of(l2_ref)
    mx = jnp.maximum(jnp.maximum(l0, l1), l2)
    e0, e1, e2 = jnp.exp(l0 - mx), jnp.exp(l1 - mx), jnp.exp(l2 - mx)
    o_dil = (e0 * rows_of(o0_ref) + e1 * rows_of(o1_ref) + e2 * rows_of(o2_ref)) / (e0 + e1 + e2)
    o_nsa = (jax.nn.sigmoid(gc_ref[...]) * oc_ref[...] + jax.nn.sigmoid(gs_ref[...]) * os_ref[...].astype(F32)
             + jax.nn.sigmoid(gw_ref[...]) * ow_ref[...])
    m = _dot(o_dil.astype(BF16), wd_ref[...]) + _dot(o_nsa.astype(BF16), wn_ref[...])
    out_ref[...] = h_ref[...] + _rms(m, g_ref[...])


def outproj_odd(o_dil, lse_dil, o_cmp, o_sel, o_win, y, gate_block, wd, wn, g, h, tm):
    S, D = h.shape
    rows = lambda w: pl.BlockSpec((tm, w), lambda i: (i, 0))
    full = lambda a: pl.BlockSpec(a.shape, lambda i: (0,) * a.ndim)
    gate = lambda b: pl.BlockSpec((tm, NSA_W), lambda i: (i, gate_block + b))

    def dil(a):
        if a.ndim == 2:
            return rows(DIL_GW)
        return pl.BlockSpec((a.shape[0], tm // a.shape[0], DIL_GW), lambda i: (0, i, 0))

    return pl.pallas_call(
        _outproj_odd_kernel,
        grid=(S // tm,),
        in_specs=[dil(a) for a in (*o_dil, *lse_dil)] + [rows(NSA_W)] * 3
                 + [gate(0), gate(1), gate(2), full(wd), full(wn), full(g), rows(D)],
        out_specs=rows(D),
        out_shape=jax.ShapeDtypeStruct((S, D), F32),
        scratch_shapes=[pltpu.VMEM((DIL_GW // LANES, tm, LANES), F32)],
        compiler_params=_cparams(("parallel",)),
    )(*o_dil, *lse_dil, o_cmp, o_sel, o_win, y, y, y, wd, wn, g, h)


def _mlp_ple_kernel(h_ref, g1_ref, wu_ref, wd_ref, g2_ref, g3_ref, wg_ref, p_ref, wp_ref, out_ref,
                    hn_ref, acc_ref):
    j = pl.program_id(1)

    @pl.when(j == 0)
    def _():
        hn_ref[...] = _rms(h_ref[...], g1_ref[...]).astype(BF16)
        acc_ref[...] = jnp.zeros(acc_ref.shape, F32)

    u = jnp.maximum(_dot(hn_ref[...], wu_ref[...]), 0.0)
    acc_ref[...] += _dot((u * u).astype(BF16), wd_ref[...])

    @pl.when(j == pl.num_programs(1) - 1)
    def _():
        h2 = h_ref[...] + _rms(acc_ref[...], g2_ref[...])
        gate = jax.nn.sigmoid(_dot(_rms(h2, g3_ref[...]).astype(BF16), wg_ref[...]))
        out_ref[...] = h2 + gate * _dot(p_ref[...].astype(BF16), wp_ref[...])


def mlp_ple(h, g1, wu, wd, g2, g3, wg, p, wp, tm, tf):
    S, D = h.shape
    FF = wu.shape[1]
    PD = p.shape[1]
    const = lambda a: pl.BlockSpec(a.shape, lambda i, j: (0,) * a.ndim)
    return pl.pallas_call(
        _mlp_ple_kernel,
        grid=(S // tm, FF // tf),
        in_specs=[pl.BlockSpec((tm, D), lambda i, j: (i, 0)), const(g1),
                  pl.BlockSpec((D, tf), lambda i, j: (0, j)),
                  pl.BlockSpec((tf, D), lambda i, j: (j, 0)),
                  const(g2), const(g3), const(wg),
                  pl.BlockSpec((tm, PD), lambda i, j: (i, 0)), const(wp)],
        out_specs=pl.BlockSpec((tm, D), lambda i, j: (i, 0)),
        out_shape=jax.ShapeDtypeStruct((S, D), F32),
        scratch_shapes=[pltpu.VMEM((tm, D), BF16), pltpu.VMEM((tm, D), F32)],
        compiler_params=_cparams(("parallel", "arbitrary")),
    )(h, g1, wu, wd, g2, g3, wg, p, wp)


def _t5_bucket_of(dist):
    n = jnp.maximum(dist, 0)
    max_exact = T5_BUCKETS // 2
    ratio = jnp.log(jnp.maximum(n, 1).astype(F32) / max_exact) / math.log(T5_MAX_DIST / max_exact)
    large = jnp.minimum(max_exact + (ratio * (T5_BUCKETS - max_exact)).astype(jnp.int32), T5_BUCKETS - 1)
    return jnp.where(n < max_exact, n, large)


def _bias_of_dist(bias, dist):
    bucket = _t5_bucket_of(dist)[None]
    out = jnp.zeros((bias.shape[1],) + dist.shape, F32)
    for b in range(T5_BUCKETS):
        out = jnp.where(bucket == b, bias[b].reshape((-1,) + (1,) * dist.ndim), out)
    return out


def _causal_tables(T, R):
    k = jnp.arange(R * T)[None, :, None]
    q = jnp.arange(T)[None, None, :]
    e = jnp.arange(R + 1)[:, None, None]
    return jnp.where(e * T + q >= k, 0.0, NEG_INF).astype(F32)[:, None]


def _t5_delta_tables(bias, T):
    n_delta = -(-(T5_FAR_DIST - 1) // T) + 2
    k = jnp.arange(T)[None, :, None]
    q = jnp.arange(T)[None, None, :]
    dist = jnp.arange(n_delta)[:, None, None] * T + q - k
    val = (_bias_of_dist(bias, dist) - bias[T5_BUCKETS - 1].reshape(-1, 1, 1, 1)) * LOG2E
    return jnp.transpose(jnp.where(dist[None] >= 0, val, NEG_INF), (1, 0, 2, 3))


def _band_table(bias, Ta, n_prev, max_rel, stride, inclusive):
    P = n_prev * Ta
    i = jnp.arange(Ta)[:, None]
    c = jnp.arange(P + Ta)[None, :]
    rel = i + P - c
    ok = (rel >= 0) & ((rel <= max_rel) if inclusive else (rel < max_rel))
    return jnp.where(ok[None], _bias_of_dist(bias, rel * stride), NEG_INF)


def _tile(S, pref):
    t = min(pref, S)
    assert S % t == 0
    return t


def even_mixer_core(h, g_pre, w_in, cos, sin, w_uq, q_norm, w_ukv, kv_norm, forget_bias, T):
    S = h.shape[0]
    scale = MLA_QK_DIM ** -0.5 * LOG2E
    half = MLA_ROPE_DIM // 2
    cos_t, sin_t = jnp.transpose(cos), jnp.transpose(sin)
    ones = jnp.ones((MLA_NOPE_DIM, S), F32)
    cq = jnp.concatenate([ones, cos_t, cos_t], axis=0) * scale
    sq = jnp.concatenate([0.0 * ones, -sin_t, sin_t], axis=0) * scale
    ck = jnp.concatenate([cos, cos], axis=1)
    sk = jnp.concatenate([-sin, sin], axis=1)
    wq = jnp.transpose(w_uq.reshape(MLA_Q_RANK, MLA_HEADS, MLA_QK_DIM), (1, 2, 0))
    swap = np.concatenate([np.arange(MLA_NOPE_DIM), MLA_NOPE_DIM + half + np.arange(half),
                           MLA_NOPE_DIM + np.arange(half)])
    wqs = wq[:, swap, :]
    wkv = jnp.transpose(w_ukv.reshape(MLA_KV_RANK, MLA_HEADS, MLA_NOPE_DIM + MLA_V_DIM), (1, 0, 2))
    wk = jnp.pad(wkv[:, :, :MLA_NOPE_DIM], ((0, 0), (0, 0), (0, MLA_ROPE_DIM)))
    wv = jnp.transpose(wkv[:, :, MLA_NOPE_DIM:], (0, 2, 1))
    e96 = jnp.pad(jnp.eye(MLA_ROPE_DIM, dtype=F32), ((0, 0), (MLA_NOPE_DIM, 0)))
    qm, km, vm, fq, fk, fv, logf, st = even_prep(
        h, g_pre, w_in, wq.astype(BF16), wqs.astype(BF16), wk.astype(BF16), e96.astype(BF16), wv.astype(BF16),
        jnp.eye(MLA_QK_DIM, dtype=BF16), q_norm[None, :], kv_norm[None, :], forget_bias[None, :],
        cq, sq, ck, sk, _tile(S, 512))
    neg_f = cumsum_split(jnp.transpose(logf))
    hps = MLA_HEADS
    causal = _causal_tables(T, 2 if S >= 2 * T else 1)
    r_mla, fast_mla = _reference_rows(st[0], st[1], st[2], 0.0, 0.0, 0.0, T, hps)
    o_mla = flash_causal(qm[None], r_mla, km, vm, causal, fast_mla, T, hps)
    r_fox, fast_fox = _reference_rows(st[3], st[4], st[5], jnp.sum(neg_f.astype(F32), axis=0), 0.0, 0.0, T, hps)
    fk_aug = jnp.concatenate([fk, jnp.transpose(neg_f, (1, 2, 0)),
                              jnp.zeros((FOX_HEADS, S, FOX_PAD - FOX_SPLIT), BF16), _stat_cols(FOX_HEADS, S)], axis=2)
    o_fox = flash_causal(fq[None], r_fox, fk_aug, fv, causal, fast_fox, T, hps)
    return o_mla, o_fox


def _odd_deint_plan():
    G = len(DIL_GROUPS)
    return tuple(((b * G + g) * DIL_GW, DIL_GW, d) for g, (_, d) in enumerate(DIL_GROUPS) if d > 1 for b in range(3))


def odd_mixer_core(y, deint, t5_bias, pos_k, pos_v, w1k, w2k, w1v, w2v, T):
    S = y.shape[0]
    G = len(DIL_GROUPS)
    Ta = 512
    o_dil, lse_dil = [], []
    for g, (w, d) in enumerate(DIL_GROUPS):
        ta = _tile(S // d, Ta)
        sb = min(BAND_SUB, ta)
        assert w // d <= sb <= ta
        btab = _band_table(t5_bias[:, g * DIL_HEADS:(g + 1) * DIL_HEADS], sb, 1, w // d, d, True)
        if d == 1:
            o, lse = band_attention(y[None], y[None], y[None], btab, ta, 1, True,
                                    widths=(DIL_GW, DIL_GW), col_blocks=(g, G + g, 2 * G + g))
            o, lse = o[0], lse[0]
        else:
            q, k, v = (deint[(b * G + g) * DIL_GW] for b in range(3))
            o, lse = band_attention(q, k, v, btab, ta, 1, True)
        o_dil.append(o)
        lse_dil.append(lse)
    nq_block = 3 * G
    gate_block = nq_block + 1
    base = (gate_block + 3) * NSA_W
    k_cmp, v_cmp, k_sel, v_sel, k_win, v_win = (y[:, base + i * HEAD_DIM:base + (i + 1) * HEAD_DIM] for i in range(6))
    bias_nsa = t5_bias[:, G * DIL_HEADS:]
    n_chunk = S // NSA_CMP_STRIDE
    cw = NSA_CMP_STRIDE * HEAD_DIM
    kc, vc = nsa_compress(k_cmp.reshape(n_chunk, cw), v_cmp.reshape(n_chunk, cw),
                          pos_k.reshape(2, cw), pos_v.reshape(2, cw),
                          w1k.reshape(2, cw, -1).astype(BF16), w2k.astype(BF16),
                          w1v.reshape(2, cw, -1).astype(BF16), jnp.transpose(w2v).astype(BF16))
    n_sel = S // NSA_SEL_BLOCK
    ci = np.arange(n_chunk)[None, :] * NSA_CMP_STRIDE
    sj = np.arange(n_sel)[:, None] * NSA_SEL_BLOCK
    overlap_t = jnp.asarray(((ci < sj + NSA_SEL_BLOCK) & (ci + NSA_CMP_LEN > sj)).astype(np.float32), BF16)
    nbg = min(n_sel, NSA_SEL_GROUP_BLOCKS)
    o_cmp, qaug, st, kaug, v_sel_t = nsa_cmp_topk(y, nq_block, k_sel.astype(BF16), v_sel.astype(BF16), kc, vc,
                                                  overlap_t, _tile(S, 256), nbg)
    delta = (bias_nsa - bias_nsa[T5_BUCKETS - 1]) * LOG2E
    r_sel, fast_sel = _reference_rows(st[0], st[1, :1], st[2], 0.0, jnp.max(delta, axis=0)[:, None],
                                      delta[0][:, None], T, NSA_HEADS)
    o_sel = flash_causal(qaug, r_sel, kaug[None], v_sel_t[None], _t5_delta_tables(bias_nsa, T), fast_sel, T,
                         hps=NSA_HEADS, key_group_tokens=nbg * NSA_SEL_BLOCK)
    Tw = _tile(S, NSA_WINDOW)
    n_prev = NSA_WINDOW // Tw
    wtab = _band_table(bias_nsa, Tw, n_prev, NSA_WINDOW, 1, False)
    o_win = band_attention(y[None], k_win[None], v_win[None], wtab, Tw, n_prev, False,
                           widths=(NSA_W, HEAD_DIM), col_blocks=(nq_block, 0, 0))[0]
    return o_dil, lse_dil, o_cmp, o_sel, o_win, gate_block


def _pad_cols(w, n):
    return jnp.pad(w, ((0, 0), (0, n - w.shape[1])))


def _even_w_in(w):
    cq, ckv, kr, fq, fk, fv, fl = jnp.split(w, np.cumsum([256, 128, 32, 512, 512, 512])[:], axis=1)
    half = MLA_ROPE_DIM // 2
    kr_sw = jnp.concatenate([kr[:, half:], kr[:, :half]], axis=1)
    head = _pad_cols(jnp.concatenate([cq, ckv, kr, kr_sw, fl], axis=1), 512)
    return jnp.concatenate([head, fq, fk, fv], axis=1).astype(BF16)


def _odd_w_in(w):
    n_main = 3 * 3 * DIL_GW + NSA_W
    main, six = w[:, :n_main], w[:, n_main:n_main + 6 * HEAD_DIM]
    gl = w[:, n_main + 6 * HEAD_DIM:]
    gl = jnp.transpose(gl.reshape(-1, NSA_HEADS, 3), (0, 2, 1))
    gl = jnp.repeat(gl[..., None], HEAD_DIM, axis=-1).reshape(w.shape[0], 3 * NSA_W)
    return _pad_cols(jnp.concatenate([main, gl, six], axis=1), ODD_IN_PAD).astype(BF16)


def _trunk(x, p, positions, t5_bias, ev_w_in, ev_q_norm, ev_w_uq, ev_kv_norm, ev_w_ukv, ev_forget_bias,
           ev_w_out, od_w_in, od_cmp_pos_k, od_cmp_pos_v, od_cmp_w1_k, od_cmp_w2_k, od_cmp_w1_v, od_cmp_w2_v,
           od_w_out, norm_mix_pre, norm_mix_post, norm_mlp_pre, norm_mlp_post, w_mlp_up, w_mlp_down,
           ple_norm, w_ple_gate, w_ple_proj):
    S, D = x.shape
    depth = p.shape[0]
    T = _tile(S, 512)
    tm = _tile(S, 512)
    inv_freq = ROPE_THETA ** (-jnp.arange(0, MLA_ROPE_DIM, 2, dtype=F32) / MLA_ROPE_DIM)
    angles = positions.astype(F32)[:, None] * inv_freq
    cos, sin = jnp.cos(angles), jnp.sin(angles)
    h = x
    for i in range(depth):
        j = i // 2
        if i % 2 == 0:
            o_mla, o_fox = even_mixer_core(h, norm_mix_pre[i][None], _even_w_in(ev_w_in[j]), cos, sin, ev_w_uq[j],
                                           ev_q_norm[j], ev_w_ukv[j], ev_kv_norm[j], ev_forget_bias[j], T)
            wo = ev_w_out[j].astype(BF16)
            na = MLA_HEADS * MLA_V_DIM
            h = outproj_even(o_mla, o_fox, wo[:na], wo[na:], norm_mix_post[i][None], h, tm)
        else:
            plan = _odd_deint_plan()
            y, *parts = odd_inproj(h, norm_mix_pre[i][None], _odd_w_in(od_w_in[j]), _tile(S, 256), plan)
            o_dil, lse_dil, o_cmp, o_sel, o_win, gate_block = odd_mixer_core(
                y, {col: a for (col, _, _), a in zip(plan, parts)}, t5_bias, od_cmp_pos_k[j], od_cmp_pos_v[j], od_cmp_w1_k[j], od_cmp_w2_k[j],
                od_cmp_w1_v[j], od_cmp_w2_v[j], T)
            wo = od_w_out[j].astype(BF16)
            h = outproj_odd(o_dil, lse_dil, o_cmp, o_sel, o_win, y, gate_block, wo[:DIL_GW], wo[DIL_GW:],
                            norm_mix_post[i][None], h, tm)
        h = mlp_ple(h, norm_mlp_pre[i][None], w_mlp_up[i].astype(BF16), w_mlp_down[i].astype(BF16),
                    norm_mlp_post[i][None], ple_norm[i][None], w_ple_gate[i].astype(BF16), p[i],
                    w_ple_proj[i].astype(BF16), _tile(S, 1024), 512)
    return h


def kernel(x, p, positions, t5_bias, ev_w_in, ev_q_norm, ev_w_uq, ev_kv_norm, ev_w_ukv, ev_forget_bias, ev_w_out, od_w_in, od_cmp_pos_k, od_cmp_pos_v, od_cmp_w1_k, od_cmp_w2_k, od_cmp_w1_v, od_cmp_w2_v, od_w_out, norm_mix_pre, norm_mix_post, norm_mlp_pre, norm_mlp_post, w_mlp_up, w_mlp_down, ple_norm, w_ple_gate, w_ple_proj):
    params = (t5_bias, ev_w_in, ev_q_norm, ev_w_uq, ev_kv_norm, ev_w_ukv, ev_forget_bias, ev_w_out, od_w_in,
              od_cmp_pos_k, od_cmp_pos_v, od_cmp_w1_k, od_cmp_w2_k, od_cmp_w1_v, od_cmp_w2_v, od_w_out,
              norm_mix_pre, norm_mix_post, norm_mlp_pre, norm_mlp_post, w_mlp_up, w_mlp_down, ple_norm,
              w_ple_gate, w_ple_proj)
    outs = [_trunk(x[b], p[:, b], positions[b], *params) for b in range(x.shape[0])]
    return jnp.stack(outs).astype(x.dtype)
```
